```python
import math
import jax, jax.numpy as jnp
from jax import lax
import numpy as np

D_MODEL = 1024
BATCH = 8
SEQ = 2048
DEPTH = 1
DEC_BATCH = 32
DEC_SEQ = 1
PAST_LEN = 8192
PAGE_SIZE = 128

D_MIX = D_MODEL
D_ATT = D_MIX // 2
D_SSM = D_MIX - D_ATT
ATT_HEAD_DIM = 64
N_ATT_HEADS = D_ATT // (2 * ATT_HEAD_DIM)
KD = 2 * ATT_HEAD_DIM
V_HEAD_DIM = 2 * ATT_HEAD_DIM
ROT_DIM = ATT_HEAD_DIM // 4
ROPE_THETA = 500000.0
SSM_GROUP = 16
N_SSM_GROUPS = D_SSM // SSM_GROUP
SSM_STATE = 64
D_FF = 2816
CONV_W = 3
Q_BLOCK = 128
NORM_EPS = 1e-6
SUBLN_EPS = 1e-5
NEG_INF = -1e30
D_PROJ = 3 * D_ATT + D_SSM

kernel_name = "hybrid_diffattn_s5_convffn_step"


def rmsnorm(x, g, eps=NORM_EPS):
    xf = x.astype(jnp.float32)
    r = lax.rsqrt(jnp.mean(xf * xf, axis=-1, keepdims=True) + eps)
    return (xf * r * g.astype(jnp.float32)).astype(x.dtype)


def lambda_init_for(layer_idx):
    return 0.8 - 0.6 * math.exp(-0.3 * layer_idx)


def rope_partial(x, pos):
    half = ROT_DIM // 2
    inv = ROPE_THETA ** (-jnp.arange(0, ROT_DIM, 2, dtype=jnp.float32) / ROT_DIM)
    ang = pos.astype(jnp.float32)[:, None] * inv[None, :]
    cos = jnp.cos(ang)[None, :, None, None, :]
    sin = jnp.sin(ang)[None, :, None, None, :]
    xr = x[..., :ROT_DIM].astype(jnp.float32)
    x1, x2 = xr[..., :half], xr[..., half:]
    rot = jnp.concatenate([x1 * cos - x2 * sin, x2 * cos + x1 * sin], axis=-1)
    return jnp.concatenate([rot.astype(x.dtype), x[..., ROT_DIM:]], axis=-1)


def diff_attend(q, k, v, q_pos, k_pos, lam):
    s = jnp.einsum('bqhcd,bkhcd->bchqk', q.astype(jnp.float32), k.astype(jnp.float32)) * (ATT_HEAD_DIM ** -0.5)
    mask = k_pos[None, :] <= q_pos[:, None]
    s = jnp.where(mask, s, NEG_INF)
    p = jax.nn.softmax(s, axis=-1)
    pd = p[:, 0] - lam * p[:, 1]
    return jnp.einsum('bhqk,bkhe->bqhe', pd, v.astype(jnp.float32))


def prompt_attend(q, k, v, lam):
    B, S = q.shape[0], q.shape[1]
    nb = S // Q_BLOCK
    k_pos = jnp.arange(S, dtype=jnp.int32)
    qb = jnp.moveaxis(q.reshape(B, nb, Q_BLOCK, N_ATT_HEADS, 2, ATT_HEAD_DIM), 1, 0)
    pb = k_pos.reshape(nb, Q_BLOCK)
    o = lax.map(lambda a: diff_attend(a[0], k, v, a[1], k_pos, lam), (qb, pb))
    return jnp.moveaxis(o, 0, 1).reshape(B, S, N_ATT_HEADS, V_HEAD_DIM)


def make_sample_attend(k_past, v_past, past_len):
    def attend(q, k, v, lam):
        T = q.shape[1]
        kk = jnp.concatenate([k_past.astype(k.dtype), k], axis=1)
        vv = jnp.concatenate([v_past.astype(v.dtype), v], axis=1)
        k_pos = jnp.arange(past_len + T, dtype=jnp.int32)
        q_pos = past_len + jnp.arange(T, dtype=jnp.int32)
        return diff_attend(q, kk, vv, q_pos, k_pos, lam)
    return attend


def _complex_scan_combine(e1, e2):
    a1r, a1i, b1r, b1i = e1
    a2r, a2i, b2r, b2i = e2
    ar = a2r * a1r - a2i * a1i
    ai = a2r * a1i + a2i * a1r
    br = a2r * b1r - a2i * b1i + b2r
    bi = a2r * b1i + a2i * b1r + b2i
    return (ar, ai, br, bi)


def ssm_mixer(u, h0_re, h0_im, a_re, a_im, log_dt, b_re, b_im, c_re, c_im, d_skip, w_glu):
    B, S, _ = u.shape
    f32 = jnp.float32
    ug = u.astype(f32).reshape(B, S, N_SSM_GROUPS, SSM_GROUP)
    ar_, ai_ = a_re.astype(f32), a_im.astype(f32)
    dt = jnp.exp(log_dt.astype(f32))[:, None]
    mag = jnp.exp(ar_ * dt)
    lb_re = mag * jnp.cos(ai_ * dt)
    lb_im = mag * jnp.sin(ai_ * dt)
    den = ar_ * ar_ + ai_ * ai_
    nr, ni = lb_re - 1.0, lb_im
    f_re = (nr * ar_ + ni * ai_) / den
    f_im = (ni * ar_ - nr * ai_) / den
    br_, bi_ = b_re.astype(f32), b_im.astype(f32)
    bb_re = f_re[..., None] * br_ - f_im[..., None] * bi_
    bb_im = f_re[..., None] * bi_ + f_im[..., None] * br_
    bu_re = jnp.einsum('bsgc,gnc->bsgn', ug, bb_re)
    bu_im = jnp.einsum('bsgc,gnc->bsgn', ug, bb_im)
    a_full_re = jnp.broadcast_to(lb_re, bu_re.shape)
    a_full_im = jnp.broadcast_to(lb_im, bu_im.shape)
    acr, aci, bcr, bci = lax.associative_scan(_complex_scan_combine, (a_full_re, a_full_im, bu_re, bu_im), axis=1)
    h_re = acr * h0_re[:, None] - aci * h0_im[:, None] + bcr
    h_im = acr * h0_im[:, None] + aci * h0_re[:, None] + bci
    y = (jnp.einsum('bsgn,gcn->bsgc', h_re, c_re.astype(f32))
         - jnp.einsum('bsgn,gcn->bsgc', h_im, c_im.astype(f32))
         + d_skip.astype(f32).reshape(N_SSM_GROUPS, SSM_GROUP) * ug)
    y = jax.nn.gelu(y.reshape(B, S, D_SSM))
    z = y * jax.nn.sigmoid(y @ w_glu.astype(f32))
    return z.astype(u.dtype), h_re[:, -1], h_im[:, -1]


def conv_ffn(x, buf, w_up, conv_w, conv_b, w_down):
    S = x.shape[1]
    h = x @ w_up
    a, g = h[..., :D_FF], h[..., D_FF:]
    ap = jnp.concatenate([buf.astype(a.dtype), a], axis=1)
    c = conv_b + sum(conv_w[j] * ap[:, j:j + S] for j in range(CONV_W))
    out = (jax.nn.silu(c) * g) @ w_down
    return out, ap[:, S:]


def decoder_layer(x, pos, attend, h0_re, h0_im, conv_buf, lambda_init, lw):
    B, S, _ = x.shape
    hn = rmsnorm(x, lw['norm_mix'])
    proj = hn @ lw['w_in']
    q = proj[..., :D_ATT].reshape(B, S, N_ATT_HEADS, 2, ATT_HEAD_DIM)
    k = proj[..., D_ATT:2 * D_ATT].reshape(B, S, N_ATT_HEADS, 2, ATT_HEAD_DIM)
    v = proj[..., 2 * D_ATT:3 * D_ATT].reshape(B, S, N_ATT_HEADS, V_HEAD_DIM)
    u = proj[..., 3 * D_ATT:]
    q = rope_partial(q, pos)
    k = rope_partial(k, pos)
    f32 = jnp.float32
    lam = (jnp.exp(jnp.sum(lw['lambda_q1'].astype(f32) * lw['lambda_k1'].astype(f32)))
           - jnp.exp(jnp.sum(lw['lambda_q2'].astype(f32) * lw['lambda_k2'].astype(f32)))
           + lambda_init)
    o = attend(q, k, v, lam)
    o = rmsnorm(o, lw['subln_g'], SUBLN_EPS) * (1.0 - lambda_init)
    o = o.reshape(B, S, D_ATT).astype(x.dtype)
    z, hT_re, hT_im = ssm_mixer(u, h0_re, h0_im, lw['ssm_a_re'], lw['ssm_a_im'], lw['ssm_log_dt'],
                                lw['ssm_b_re'], lw['ssm_b_im'], lw['ssm_c_re'], lw['ssm_c_im'],
                                lw['ssm_d'], lw['w_glu'])
    x = x + jnp.concatenate([o, z], axis=-1) @ lw['w_o']
    f, new_buf = conv_ffn(rmsnorm(x, lw['norm_ffn']), conv_buf, lw['w_up'], lw['conv_w'], lw['conv_b'], lw['w_down'])
    x = x + f
    return x, k.reshape(B, S, N_ATT_HEADS, KD), v, hT_re, hT_im, new_buf


def setup_inputs(seed: int = 0) -> dict:
    key = jax.random.key(seed)
    ks = jax.random.split(key, 32)
    f32 = jnp.float32
    n_pages = PAST_LEN // PAGE_SIZE
    n_used = DEC_BATCH * n_pages
    n_pool = (n_used * 5 + 3) // 4
    nrm = lambda k, shape, s: jax.random.normal(k, shape, f32) * s
    page_table = jax.random.permutation(ks[0], n_pool)[:n_used].reshape(DEC_BATCH, n_pages).astype(jnp.int32)
    n_idx = jnp.arange(SSM_STATE, dtype=f32)
    return {
        'x_prompt': nrm(ks[1], (BATCH, SEQ, D_MODEL), 1.0),
        'x_sample': nrm(ks[2], (DEC_BATCH, DEC_SEQ, D_MODEL), 1.0),
        'cache_k': nrm(ks[3], (DEPTH, n_pool, PAGE_SIZE, N_ATT_HEADS, KD), 1.0),
        'cache_v': nrm(ks[4], (DEPTH, n_pool, PAGE_SIZE, N_ATT_HEADS, V_HEAD_DIM), 1.0),
        'state_ssm_re': nrm(ks[5], (DEPTH, DEC_BATCH, N_SSM_GROUPS, SSM_STATE), 0.5),
        'state_ssm_im': nrm(ks[6], (DEPTH, DEC_BATCH, N_SSM_GROUPS, SSM_STATE), 0.5),
        'state_conv': nrm(ks[7], (DEPTH, DEC_BATCH, CONV_W - 1, D_FF), 1.0),
        'page_table': page_table,
        'norm_mix': 1.0 + nrm(ks[8], (DEPTH, D_MODEL), 0.02),
        'w_in': nrm(ks[9], (DEPTH, D_MODEL, D_PROJ), D_MODEL ** -0.5),
        'lambda_q1': nrm(ks[10], (DEPTH, ATT_HEAD_DIM), 0.1),
        'lambda_k1': nrm(ks[11], (DEPTH, ATT_HEAD_DIM), 0.1),
        'lambda_q2': nrm(ks[12], (DEPTH, ATT_HEAD_DIM), 0.1),
        'lambda_k2': nrm(ks[13], (DEPTH, ATT_HEAD_DIM), 0.1),
        'subln_g': 1.0 + nrm(ks[14], (DEPTH, V_HEAD_DIM), 0.02),
        'ssm_a_re': -0.5 + nrm(ks[15], (DEPTH, N_SSM_GROUPS, SSM_STATE), 0.01),
        'ssm_a_im': math.pi * n_idx + nrm(ks[16], (DEPTH, N_SSM_GROUPS, SSM_STATE), 0.01),
        'ssm_log_dt': jax.random.uniform(ks[17], (DEPTH, N_SSM_GROUPS), f32, math.log(1e-3), math.log(1e-1)),
        'ssm_b_re': nrm(ks[18], (DEPTH, N_SSM_GROUPS, SSM_STATE, SSM_GROUP), (2.0 * SSM_GROUP) ** -0.5),
        'ssm_b_im': nrm(ks[19], (DEPTH, N_SSM_GROUPS, SSM_STATE, SSM_GROUP), (2.0 * SSM_GROUP) ** -0.5),
        'ssm_c_re': nrm(ks[20], (DEPTH, N_SSM_GROUPS, SSM_GROUP, SSM_STATE), (2.0 * SSM_STATE) ** -0.5),
        'ssm_c_im': nrm(ks[21], (DEPTH, N_SSM_GROUPS, SSM_GROUP, SSM_STATE), (2.0 * SSM_STATE) ** -0.5),
        'ssm_d': nrm(ks[22], (DEPTH, D_SSM), 1.0),
        'w_glu': nrm(ks[23], (DEPTH, D_SSM, D_SSM), D_SSM ** -0.5),
        'w_o': nrm(ks[24], (DEPTH, D_MIX, D_MODEL), D_MIX ** -0.5),
        'norm_ffn': 1.0 + nrm(ks[25], (DEPTH, D_MODEL), 0.02),
        'w_up': nrm(ks[26], (DEPTH, D_MODEL, 2 * D_FF), D_MODEL ** -0.5),
        'conv_w': nrm(ks[27], (DEPTH, CONV_W, D_FF), CONV_W ** -0.5),
        'conv_b': nrm(ks[28], (DEPTH, D_FF), 0.02),
        'w_down': nrm(ks[29], (DEPTH, D_FF, D_MODEL), D_FF ** -0.5),
        'final_norm': 1.0 + nrm(ks[30], (D_MODEL,), 0.02),
    }


def reference(x_prompt, x_sample, cache_k, cache_v, state_ssm_re, state_ssm_im, state_conv, page_table,
              norm_mix, w_in, lambda_q1, lambda_k1, lambda_q2, lambda_k2, subln_g,
              ssm_a_re, ssm_a_im, ssm_log_dt, ssm_b_re, ssm_b_im, ssm_c_re, ssm_c_im, ssm_d, w_glu,
              w_o, norm_ffn, w_up, conv_w, conv_b, w_down, final_norm):
    Bp, Sp = x_prompt.shape[0], x_prompt.shape[1]
    Bs, Ts = x_sample.shape[0], x_sample.shape[1]
    n_pages = page_table.shape[1]
    past_len = n_pages * cache_k.shape[2]
    pos_p = jnp.arange(Sp, dtype=jnp.int32)
    pos_s = past_len + jnp.arange(Ts, dtype=jnp.int32)
    xp, xs = x_prompt, x_sample
    kp_l, vp_l, hrp_l, hip_l, cp_l = [], [], [], [], []
    ks_l, vs_l, hrs_l, his_l, cs_l = [], [], [], [], []
    for i in range(DEPTH):
        lw = {
            'norm_mix': norm_mix[i], 'w_in': w_in[i],
            'lambda_q1': lambda_q1[i], 'lambda_k1': lambda_k1[i],
            'lambda_q2': lambda_q2[i], 'lambda_k2': lambda_k2[i], 'subln_g': subln_g[i],
            'ssm_a_re': ssm_a_re[i], 'ssm_a_im': ssm_a_im[i], 'ssm_log_dt': ssm_log_dt[i],
            'ssm_b_re': ssm_b_re[i], 'ssm_b_im': ssm_b_im[i], 'ssm_c_re': ssm_c_re[i], 'ssm_c_im': ssm_c_im[i],
            'ssm_d': ssm_d[i], 'w_glu': w_glu[i], 'w_o': w_o[i], 'norm_ffn': norm_ffn[i],
            'w_up': w_up[i], 'conv_w': conv_w[i], 'conv_b': conv_b[i], 'w_down': w_down[i],
        }
        lam_init = lambda_init_for(i)
        h0 = jnp.zeros((Bp, N_SSM_GROUPS, SSM_STATE), jnp.float32)
        buf0 = jnp.zeros((Bp, CONV_W - 1, D_FF), xp.dtype)
        xp, kp, vp, hrp, hip, cbp = decoder_layer(xp, pos_p, prompt_attend, h0, h0, buf0, lam_init, lw)
        k_past = cache_k[i][page_table].reshape(Bs, past_len, N_ATT_HEADS, 2, ATT_HEAD_DIM)
        v_past = cache_v[i][page_table].reshape(Bs, past_len, N_ATT_HEADS, V_HEAD_DIM)
        attend_s = make_sample_attend(k_past, v_past, past_len)
        xs, kss, vss, hrs, his, cbs = decoder_layer(xs, pos_s, attend_s,
                                                    state_ssm_re[i].astype(jnp.float32),
                                                    state_ssm_im[i].astype(jnp.float32),
                                                    state_conv[i], lam_init, lw)
        kp_l.append(kp); vp_l.append(vp); hrp_l.append(hrp); hip_l.append(hip); cp_l.append(cbp)
        ks_l.append(kss); vs_l.append(vss); hrs_l.append(hrs); his_l.append(his); cs_l.append(cbs)
    y_prompt = rmsnorm(xp, final_norm)
    y_sample = rmsnorm(xs, final_norm)
    return (y_prompt, y_sample,
            jnp.stack(kp_l), jnp.stack(vp_l), jnp.stack(hrp_l), jnp.stack(hip_l), jnp.stack(cp_l),
            jnp.stack(ks_l), jnp.stack(vs_l), jnp.stack(hrs_l), jnp.stack(his_l), jnp.stack(cs_l))
```

```python
import functools
import math

import jax
import jax.numpy as jnp
from jax import lax
from jax.experimental import pallas as pl
from jax.experimental.pallas import tpu as pltpu

F32 = jnp.float32
BF16 = jnp.bfloat16

D_MODEL = 1024
D_ATT = 512
D_SSM = 512
ATT_HEAD_DIM = 64
N_ATT_HEADS = 4
HEAD_W = 2 * ATT_HEAD_DIM
ROT_DIM = 16
ROPE_THETA = 500000.0
SSM_GROUP = 16
N_SSM_GROUPS = 32
SSM_STATE = 64
N_STATE = N_SSM_GROUPS * SSM_STATE
D_FF = 2816
CONV_W = 3
NORM_EPS = 1e-6
SUBLN_EPS = 1e-5
NEG_INF = -1e30
D_PROJ = 3 * D_ATT + D_SSM
LAMBDA_INIT = 0.8 - 0.6 * math.exp(-0.3 * 0)

V7X_VMEM_LIMIT = 56 * 1024 * 1024
FF_CHUNK = 256
N_FF_CHUNKS = D_FF // FF_CHUNK
PAGES_PER_STEP = 8
CARRY_ROWS = 8


def _params(*sem):
    return pltpu.CompilerParams(dimension_semantics=sem, vmem_limit_bytes=V7X_VMEM_LIMIT)


def _lambda(lq1_ref, lk1_ref, lq2_ref, lk2_ref):
    s1 = jnp.sum(lq1_ref[...] * lk1_ref[...], axis=-1, keepdims=True)
    s2 = jnp.sum(lq2_ref[...] * lk2_ref[...], axis=-1, keepdims=True)
    return jnp.exp(s1) - jnp.exp(s2) + LAMBDA_INIT


def _sub_ln(o, g):
    r = lax.rsqrt(jnp.mean(o * o, axis=-1, keepdims=True) + SUBLN_EPS)
    return (o * r * g) * (1.0 - LAMBDA_INIT)


def _ssm_prep_kernel(ar_ref, ai_ref, ldt_ref, art_ref, ait_ref, brt_ref, bit_ref,
                     lbre_ref, lbim_ref, bbre_ref, bbim_ref):
    dt = jnp.exp(ldt_ref[...])

    def disc(ar, ai):
        mag = jnp.exp(ar * dt)
        return mag * jnp.cos(ai * dt), mag * jnp.sin(ai * dt)

    lb_re, lb_im = disc(ar_ref[...], ai_ref[...])
    lbre_ref[...] = lb_re
    lbim_ref[...] = lb_im
    ar, ai = art_ref[...], ait_ref[...]
    t_re, t_im = disc(ar, ai)
    den = ar * ar + ai * ai
    nr, ni = t_re - 1.0, t_im
    f_re = (nr * ar + ni * ai) / den
    f_im = (ni * ar - nr * ai) / den
    br, bi = brt_ref[...], bit_ref[...]
    bbre_ref[...] = f_re * br - f_im * bi
    bbim_ref[...] = f_re * bi + f_im * br


def _ssm_prep(a_re, a_im, log_dt, b_re, b_im):
    G, N, C = N_SSM_GROUPS, SSM_STATE, SSM_GROUP
    tile = lambda a: jnp.tile(a, (1, C))
    flat = lambda b: jnp.transpose(b, (0, 2, 1)).reshape(G, C * N)
    outs = pl.pallas_call(
        _ssm_prep_kernel,
        out_shape=[jax.ShapeDtypeStruct((G, N), F32)] * 2 + [jax.ShapeDtypeStruct((G, C * N), F32)] * 2,
        name="ssm_prep",
    )(a_re, a_im, log_dt.reshape(G, 1), tile(a_re), tile(a_im), flat(b_re), flat(b_im))
    return outs


def _in_proj_kernel(x_ref, g_ref, w_ref, cos_ref, sa_ref, sb_ref, q_ref, k_ref, v_ref, u_ref):
    x = x_ref[...]
    r = lax.rsqrt(jnp.mean(x * x, axis=-1, keepdims=True) + NORM_EPS)
    hn = (x * r * g_ref[...]).astype(BF16)
    proj = jnp.dot(hn, w_ref[...], preferred_element_type=F32)
    cos, sa, sb = cos_ref[...], sa_ref[...], sb_ref[...]
    half = ROT_DIM // 2

    def rope(xh):
        return xh * cos + pltpu.roll(xh, HEAD_W - half, 1) * sa + pltpu.roll(xh, half, 1) * sb

    for h in range(N_ATT_HEADS):
        lo, hi = h * HEAD_W, (h + 1) * HEAD_W
        q_ref[:, lo:hi] = rope(proj[:, lo:hi]) * (ATT_HEAD_DIM ** -0.5)
        k_ref[:, lo:hi] = rope(proj[:, D_ATT + lo:D_ATT + hi])
    v_ref[...] = proj[:, 2 * D_ATT:3 * D_ATT]
    u_ref[...] = proj[:, 3 * D_ATT:]


def _rope_tables(pos):
    half = ROT_DIM // 2
    inv = ROPE_THETA ** (-jnp.arange(0, ROT_DIM, 2, dtype=F32) / ROT_DIM)
    ang = pos.astype(F32)[:, None] * inv[None, :]
    cos, sin = jnp.cos(ang), jnp.sin(ang)
    S = pos.shape[0]
    pad = jnp.zeros((S, ATT_HEAD_DIM - ROT_DIM), F32)
    zh = jnp.zeros((S, half), F32)
    comp = lambda a, b, c: jnp.concatenate([a, b, c], axis=1)
    two = lambda t: jnp.concatenate([t, t], axis=1)
    return (two(comp(cos, cos, pad + 1.0)), two(comp(-sin, zh, pad)), two(comp(zh, sin, pad)))


def _in_proj(x, g, w_bf16, tables, tm):
    B, S, D = x.shape
    ns = S // tm
    row = lambda b, s: (b * ns + s, 0)
    tab = pl.BlockSpec((tm, HEAD_W), lambda b, s: (s, 0))
    qkv_spec = pl.BlockSpec((tm, D_ATT), row)
    q, k, v, u = pl.pallas_call(
        _in_proj_kernel,
        grid=(B, ns),
        in_specs=[pl.BlockSpec((None, tm, D), lambda b, s: (b, s, 0)),
                  pl.BlockSpec((1, D), lambda b, s: (0, 0)),
                  pl.BlockSpec((D, D_PROJ), lambda b, s: (0, 0)),
                  tab, tab, tab],
        out_specs=[qkv_spec, qkv_spec, qkv_spec, pl.BlockSpec((tm, D_SSM), lambda b, s: (s, b))],
        out_shape=[jax.ShapeDtypeStruct((B * S, D_ATT), F32)] * 3 + [jax.ShapeDtypeStruct((S, B * D_SSM), F32)],
        compiler_params=_params("parallel", "parallel"),
        name="in_proj",
    )(x, g.reshape(1, D), w_bf16, *tables)
    return q, k, v, u


def _attn_kernel(lq1_ref, lk1_ref, lq2_ref, lk2_ref, g_ref, q_ref, k_ref, v_ref, o_ref, *, tq):
    i = pl.program_id(2)
    lam = _lambda(lq1_ref, lk1_ref, lq2_ref, lk2_ref)
    q = q_ref[...]
    lane = lax.broadcasted_iota(jnp.int32, q.shape, 1)
    qs = jnp.concatenate([jnp.where(lane < ATT_HEAD_DIM, q, 0.0), jnp.where(lane >= ATT_HEAD_DIM, q, 0.0)], axis=0)

    def block(j, carry, masked):
        m, l, acc = carry
        start = pl.multiple_of(j * tq, tq)
        kj = k_ref[pl.ds(start, tq), :]
        vj = v_ref[pl.ds(start, tq), :]
        s = lax.dot_general(qs, kj, (((1,), (1,)), ((), ())), preferred_element_type=F32)
        if masked:
            r = lax.broadcasted_iota(jnp.int32, (tq, tq), 0)
            c = lax.broadcasted_iota(jnp.int32, (tq, tq), 1)
            keep = jnp.concatenate([c <= r, c <= r], axis=0)
            s = jnp.where(keep, s, NEG_INF)
        m_new = jnp.maximum(m, jnp.max(s, axis=-1, keepdims=True))
        alpha = jnp.exp(m - m_new)
        p = jnp.exp(s - m_new)
        l = alpha * l + jnp.sum(p, axis=-1, keepdims=True)
        acc = alpha * acc + jnp.dot(p, vj, preferred_element_type=F32)
        return m_new, l, acc

    init = (jnp.full((2 * tq, 1), NEG_INF, F32), jnp.zeros((2 * tq, 1), F32), jnp.zeros((2 * tq, HEAD_W), F32))
    carry = lax.fori_loop(0, i, lambda j, c: block(j, c, False), init)
    m, l, acc = block(i, carry, True)
    o = acc[:tq] / l[:tq] - lam * (acc[tq:] / l[tq:])
    o_ref[...] = _sub_ln(o, g_ref[...])


def _attn(q, k, v, lam_vecs, g, B, S, tq):
    nq = S // tq
    vec = pl.BlockSpec((1, ATT_HEAD_DIM), lambda b, h, i: (0, 0))
    kv = pl.BlockSpec((S, HEAD_W), lambda b, h, i: (b, h))
    qo = pl.BlockSpec((tq, HEAD_W), lambda b, h, i: (b * nq + i, h))
    return pl.pallas_call(
        functools.partial(_attn_kernel, tq=tq),
        grid=(B, N_ATT_HEADS, nq),
        in_specs=[vec, vec, vec, vec, pl.BlockSpec((1, HEAD_W), lambda b, h, i: (0, 0)), qo, kv, kv],
        out_specs=qo,
        out_shape=jax.ShapeDtypeStruct((B * S, D_ATT), F32),
        compiler_params=_params("parallel", "parallel", "parallel"),
        name="attn",
    )(*lam_vecs, g, q, k, v)


def _decode_kernel(pt_ref, lq1_ref, lk1_ref, lq2_ref, lk2_ref, g_ref, q_ref, kn_ref, vn_ref, *rest):
    del pt_ref
    P = PAGES_PER_STEP
    k_refs, v_refs = rest[:P], rest[P:2 * P]
    o_ref, m_scr, l_scr, acc_scr, q8_scr = rest[2 * P:]
    j = pl.program_id(1)
    H = N_ATT_HEADS

    @pl.when(j == 0)
    def _():
        m_scr[...] = jnp.full(m_scr.shape, NEG_INF, F32)
        l_scr[...] = jnp.zeros(l_scr.shape, F32)
        acc_scr[...] = jnp.zeros(acc_scr.shape, F32)
        r = lax.broadcasted_iota(jnp.int32, (2 * H, D_ATT), 0)
        lane = lax.broadcasted_iota(jnp.int32, (2 * H, D_ATT), 1)
        want = (r % H) * HEAD_W + (r // H) * ATT_HEAD_DIM
        keep = (lane >= want) & (lane < want + ATT_HEAD_DIM)
        q8_scr[...] = jnp.where(keep, jnp.broadcast_to(q_ref[...], (2 * H, D_ATT)), 0.0)

    q8 = q8_scr[...]
    s = jnp.concatenate(
        [lax.dot_general(q8, kr[...], (((1,), (1,)), ((), ())), preferred_element_type=F32) for kr in k_refs],
        axis=1)
    m = m_scr[...]
    m_new = jnp.maximum(m, jnp.max(s, axis=-1, keepdims=True))
    alpha = jnp.exp(m - m_new)
    p = jnp.exp(s - m_new)
    page = k_refs[0].shape[0]
    pv = jnp.dot(p[:, :page], v_refs[0][...], preferred_element_type=F32)
    for i in range(1, P):
        pv = pv + jnp.dot(p[:, i * page:(i + 1) * page], v_refs[i][...], preferred_element_type=F32)
    m_scr[...] = m_new
    l_scr[...] = alpha * l_scr[...] + jnp.sum(p, axis=-1, keepdims=True)
    acc_scr[...] = alpha * acc_scr[...] + pv

    @pl.when(j == pl.num_programs(1) - 1)
    def _():
        lam = _lambda(lq1_ref, lk1_ref, lq2_ref, lk2_ref)
        s_new = jnp.sum(q8 * kn_ref[...], axis=-1, keepdims=True)
        m_old = m_scr[...]
        m_fin = jnp.maximum(m_old, s_new)
        a = jnp.exp(m_old - m_fin)
        p_new = jnp.exp(s_new - m_fin)
        l = a * l_scr[...] + p_new
        acc = (a * acc_scr[...] + p_new * vn_ref[...]) / l
        g = g_ref[...]
        for h in range(H):
            lo, hi = h * HEAD_W, (h + 1) * HEAD_W
            o = acc[h:h + 1, lo:hi] - lam * acc[H + h:H + h + 1, lo:hi]
            o_ref[:, lo:hi] = _sub_ln(o, g)


def _decode_attn(q, k_new, v_new, cache_k, cache_v, page_table, lam_vecs, g):
    Bs, n_pages = page_table.shape
    n_pool, page = cache_k.shape[0], cache_k.shape[1]
    P = PAGES_PER_STEP
    ck = cache_k.reshape(n_pool, page, D_ATT)
    cv = cache_v.reshape(n_pool, page, D_ATT)
    vec = pl.BlockSpec((1, ATT_HEAD_DIM), lambda b, j, pt: (0, 0))
    tok = pl.BlockSpec((None, 1, D_ATT), lambda b, j, pt: (b, 0, 0))
    page_spec = lambda i: pl.BlockSpec((None, page, D_ATT), lambda b, j, pt: (pt[b, j * P + i], 0, 0))
    grid_spec = pltpu.PrefetchScalarGridSpec(
        num_scalar_prefetch=1,
        grid=(Bs, n_pages // P),
        in_specs=[vec, vec, vec, vec, pl.BlockSpec((1, HEAD_W), lambda b, j, pt: (0, 0)), tok, tok, tok]
                 + [page_spec(i) for i in range(P)] * 2,
        out_specs=tok,
        scratch_shapes=[pltpu.VMEM((2 * N_ATT_HEADS, 1), F32), pltpu.VMEM((2 * N_ATT_HEADS, 1), F32),
                        pltpu.VMEM((2 * N_ATT_HEADS, D_ATT), F32), pltpu.VMEM((2 * N_ATT_HEADS, D_ATT), F32)],
    )
    tok3 = lambda a: a.reshape(Bs, 1, D_ATT)
    o = pl.pallas_call(
        _decode_kernel,
        grid_spec=grid_spec,
        out_shape=jax.ShapeDtypeStruct((Bs, 1, D_ATT), F32),
        compiler_params=_params("parallel", "arbitrary"),
        name="decode_attn",
    )(page_table, *lam_vecs, g, tok3(q), tok3(k_new), tok3(v_new), *([ck] * P), *([cv] * P))
    return o.reshape(Bs, D_ATT)


SCAN_COLS = 512


def _ssm_kernel(u_ref, bb_ref, lre_ref, lim_ref, c_ref, d_ref, wglu_ref, h0re_ref, h0im_ref,
                z_ref, hre_ref, him_ref, bu_scr, hs_scr, *, tc, bk):
    t0 = pl.program_id(0)

    @pl.when(t0 == 0)
    def _():
        hre_ref[...] = h0re_ref[...]
        him_ref[...] = h0im_ref[...]

    u = u_ref[...].reshape(tc * bk, D_SSM)
    bu = jnp.dot(u.astype(BF16), bb_ref[...], preferred_element_type=F32)
    bu_scr[...] = bu.reshape(tc, bk, 2 * N_STATE)

    for cg in range(N_STATE // SCAN_COLS):
        re = slice(cg * SCAN_COLS, (cg + 1) * SCAN_COLS)
        im = slice(N_STATE + cg * SCAN_COLS, N_STATE + (cg + 1) * SCAN_COLS)
        lr = jnp.broadcast_to(lre_ref[:, re], (bk, SCAN_COLS))
        li = jnp.broadcast_to(lim_ref[:, re], (bk, SCAN_COLS))

        def step(t, carry):
            hr, hi = carry
            nhr = lr * hr - li * hi + bu_scr[t, :, re]
            nhi = lr * hi + li * hr + bu_scr[t, :, im]
            hs_scr[t, :, re] = nhr
            hs_scr[t, :, im] = nhi
            return nhr, nhi

        hr, hi = lax.fori_loop(0, tc, step, (hre_ref[:, re], him_ref[:, re]))
        hre_ref[:, re] = hr
        him_ref[:, re] = hi

    hs = hs_scr[...].reshape(tc * bk, 2 * N_STATE)
    y = jnp.dot(hs.astype(BF16), c_ref[...], preferred_element_type=F32) + d_ref[...] * u
    y = jax.nn.gelu(y)
    gate = jnp.dot(y.astype(BF16), wglu_ref[...], preferred_element_type=F32)
    z_ref[...] = (y * jax.nn.sigmoid(gate)).reshape(tc, bk, D_SSM)


def _ssm(u3, bb, lb_re, lb_im, c_full, d, w_glu, h0_re, h0_im, tc):
    S, bk, _ = u3.shape
    const = lambda shape: pl.BlockSpec(shape, lambda t: (0,) * len(shape))
    seq = pl.BlockSpec((tc, bk, D_SSM), lambda t: (t, 0, 0))
    z, h_re, h_im = pl.pallas_call(
        functools.partial(_ssm_kernel, tc=tc, bk=bk),
        grid=(S // tc,),
        in_specs=[seq, const((D_SSM, 2 * N_STATE)), const((1, N_STATE)), const((1, N_STATE)),
                  const((2 * N_STATE, D_SSM)), const((1, D_SSM)), const((D_SSM, D_SSM)),
                  const((bk, N_STATE)), const((bk, N_STATE))],
        out_specs=[seq, const((bk, N_STATE)), const((bk, N_STATE))],
        out_shape=[jax.ShapeDtypeStruct((S, bk, D_SSM), F32), jax.ShapeDtypeStruct((bk, N_STATE), F32),
                   jax.ShapeDtypeStruct((bk, N_STATE), F32)],
        scratch_shapes=[pltpu.VMEM((tc, bk, 2 * N_STATE), F32), pltpu.VMEM((tc, bk, 2 * N_STATE), F32)],
        compiler_params=_params("arbitrary"),
        name="ssm",
    )(u3, bb, lb_re.reshape(1, N_STATE), lb_im.reshape(1, N_STATE), c_full, d.reshape(1, D_SSM), w_glu, h0_re, h0_im)
    return z, h_re, h_im


def _ffn_kernel(*refs, tm, decode):
    if decode:
        (x_ref, o_ref, z_ref, wo_ref, gn_ref, wua_ref, wug_ref, cw_ref, cb_ref, wd_ref, fn_ref, p2_ref, p1_ref,
         y_ref, a_out_ref, acc_scr, hn_scr) = refs
    else:
        (x_ref, o_ref, z_ref, wo_ref, gn_ref, wua_ref, wug_ref, cw_ref, cb_ref, wd_ref, fn_ref,
         y_ref, cs_ref, acc_scr, hn_scr, carry_scr) = refs

        @pl.when(pl.program_id(1) == 0)
        def _():
            carry_scr[...] = jnp.zeros(carry_scr.shape, F32)

    mix = (jnp.dot(o_ref[...].astype(BF16), wo_ref[:D_ATT, :], preferred_element_type=F32)
           + jnp.dot(z_ref[...].astype(BF16), wo_ref[D_ATT:, :], preferred_element_type=F32))
    x1 = x_ref[...] + mix
    r = lax.rsqrt(jnp.mean(x1 * x1, axis=-1, keepdims=True) + NORM_EPS)
    hn_scr[...] = (x1 * r * gn_ref[...]).astype(BF16)
    acc_scr[...] = x1

    def chunk(j, _):
        hn = hn_scr[...]
        a = jnp.dot(hn, wua_ref[j], preferred_element_type=F32)
        gate = jnp.dot(hn, wug_ref[j], preferred_element_type=F32)
        if decode:
            am2, am1 = p2_ref[j], p1_ref[j]
            a_out_ref[j] = a
        else:
            prev = carry_scr[j]
            p2 = jnp.broadcast_to(prev[CARRY_ROWS - 2:CARRY_ROWS - 1], a.shape)
            p1 = jnp.broadcast_to(prev[CARRY_ROWS - 1:], a.shape)
            row = lax.broadcasted_iota(jnp.int32, a.shape, 0)
            am1 = jnp.where(row < 1, p1, pltpu.roll(a, 1, 0))
            am2 = jnp.where(row < 1, p2, jnp.where(row < 2, p1, pltpu.roll(a, 2, 0)))
            carry_scr[j] = a[tm - CARRY_ROWS:]
        cw = cw_ref[j]
        c = cb_ref[j] + (cw[0:1] * am2 + cw[1:2] * am1 + cw[2:3] * a)
        hidden = (jax.nn.silu(c) * gate).astype(BF16)
        acc_scr[...] += jnp.dot(hidden, wd_ref[j], preferred_element_type=F32)
        return 0

    lax.fori_loop(0, N_FF_CHUNKS, chunk, 0)
    x2 = acc_scr[...]
    r2 = lax.rsqrt(jnp.mean(x2 * x2, axis=-1, keepdims=True) + NORM_EPS)
    y_ref[...] = x2 * r2 * fn_ref[...]
    if not decode:
        cs_ref[...] = carry_scr[...]


def _chunk_cols(a):
    return jnp.transpose(a.reshape(a.shape[0], N_FF_CHUNKS, FF_CHUNK), (1, 0, 2))


def _ffn(x, o, z, weights, tm, prev=None):
    B, S, D = x.shape
    ns = S // tm
    decode = prev is not None
    wo, gn, wua, wug, cw, cb, wd, fn = weights
    const = lambda shape: pl.BlockSpec(shape, lambda b, s: (0,) * len(shape), pipeline_mode=pl.Buffered(1))
    xs = pl.BlockSpec((None, tm, D), lambda b, s: (b, s, 0))
    in_specs = [xs, pl.BlockSpec((tm, D_ATT), lambda b, s: (b * ns + s, 0)),
                pl.BlockSpec((tm, D_SSM), lambda b, s: (s, b)),
                const((D, D)), const((1, D)), const((N_FF_CHUNKS, D, FF_CHUNK)), const((N_FF_CHUNKS, D, FF_CHUNK)),
                const((N_FF_CHUNKS, CONV_W, FF_CHUNK)), const((N_FF_CHUNKS, 1, FF_CHUNK)),
                const((N_FF_CHUNKS, FF_CHUNK, D)), const((1, D))]
    args = [x, o, z, wo, gn, wua, wug, cw, cb, wd, fn]
    scratch = [pltpu.VMEM((tm, D), F32), pltpu.VMEM((tm, D), BF16)]
    if decode:
        chunked = pl.BlockSpec((N_FF_CHUNKS, tm, FF_CHUNK), lambda b, s: (0, 0, 0))
        in_specs += [chunked, chunked]
        args += [_chunk_cols(prev[0]), _chunk_cols(prev[1])]
        out_specs = [xs, chunked]
        out_shape = [jax.ShapeDtypeStruct((B, S, D), F32), jax.ShapeDtypeStruct((N_FF_CHUNKS, tm, FF_CHUNK), F32)]
    else:
        out_specs = [xs, pl.BlockSpec((None, N_FF_CHUNKS, CARRY_ROWS, FF_CHUNK), lambda b, s: (b, 0, 0, 0))]
        out_shape = [jax.ShapeDtypeStruct((B, S, D), F32),
                     jax.ShapeDtypeStruct((B, N_FF_CHUNKS, CARRY_ROWS, FF_CHUNK), F32)]
        scratch.append(pltpu.VMEM((N_FF_CHUNKS, CARRY_ROWS, FF_CHUNK), F32))
    y, extra = pl.pallas_call(
        functools.partial(_ffn_kernel, tm=tm, decode=decode),
        grid=(B, ns),
        in_specs=in_specs, out_specs=out_specs, out_shape=out_shape, scratch_shapes=scratch,
        compiler_params=_params("parallel", "arbitrary"),
        name="ffn_decode" if decode else "ffn",
    )(*args)
    return y, extra


def kernel(x_prompt, x_sample, cache_k, cache_v, state_ssm_re, state_ssm_im, state_conv, page_table, norm_mix, w_in, lambda_q1, lambda_k1, lambda_q2, lambda_k2, subln_g, ssm_a_re, ssm_a_im, ssm_log_dt, ssm_b_re, ssm_b_im, ssm_c_re, ssm_c_im, ssm_d, w_glu, w_o, norm_ffn, w_up, conv_w, conv_b, w_down, final_norm):
    assert norm_mix.shape[0] == 1, "single layer"
    Bp, Sp, D = x_prompt.shape
    Bs = x_sample.shape[0]
    G, N, C = N_SSM_GROUPS, SSM_STATE, SSM_GROUP
    past_len = page_table.shape[1] * cache_k.shape[2]

    w_in_b = w_in[0].astype(BF16)
    wo_b = w_o[0].astype(BF16)
    wglu_b = w_glu[0].astype(BF16)
    wua = jnp.transpose(w_up[0][:, :D_FF].astype(BF16).reshape(D, N_FF_CHUNKS, FF_CHUNK), (1, 0, 2))
    wug = jnp.transpose(w_up[0][:, D_FF:].astype(BF16).reshape(D, N_FF_CHUNKS, FF_CHUNK), (1, 0, 2))
    wd = w_down[0].astype(BF16).reshape(N_FF_CHUNKS, FF_CHUNK, D)
    cw = _chunk_cols(conv_w[0])
    cb = _chunk_cols(conv_b[0].reshape(1, D_FF))
    ffn_w = (wo_b, norm_ffn[0].reshape(1, D), wua, wug, cw, cb, wd, final_norm.reshape(1, D))
    lam_vecs = [v.reshape(1, ATT_HEAD_DIM) for v in (lambda_q1[0], lambda_k1[0], lambda_q2[0], lambda_k2[0])]
    g_sub = subln_g[0].reshape(1, HEAD_W)

    lb_re, lb_im, bbt_re, bbt_im = _ssm_prep(ssm_a_re[0], ssm_a_im[0], ssm_log_dt[0], ssm_b_re[0], ssm_b_im[0])
    eye = jnp.eye(G, dtype=F32)
    place_b = lambda bbt: (bbt.reshape(G, C, 1, N) * eye[:, None, :, None]).reshape(G * C, G * N)
    bb_full = jnp.concatenate([place_b(bbt_re), place_b(bbt_im)], axis=1).astype(BF16)
    place_c = lambda c: (jnp.transpose(c, (0, 2, 1)).reshape(G, N, 1, C) * eye[:, None, :, None]).reshape(G * N, G * C)
    c_full = jnp.concatenate([place_c(ssm_c_re[0]), -place_c(ssm_c_im[0])], axis=0).astype(BF16)

    tabs_p = _rope_tables(jnp.arange(Sp, dtype=jnp.int32))
    q, k, v, u = _in_proj(x_prompt, norm_mix[0], w_in_b, tabs_p, tm=512)
    o = _attn(q, k, v, lam_vecs, g_sub, Bp, Sp, tq=256)
    zeros_h = jnp.zeros((Bp, N_STATE), F32)
    z, hre_p, him_p = _ssm(u.reshape(Sp, Bp, D_SSM), bb_full, lb_re, lb_im, c_full, ssm_d[0], wglu_b,
                           zeros_h, zeros_h, tc=64)
    y_prompt, cs_p = _ffn(x_prompt, o, z.reshape(Sp, Bp * D_SSM), ffn_w, tm=512)
    conv_prompt = jnp.transpose(cs_p[:, :, CARRY_ROWS - (CONV_W - 1):], (0, 2, 1, 3)).reshape(Bp, CONV_W - 1, D_FF)

    tabs_s = _rope_tables(jnp.full((Bs,), past_len, jnp.int32))
    qs, ks, vs, us = _in_proj(x_sample.reshape(1, Bs, D), norm_mix[0], w_in_b, tabs_s, tm=Bs)
    os_ = _decode_attn(qs, ks, vs, cache_k[0], cache_v[0], page_table, lam_vecs, g_sub)
    zs, hre_s, him_s = _ssm(us.reshape(1, Bs, D_SSM), bb_full, lb_re, lb_im, c_full, ssm_d[0], wglu_b,
                            state_ssm_re[0].reshape(Bs, N_STATE), state_ssm_im[0].reshape(Bs, N_STATE), tc=1)
    sc = state_conv[0]
    y_s, a_s = _ffn(x_sample.reshape(1, Bs, D), os_, zs.reshape(Bs, D_SSM), ffn_w, tm=Bs, prev=(sc[:, 0], sc[:, 1]))
    a_new = jnp.transpose(a_s, (1, 0, 2)).reshape(Bs, D_FF)
    conv_sample = jnp.stack([sc[:, 1], a_new], axis=1)

    st = lambda h, b: h.reshape(1, b, G, N)
    return (y_prompt, y_s.reshape(Bs, 1, D),
            k.reshape(1, Bp, Sp, N_ATT_HEADS, HEAD_W), v.reshape(1, Bp, Sp, N_ATT_HEADS, HEAD_W),
            st(hre_p, Bp), st(him_p, Bp), conv_prompt[None],
            ks.reshape(1, Bs, 1, N_ATT_HEADS, HEAD_W), vs.reshape(1, Bs, 1, N_ATT_HEADS, HEAD_W),
            st(hre_s, Bs), st(him_s, Bs), conv_sample[None])
```

```python
import functools
import math

import jax
import jax.numpy as jnp
from jax import lax
from jax.experimental import pallas as pl
from jax.experimental.pallas import tpu as pltpu

F32 = jnp.float32
BF16 = jnp.bfloat16

D_MODEL = 1024
D_ATT = 512
D_SSM = 512
ATT_HEAD_DIM = 64
N_ATT_HEADS = 4
HEAD_W = 2 * ATT_HEAD_DIM
ROT_DIM = 16
ROPE_THETA = 500000.0
SSM_GROUP = 16
N_SSM_GROUPS = 32
SSM_STATE = 64
N_STATE = N_SSM_GROUPS * SSM_STATE
D_FF = 2816
CONV_W = 3
NORM_EPS = 1e-6
SUBLN_EPS = 1e-5
NEG_INF = -1e30
D_PROJ = 3 * D_ATT + D_SSM
LAMBDA_INIT = 0.8 - 0.6 * math.exp(-0.3 * 0)

V7X_VMEM_LIMIT = 56 * 1024 * 1024
FF_CHUNK = 256
N_FF_CHUNKS = D_FF // FF_CHUNK
PAGES_PER_STEP = 8
CARRY_ROWS = 8


def _params(*sem):
    return pltpu.CompilerParams(dimension_semantics=sem, vmem_limit_bytes=V7X_VMEM_LIMIT)


def _lambda(lq1_ref, lk1_ref, lq2_ref, lk2_ref):
    s1 = jnp.sum(lq1_ref[...] * lk1_ref[...], axis=-1, keepdims=True)
    s2 = jnp.sum(lq2_ref[...] * lk2_ref[...], axis=-1, keepdims=True)
    return jnp.exp(s1) - jnp.exp(s2) + LAMBDA_INIT


def _sub_ln(o, g):
    r = lax.rsqrt(jnp.mean(o * o, axis=-1, keepdims=True) + SUBLN_EPS)
    return (o * r * g) * (1.0 - LAMBDA_INIT)


def _ssm_prep_kernel(ar_ref, ai_ref, ldt_ref, art_ref, ait_ref, brt_ref, bit_ref,
                     lbre_ref, lbim_ref, bbre_ref, bbim_ref):
    dt = jnp.exp(ldt_ref[...])

    def disc(ar, ai):
        mag = jnp.exp(ar * dt)
        return mag * jnp.cos(ai * dt), mag * jnp.sin(ai * dt)

    lb_re, lb_im = disc(ar_ref[...], ai_ref[...])
    lbre_ref[...] = lb_re
    lbim_ref[...] = lb_im
    ar, ai = art_ref[...], ait_ref[...]
    t_re, t_im = disc(ar, ai)
    den = ar * ar + ai * ai
    nr, ni = t_re - 1.0, t_im
    f_re = (nr * ar + ni * ai) / den
    f_im = (ni * ar - nr * ai) / den
    br, bi = brt_ref[...], bit_ref[...]
    bbre_ref[...] = f_re * br - f_im * bi
    bbim_ref[...] = f_re * bi + f_im * br


def _ssm_prep(a_re, a_im, log_dt, b_re, b_im):
    G, N, C = N_SSM_GROUPS, SSM_STATE, SSM_GROUP
    tile = lambda a: jnp.tile(a, (1, C))
    flat = lambda b: jnp.transpose(b, (0, 2, 1)).reshape(G, C * N)
    outs = pl.pallas_call(
        _ssm_prep_kernel,
        out_shape=[jax.ShapeDtypeStruct((G, N), F32)] * 2 + [jax.ShapeDtypeStruct((G, C * N), F32)] * 2,
        name="ssm_prep",
    )(a_re, a_im, log_dt.reshape(G, 1), tile(a_re), tile(a_im), flat(b_re), flat(b_im))
    return outs


def _in_proj_kernel(x_ref, g_ref, w_ref, cos_ref, sa_ref, sb_ref, q_ref, k_ref, v_ref, u_ref):
    x = x_ref[...]
    r = lax.rsqrt(jnp.mean(x * x, axis=-1, keepdims=True) + NORM_EPS)
    hn = (x * r * g_ref[...]).astype(BF16)
    proj = jnp.dot(hn, w_ref[...], preferred_element_type=F32)
    cos, sa, sb = cos_ref[...], sa_ref[...], sb_ref[...]
    half = ROT_DIM // 2

    def rope(xh):
        return xh * cos + pltpu.roll(xh, HEAD_W - half, 1) * sa + pltpu.roll(xh, half, 1) * sb

    for h in range(N_ATT_HEADS):
        lo, hi = h * HEAD_W, (h + 1) * HEAD_W
        q_ref[:, lo:hi] = rope(proj[:, lo:hi]) * (ATT_HEAD_DIM ** -0.5)
        k_ref[:, lo:hi] = rope(proj[:, D_ATT + lo:D_ATT + hi])
    v_ref[...] = proj[:, 2 * D_ATT:3 * D_ATT]
    u_ref[...] = proj[:, 3 * D_ATT:]


def _rope_tables(pos):
    half = ROT_DIM // 2
    inv = ROPE_THETA ** (-jnp.arange(0, ROT_DIM, 2, dtype=F32) / ROT_DIM)
    ang = pos.astype(F32)[:, None] * inv[None, :]
    cos, sin = jnp.cos(ang), jnp.sin(ang)
    S = pos.shape[0]
    pad = jnp.zeros((S, ATT_HEAD_DIM - ROT_DIM), F32)
    zh = jnp.zeros((S, half), F32)
    comp = lambda a, b, c: jnp.concatenate([a, b, c], axis=1)
    two = lambda t: jnp.concatenate([t, t], axis=1)
    return (two(comp(cos, cos, pad + 1.0)), two(comp(-sin, zh, pad)), two(comp(zh, sin, pad)))


def _in_proj(x, g, w_bf16, tables, tm):
    B, S, D = x.shape
    ns = S // tm
    row = lambda b, s: (b * ns + s, 0)
    tab = pl.BlockSpec((tm, HEAD_W), lambda b, s: (s, 0))
    qkv_spec = pl.BlockSpec((tm, D_ATT), row)
    q, k, v, u = pl.pallas_call(
        _in_proj_kernel,
        grid=(B, ns),
        in_specs=[pl.BlockSpec((None, tm, D), lambda b, s: (b, s, 0)),
                  pl.BlockSpec((1, D), lambda b, s: (0, 0)),
                  pl.BlockSpec((D, D_PROJ), lambda b, s: (0, 0)),
                  tab, tab, tab],
        out_specs=[qkv_spec, qkv_spec, qkv_spec, pl.BlockSpec((tm, D_SSM), lambda b, s: (s, b))],
        out_shape=[jax.ShapeDtypeStruct((B * S, D_ATT), F32)] * 3 + [jax.ShapeDtypeStruct((S, B * D_SSM), F32)],
        compiler_params=_params("parallel", "parallel"),
        name="in_proj",
    )(x, g.reshape(1, D), w_bf16, *tables)
    return q, k, v, u


def _attn_kernel(lq1_ref, lk1_ref, lq2_ref, lk2_ref, g_ref, q_ref, k_ref, v_ref, o_ref, *, tq):
    i = pl.program_id(2)
    lam = _lambda(lq1_ref, lk1_ref, lq2_ref, lk2_ref)
    q = q_ref[...]
    lane = lax.broadcasted_iota(jnp.int32, q.shape, 1)
    qs = jnp.concatenate([jnp.where(lane < ATT_HEAD_DIM, q, 0.0), jnp.where(lane >= ATT_HEAD_DIM, q, 0.0)], axis=0)

    def block(j, carry, masked):
        m, l, acc = carry
        start = pl.multiple_of(j * tq, tq)
        kj = k_ref[pl.ds(start, tq), :]
        vj = v_ref[pl.ds(start, tq), :]
        s = lax.dot_general(qs, kj, (((1,), (1,)), ((), ())), preferred_element_type=F32)
        if masked:
            r = lax.broadcasted_iota(jnp.int32, (tq, tq), 0)
            c = lax.broadcasted_iota(jnp.int32, (tq, tq), 1)
            keep = jnp.concatenate([c <= r, c <= r], axis=0)
            s = jnp.where(keep, s, NEG_INF)
        m_new = jnp.maximum(m, jnp.max(s, axis=-1, keepdims=True))
        alpha = jnp.exp(m - m_new)
        p = jnp.exp(s - m_new)
        l = alpha * l + jnp.sum(p, axis=-1, keepdims=True)
        acc = alpha * acc + jnp.dot(p, vj, preferred_element_type=F32)
        return m_new, l, acc

    init = (jnp.full((2 * tq, 1), NEG_INF, F32), jnp.zeros((2 * tq, 1), F32), jnp.zeros((2 * tq, HEAD_W), F32))
    carry = lax.fori_loop(0, i, lambda j, c: block(j, c, False), init)
    m, l, acc = block(i, carry, True)
    o = acc[:tq] / l[:tq] - lam * (acc[tq:] / l[tq:])
    o_ref[...] = _sub_ln(o, g_ref[...])


def _attn(q, k, v, lam_vecs, g, B, S, tq):
    nq = S // tq
    vec = pl.BlockSpec((1, ATT_HEAD_DIM), lambda b, h, i: (0, 0))
    kv = pl.BlockSpec((S, HEAD_W), lambda b, h, i: (b, h))
    qo = pl.BlockSpec((tq, HEAD_W), lambda b, h, i: (b * nq + i, h))
    return pl.pallas_call(
        functools.partial(_attn_kernel, tq=tq),
        grid=(B, N_ATT_HEADS, nq),
        in_specs=[vec, vec, vec, vec, pl.BlockSpec((1, HEAD_W), lambda b, h, i: (0, 0)), qo, kv, kv],
        out_specs=qo,
        out_shape=jax.ShapeDtypeStruct((B * S, D_ATT), F32),
        compiler_params=_params("parallel", "parallel", "parallel"),
        name="attn",
    )(*lam_vecs, g, q, k, v)


def _decode_kernel(pt_ref, lq1_ref, lk1_ref, lq2_ref, lk2_ref, g_ref, q_ref, kn_ref, vn_ref, *rest):
    del pt_ref
    P = PAGES_PER_STEP
    k_refs, v_refs = rest[:P], rest[P:2 * P]
    o_ref, m_scr, l_scr, acc_scr = rest[2 * P:]
    j = pl.program_id(1)
    H = N_ATT_HEADS

    @pl.when(j == 0)
    def _():
        m_scr[...] = jnp.full(m_scr.shape, NEG_INF, F32)
        l_scr[...] = jnp.zeros(l_scr.shape, F32)
        acc_scr[...] = jnp.zeros(acc_scr.shape, F32)

    q = q_ref[...]
    lane = lax.broadcasted_iota(jnp.int32, q.shape, 1)
    q8 = jnp.concatenate([jnp.where(lane < ATT_HEAD_DIM, q, 0.0), jnp.where(lane >= ATT_HEAD_DIM, q, 0.0)], axis=0)
    rows = k_refs[0].shape[0]
    r = lax.broadcasted_iota(jnp.int32, (2 * H, rows), 0)
    col = lax.broadcasted_iota(jnp.int32, (2 * H, rows), 1)
    own = (col % H) == (r % H)
    s = jnp.concatenate(
        [jnp.where(own, lax.dot_general(q8, kr[...], (((1,), (1,)), ((), ())), preferred_element_type=F32), NEG_INF)
         for kr in k_refs], axis=1)
    m = m_scr[...]
    m_new = jnp.maximum(m, jnp.max(s, axis=-1, keepdims=True))
    alpha = jnp.exp(m - m_new)
    p = jnp.exp(s - m_new)
    pv = jnp.dot(p[:, :rows], v_refs[0][...], preferred_element_type=F32)
    for i in range(1, P):
        pv = pv + jnp.dot(p[:, i * rows:(i + 1) * rows], v_refs[i][...], preferred_element_type=F32)
    m_scr[...] = m_new
    l_scr[...] = alpha * l_scr[...] + jnp.sum(p, axis=-1, keepdims=True)
    acc_scr[...] = alpha * acc_scr[...] + pv

    @pl.when(j == pl.num_programs(1) - 1)
    def _():
        lam = _lambda(lq1_ref, lk1_ref, lq2_ref, lk2_ref)
        kn = jnp.concatenate([kn_ref[...], kn_ref[...]], axis=0)
        vn = jnp.concatenate([vn_ref[...], vn_ref[...]], axis=0)
        s_new = jnp.sum(q8 * kn, axis=-1, keepdims=True)
        m_old = m_scr[...]
        m_fin = jnp.maximum(m_old, s_new)
        a = jnp.exp(m_old - m_fin)
        p_new = jnp.exp(s_new - m_fin)
        l = a * l_scr[...] + p_new
        acc = (a * acc_scr[...] + p_new * vn) / l
        o_ref[...] = _sub_ln(acc[:H] - lam * acc[H:], g_ref[...])


def _decode_attn(q, k_new, v_new, cache_k, cache_v, page_table, lam_vecs, g):
    Bs, n_pages = page_table.shape
    n_pool, page = cache_k.shape[0], cache_k.shape[1]
    P, H = PAGES_PER_STEP, N_ATT_HEADS
    ck = cache_k.reshape(n_pool, page * H, HEAD_W)
    cv = cache_v.reshape(n_pool, page * H, HEAD_W)
    vec = pl.BlockSpec((1, ATT_HEAD_DIM), lambda b, j, pt: (0, 0))
    tok = pl.BlockSpec((None, H, HEAD_W), lambda b, j, pt: (b, 0, 0))
    page_spec = lambda i: pl.BlockSpec((None, page * H, HEAD_W), lambda b, j, pt: (pt[b, j * P + i], 0, 0))
    grid_spec = pltpu.PrefetchScalarGridSpec(
        num_scalar_prefetch=1,
        grid=(Bs, n_pages // P),
        in_specs=[vec, vec, vec, vec, pl.BlockSpec((1, HEAD_W), lambda b, j, pt: (0, 0)), tok, tok, tok]
                 + [page_spec(i) for i in range(P)] * 2,
        out_specs=tok,
        scratch_shapes=[pltpu.VMEM((2 * H, 1), F32), pltpu.VMEM((2 * H, 1), F32), pltpu.VMEM((2 * H, HEAD_W), F32)],
    )
    tok3 = lambda a: a.reshape(Bs, H, HEAD_W)
    o = pl.pallas_call(
        _decode_kernel,
        grid_spec=grid_spec,
        out_shape=jax.ShapeDtypeStruct((Bs, H, HEAD_W), F32),
        compiler_params=_params("parallel", "arbitrary"),
        name="decode_attn",
    )(page_table, *lam_vecs, g, tok3(q), tok3(k_new), tok3(v_new), *([ck] * P), *([cv] * P))
    return o.reshape(Bs, D_ATT)


SCAN_COLS = 512


def _ssm_kernel(u_ref, bb_ref, lre_ref, lim_ref, c_ref, d_ref, wglu_ref, h0re_ref, h0im_ref,
                z_ref, hre_ref, him_ref, bu_scr, hs_scr, *, tc, bk):
    t0 = pl.program_id(0)

    @pl.when(t0 == 0)
    def _():
        hre_ref[...] = h0re_ref[...]
        him_ref[...] = h0im_ref[...]

    u = u_ref[...].reshape(tc * bk, D_SSM)
    bu = jnp.dot(u.astype(BF16), bb_ref[...], preferred_element_type=F32)
    bu_scr[...] = bu.reshape(tc, bk, 2 * N_STATE)

    for cg in range(N_STATE // SCAN_COLS):
        re = slice(cg * SCAN_COLS, (cg + 1) * SCAN_COLS)
        im = slice(N_STATE + cg * SCAN_COLS, N_STATE + (cg + 1) * SCAN_COLS)
        lr = jnp.broadcast_to(lre_ref[:, re], (bk, SCAN_COLS))
        li = jnp.broadcast_to(lim_ref[:, re], (bk, SCAN_COLS))

        def step(t, carry):
            hr, hi = carry
            nhr = lr * hr - li * hi + bu_scr[t, :, re]
            nhi = lr * hi + li * hr + bu_scr[t, :, im]
            hs_scr[t, :, re] = nhr
            hs_scr[t, :, im] = nhi
            return nhr, nhi

        hr, hi = lax.fori_loop(0, tc, step, (hre_ref[:, re], him_ref[:, re]))
        hre_ref[:, re] = hr
        him_ref[:, re] = hi

    hs = hs_scr[...].reshape(tc * bk, 2 * N_STATE)
    y = jnp.dot(hs.astype(BF16), c_ref[...], preferred_element_type=F32) + d_ref[...] * u
    y = jax.nn.gelu(y)
    gate = jnp.dot(y.astype(BF16), wglu_ref[...], preferred_element_type=F32)
    z_ref[...] = (y * jax.nn.sigmoid(gate)).reshape(tc, bk, D_SSM)


def _ssm(u3, bb, lb_re, lb_im, c_full, d, w_glu, h0_re, h0_im, tc):
    S, bk, _ = u3.shape
    const = lambda shape: pl.BlockSpec(shape, lambda t: (0,) * len(shape))
    seq = pl.BlockSpec((tc, bk, D_SSM), lambda t: (t, 0, 0))
    z, h_re, h_im = pl.pallas_call(
        functools.partial(_ssm_kernel, tc=tc, bk=bk),
        grid=(S // tc,),
        in_specs=[seq, const((D_SSM, 2 * N_STATE)), const((1, N_STATE)), const((1, N_STATE)),
                  const((2 * N_STATE, D_SSM)), const((1, D_SSM)), const((D_SSM, D_SSM)),
                  const((bk, N_STATE)), const((bk, N_STATE))],
        out_specs=[seq, const((bk, N_STATE)), const((bk, N_STATE))],
        out_shape=[jax.ShapeDtypeStruct((S, bk, D_SSM), F32), jax.ShapeDtypeStruct((bk, N_STATE), F32),
                   jax.ShapeDtypeStruct((bk, N_STATE), F32)],
        scratch_shapes=[pltpu.VMEM((tc, bk, 2 * N_STATE), F32), pltpu.VMEM((tc, bk, 2 * N_STATE), F32)],
        compiler_params=_params("arbitrary"),
        name="ssm",
    )(u3, bb, lb_re.reshape(1, N_STATE), lb_im.reshape(1, N_STATE), c_full, d.reshape(1, D_SSM), w_glu, h0_re, h0_im)
    return z, h_re, h_im


def _ffn_kernel(*refs, tm, decode):
    if decode:
        (x_ref, o_ref, z_ref, wo_ref, gn_ref, wua_ref, wug_ref, cw_ref, cb_ref, wd_ref, fn_ref, p2_ref, p1_ref,
         y_ref, a_out_ref, acc_scr, hn_scr) = refs
    else:
        (x_ref, o_ref, z_ref, wo_ref, gn_ref, wua_ref, wug_ref, cw_ref, cb_ref, wd_ref, fn_ref,
         y_ref, cs_ref, acc_scr, hn_scr, carry_scr) = refs

        @pl.when(pl.program_id(1) == 0)
        def _():
            carry_scr[...] = jnp.zeros(carry_scr.shape, F32)

    mix = (jnp.dot(o_ref[...].astype(BF16), wo_ref[:D_ATT, :], preferred_element_type=F32)
           + jnp.dot(z_ref[...].astype(BF16), wo_ref[D_ATT:, :], preferred_element_type=F32))
    x1 = x_ref[...] + mix
    r = lax.rsqrt(jnp.mean(x1 * x1, axis=-1, keepdims=True) + NORM_EPS)
    hn_scr[...] = (x1 * r * gn_ref[...]).astype(BF16)
    acc_scr[...] = x1

    def chunk(j, _):
        hn = hn_scr[...]
        a = jnp.dot(hn, wua_ref[j], preferred_element_type=F32)
        gate = jnp.dot(hn, wug_ref[j], preferred_element_type=F32)
        if decode:
            am2, am1 = p2_ref[j], p1_ref[j]
            a_out_ref[j] = a
        else:
            prev = carry_scr[j]
            p2 = jnp.broadcast_to(prev[CARRY_ROWS - 2:CARRY_ROWS - 1], a.shape)
            p1 = jnp.broadcast_to(prev[CARRY_ROWS - 1:], a.shape)
            row = lax.broadcasted_iota(jnp.int32, a.shape, 0)
            am1 = jnp.where(row < 1, p1, pltpu.roll(a, 1, 0))
            am2 = jnp.where(row < 1, p2, jnp.where(row < 2, p1, pltpu.roll(a, 2, 0)))
            carry_scr[j] = a[tm - CARRY_ROWS:]
        cw = cw_ref[j]
        c = cb_ref[j] + (cw[0:1] * am2 + cw[1:2] * am1 + cw[2:3] * a)
        hidden = (jax.nn.silu(c) * gate).astype(BF16)
        acc_scr[...] += jnp.dot(hidden, wd_ref[j], preferred_element_type=F32)
        return 0

    lax.fori_loop(0, N_FF_CHUNKS, chunk, 0)
    x2 = acc_scr[...]
    r2 = lax.rsqrt(jnp.mean(x2 * x2, axis=-1, keepdims=True) + NORM_EPS)
    y_ref[...] = x2 * r2 * fn_ref[...]
    if not decode:
        cs_ref[...] = carry_scr[...]


def _chunk_cols(a):
    return jnp.transpose(a.reshape(a.shape[0], N_FF_CHUNKS, FF_CHUNK), (1, 0, 2))


def _ffn(x, o, z, weights, tm, prev=None):
    B, S, D = x.shape
    ns = S // tm
    decode = prev is not None
    wo, gn, wua, wug, cw, cb, wd, fn = weights
    const = lambda shape: pl.BlockSpec(shape, lambda b, s: (0,) * len(shape), pipeline_mode=pl.Buffered(1))
    xs = pl.BlockSpec((None, tm, D), lambda b, s: (b, s, 0))
    in_specs = [xs, pl.BlockSpec((tm, D_ATT), lambda b, s: (b * ns + s, 0)),
                pl.BlockSpec((tm, D_SSM), lambda b, s: (s, b)),
                const((D, D)), const((1, D)), const((N_FF_CHUNKS, D, FF_CHUNK)), const((N_FF_CHUNKS, D, FF_CHUNK)),
                const((N_FF_CHUNKS, CONV_W, FF_CHUNK)), const((N_FF_CHUNKS, 1, FF_CHUNK)),
                const((N_FF_CHUNKS, FF_CHUNK, D)), const((1, D))]
    args = [x, o, z, wo, gn, wua, wug, cw, cb, wd, fn]
    scratch = [pltpu.VMEM((tm, D), F32), pltpu.VMEM((tm, D), BF16)]
    if decode:
        chunked = pl.BlockSpec((N_FF_CHUNKS, tm, FF_CHUNK), lambda b, s: (0, 0, 0))
        in_specs += [chunked, chunked]
        args += [_chunk_cols(prev[0]), _chunk_cols(prev[1])]
        out_specs = [xs, chunked]
        out_shape = [jax.ShapeDtypeStruct((B, S, D), F32), jax.ShapeDtypeStruct((N_FF_CHUNKS, tm, FF_CHUNK), F32)]
    else:
        out_specs = [xs, pl.BlockSpec((None, N_FF_CHUNKS, CARRY_ROWS, FF_CHUNK), lambda b, s: (b, 0, 0, 0))]
        out_shape = [jax.ShapeDtypeStruct((B, S, D), F32),
                     jax.ShapeDtypeStruct((B, N_FF_CHUNKS, CARRY_ROWS, FF_CHUNK), F32)]
        scratch.append(pltpu.VMEM((N_FF_CHUNKS, CARRY_ROWS, FF_CHUNK), F32))
    y, extra = pl.pallas_call(
        functools.partial(_ffn_kernel, tm=tm, decode=decode),
        grid=(B, ns),
        in_specs=in_specs, out_specs=out_specs, out_shape=out_shape, scratch_shapes=scratch,
        compiler_params=_params("parallel", "arbitrary"),
        name="ffn_decode" if decode else "ffn",
    )(*args)
    return y, extra


def kernel(x_prompt, x_sample, cache_k, cache_v, state_ssm_re, state_ssm_im, state_conv, page_table, norm_mix, w_in, lambda_q1, lambda_k1, lambda_q2, lambda_k2, subln_g, ssm_a_re, ssm_a_im, ssm_log_dt, ssm_b_re, ssm_b_im, ssm_c_re, ssm_c_im, ssm_d, w_glu, w_o, norm_ffn, w_up, conv_w, conv_b, w_down, final_norm):
    assert norm_mix.shape[0] == 1, "single layer"
    Bp, Sp, D = x_prompt.shape
    Bs = x_sample.shape[0]
    G, N, C = N_SSM_GROUPS, SSM_STATE, SSM_GROUP
    past_len = page_table.shape[1] * cache_k.shape[2]

    w_in_b = w_in[0].astype(BF16)
    wo_b = w_o[0].astype(BF16)
    wglu_b = w_glu[0].astype(BF16)
    wua = jnp.transpose(w_up[0][:, :D_FF].astype(BF16).reshape(D, N_FF_CHUNKS, FF_CHUNK), (1, 0, 2))
    wug = jnp.transpose(w_up[0][:, D_FF:].astype(BF16).reshape(D, N_FF_CHUNKS, FF_CHUNK), (1, 0, 2))
    wd = w_down[0].astype(BF16).reshape(N_FF_CHUNKS, FF_CHUNK, D)
    cw = _chunk_cols(conv_w[0])
    cb = _chunk_cols(conv_b[0].reshape(1, D_FF))
    ffn_w = (wo_b, norm_ffn[0].reshape(1, D), wua, wug, cw, cb, wd, final_norm.reshape(1, D))
    lam_vecs = [v.reshape(1, ATT_HEAD_DIM) for v in (lambda_q1[0], lambda_k1[0], lambda_q2[0], lambda_k2[0])]
    g_sub = subln_g[0].reshape(1, HEAD_W)

    lb_re, lb_im, bbt_re, bbt_im = _ssm_prep(ssm_a_re[0], ssm_a_im[0], ssm_log_dt[0], ssm_b_re[0], ssm_b_im[0])
    eye = jnp.eye(G, dtype=F32)
    place_b = lambda bbt: (bbt.reshape(G, C, 1, N) * eye[:, None, :, None]).reshape(G * C, G * N)
    bb_full = jnp.concatenate([place_b(bbt_re), place_b(bbt_im)], axis=1).astype(BF16)
    place_c = lambda c: (jnp.transpose(c, (0, 2, 1)).reshape(G, N, 1, C) * eye[:, None, :, None]).reshape(G * N, G * C)
    c_full = jnp.concatenate([place_c(ssm_c_re[0]), -place_c(ssm_c_im[0])], axis=0).astype(BF16)

    tabs_p = _rope_tables(jnp.arange(Sp, dtype=jnp.int32))
    q, k, v, u = _in_proj(x_prompt, norm_mix[0], w_in_b, tabs_p, tm=512)
    o = _attn(q, k, v, lam_vecs, g_sub, Bp, Sp, tq=256)
    zeros_h = jnp.zeros((Bp, N_STATE), F32)
    z, hre_p, him_p = _ssm(u.reshape(Sp, Bp, D_SSM), bb_full, lb_re, lb_im, c_full, ssm_d[0], wglu_b,
                           zeros_h, zeros_h, tc=64)
    y_prompt, cs_p = _ffn(x_prompt, o, z.reshape(Sp, Bp * D_SSM), ffn_w, tm=512)
    conv_prompt = jnp.transpose(cs_p[:, :, CARRY_ROWS - (CONV_W - 1):], (0, 2, 1, 3)).reshape(Bp, CONV_W - 1, D_FF)

    tabs_s = _rope_tables(jnp.full((Bs,), past_len, jnp.int32))
    qs, ks, vs, us = _in_proj(x_sample.reshape(1, Bs, D), norm_mix[0], w_in_b, tabs_s, tm=Bs)
    os_ = _decode_attn(qs, ks, vs, cache_k[0], cache_v[0], page_table, lam_vecs, g_sub)
    zs, hre_s, him_s = _ssm(us.reshape(1, Bs, D_SSM), bb_full, lb_re, lb_im, c_full, ssm_d[0], wglu_b,
                            state_ssm_re[0].reshape(Bs, N_STATE), state_ssm_im[0].reshape(Bs, N_STATE), tc=1)
    sc = state_conv[0]
    y_s, a_s = _ffn(x_sample.reshape(1, Bs, D), os_, zs.reshape(Bs, D_SSM), ffn_w, tm=Bs, prev=(sc[:, 0], sc[:, 1]))
    a_new = jnp.transpose(a_s, (1, 0, 2)).reshape(Bs, D_FF)
    conv_sample = jnp.stack([sc[:, 1], a_new], axis=1)

    st = lambda h, b: h.reshape(1, b, G, N)
    return (y_prompt, y_s.reshape(Bs, 1, D),
            k.reshape(1, Bp, Sp, N_ATT_HEADS, HEAD_W), v.reshape(1, Bp, Sp, N_ATT_HEADS, HEAD_W),
            st(hre_p, Bp), st(him_p, Bp), conv_prompt[None],
            ks.reshape(1, Bs, 1, N_ATT_HEADS, HEAD_W), vs.reshape(1, Bs, 1, N_ATT_HEADS, HEAD_W),
            st(hre_s, Bs), st(him_s, Bs), conv_sample[None])
```

```python
import functools
import math

import jax
import jax.numpy as jnp
from jax import lax
from jax.experimental import pallas as pl
from jax.experimental.pallas import tpu as pltpu

F32 = jnp.float32
BF16 = jnp.bfloat16

D_MODEL = 1024
D_ATT = 512
D_SSM = 512
ATT_HEAD_DIM = 64
N_ATT_HEADS = 4
HEAD_W = 2 * ATT_HEAD_DIM
ROT_DIM = 16
ROPE_THETA = 500000.0
SSM_GROUP = 16
N_SSM_GROUPS = 32
SSM_STATE = 64
N_STATE = N_SSM_GROUPS * SSM_STATE
D_FF = 2816
CONV_W = 3
NORM_EPS = 1e-6
SUBLN_EPS = 1e-5
NEG_INF = -1e30
D_PROJ = 3 * D_ATT + D_SSM
LAMBDA_INIT = 0.8 - 0.6 * math.exp(-0.3 * 0)
LOG2_E = math.log2(math.e)

V7X_VMEM_LIMIT = 56 * 1024 * 1024
FF_CHUNK = 256
N_FF_CHUNKS = D_FF // FF_CHUNK
PAGES_PER_STEP = 16
CARRY_ROWS = 8


def _params(*sem):
    return pltpu.CompilerParams(dimension_semantics=sem, vmem_limit_bytes=V7X_VMEM_LIMIT)


def _lambda(lq1_ref, lk1_ref, lq2_ref, lk2_ref):
    s1 = jnp.sum(lq1_ref[...] * lk1_ref[...], axis=-1, keepdims=True)
    s2 = jnp.sum(lq2_ref[...] * lk2_ref[...], axis=-1, keepdims=True)
    return jnp.exp(s1) - jnp.exp(s2) + LAMBDA_INIT


def _sub_ln(o, g):
    r = lax.rsqrt(jnp.mean(o * o, axis=-1, keepdims=True) + SUBLN_EPS)
    return (o * r * g) * (1.0 - LAMBDA_INIT)


def _ssm_prep_kernel(ar_ref, ai_ref, ldt_ref, art_ref, ait_ref, brt_ref, bit_ref,
                     lbre_ref, lbim_ref, bbre_ref, bbim_ref):
    dt = jnp.exp(ldt_ref[...])

    def disc(ar, ai):
        mag = jnp.exp(ar * dt)
        return mag * jnp.cos(ai * dt), mag * jnp.sin(ai * dt)

    lb_re, lb_im = disc(ar_ref[...], ai_ref[...])
    lbre_ref[...] = lb_re
    lbim_ref[...] = lb_im
    ar, ai = art_ref[...], ait_ref[...]
    t_re, t_im = disc(ar, ai)
    den = ar * ar + ai * ai
    nr, ni = t_re - 1.0, t_im
    f_re = (nr * ar + ni * ai) / den
    f_im = (ni * ar - nr * ai) / den
    br, bi = brt_ref[...], bit_ref[...]
    bbre_ref[...] = f_re * br - f_im * bi
    bbim_ref[...] = f_re * bi + f_im * br


def _ssm_prep(a_re, a_im, log_dt, b_re, b_im):
    G, N, C = N_SSM_GROUPS, SSM_STATE, SSM_GROUP
    tile = lambda a: jnp.tile(a, (1, C))
    flat = lambda b: jnp.transpose(b, (0, 2, 1)).reshape(G, C * N)
    outs = pl.pallas_call(
        _ssm_prep_kernel,
        out_shape=[jax.ShapeDtypeStruct((G, N), F32)] * 2 + [jax.ShapeDtypeStruct((G, C * N), F32)] * 2,
        name="ssm_prep",
    )(a_re, a_im, log_dt.reshape(G, 1), tile(a_re), tile(a_im), flat(b_re), flat(b_im))
    return outs


def _in_proj_kernel(x_ref, g_ref, w_ref, cos_ref, sa_ref, sb_ref, q_ref, k_ref, v_ref, u_ref):
    x = x_ref[...]
    r = lax.rsqrt(jnp.mean(x * x, axis=-1, keepdims=True) + NORM_EPS)
    hn = (x * r * g_ref[...]).astype(BF16)
    proj = jnp.dot(hn, w_ref[...], preferred_element_type=F32)
    cos, sa, sb = cos_ref[...], sa_ref[...], sb_ref[...]
    half = ROT_DIM // 2

    def rope(xh):
        return xh * cos + pltpu.roll(xh, HEAD_W - half, 1) * sa + pltpu.roll(xh, half, 1) * sb

    for h in range(N_ATT_HEADS):
        lo, hi = h * HEAD_W, (h + 1) * HEAD_W
        q_ref[:, lo:hi] = rope(proj[:, lo:hi]) * (ATT_HEAD_DIM ** -0.5)
        k_ref[:, lo:hi] = rope(proj[:, D_ATT + lo:D_ATT + hi])
    v_ref[...] = proj[:, 2 * D_ATT:3 * D_ATT]
    u_ref[...] = proj[:, 3 * D_ATT:]


def _rope_tables(pos):
    half = ROT_DIM // 2
    inv = ROPE_THETA ** (-jnp.arange(0, ROT_DIM, 2, dtype=F32) / ROT_DIM)
    ang = pos.astype(F32)[:, None] * inv[None, :]
    cos, sin = jnp.cos(ang), jnp.sin(ang)
    S = pos.shape[0]
    pad = jnp.zeros((S, ATT_HEAD_DIM - ROT_DIM), F32)
    zh = jnp.zeros((S, half), F32)
    comp = lambda a, b, c: jnp.concatenate([a, b, c], axis=1)
    two = lambda t: jnp.concatenate([t, t], axis=1)
    return (two(comp(cos, cos, pad + 1.0)), two(comp(-sin, zh, pad)), two(comp(zh, sin, pad)))


def _in_proj(x, g, w_bf16, tables, tm):
    B, S, D = x.shape
    ns = S // tm
    row = lambda b, s: (b * ns + s, 0)
    tab = pl.BlockSpec((tm, HEAD_W), lambda b, s: (s, 0))
    qkv_spec = pl.BlockSpec((tm, D_ATT), row)
    q, k, v, u = pl.pallas_call(
        _in_proj_kernel,
        grid=(B, ns),
        in_specs=[pl.BlockSpec((None, tm, D), lambda b, s: (b, s, 0)),
                  pl.BlockSpec((1, D), lambda b, s: (0, 0)),
                  pl.BlockSpec((D, D_PROJ), lambda b, s: (0, 0)),
                  tab, tab, tab],
        out_specs=[qkv_spec, qkv_spec, qkv_spec, pl.BlockSpec((tm, D_SSM), lambda b, s: (s, b))],
        out_shape=[jax.ShapeDtypeStruct((B * S, D_ATT), F32)] * 3 + [jax.ShapeDtypeStruct((S, B * D_SSM), F32)],
        compiler_params=_params("parallel", "parallel"),
        name="in_proj",
    )(x, g.reshape(1, D), w_bf16, *tables)
    return q, k, v, u


def _attn_kernel(lq1_ref, lk1_ref, lq2_ref, lk2_ref, g_ref, q_ref, k_ref, v_ref, o_ref, *, tq):
    i = pl.program_id(2)
    lam = _lambda(lq1_ref, lk1_ref, lq2_ref, lk2_ref)
    nh = q_ref.shape[1] // HEAD_W
    lane = lax.broadcasted_iota(jnp.int32, (tq, HEAD_W), 1)
    qs = []
    for h in range(nh):
        q = q_ref[:, h * HEAD_W:(h + 1) * HEAD_W] * LOG2_E
        qs.append(jnp.concatenate([jnp.where(lane < ATT_HEAD_DIM, q, 0.0), jnp.where(lane >= ATT_HEAD_DIM, q, 0.0)],
                                  axis=0))

    def block(j, carry, masked):
        start = pl.multiple_of(j * tq, tq)
        out = []
        for h in range(nh):
            m, l, acc = carry[h]
            kj = k_ref[pl.ds(start, tq), h * HEAD_W:(h + 1) * HEAD_W]
            vj = v_ref[pl.ds(start, tq), h * HEAD_W:(h + 1) * HEAD_W]
            s = lax.dot_general(qs[h], kj, (((1,), (1,)), ((), ())), preferred_element_type=F32)
            if masked:
                r = lax.broadcasted_iota(jnp.int32, (tq, tq), 0)
                c = lax.broadcasted_iota(jnp.int32, (tq, tq), 1)
                keep = jnp.concatenate([c <= r, c <= r], axis=0)
                s = jnp.where(keep, s, NEG_INF)
            m_new = jnp.maximum(m, jnp.max(s, axis=-1, keepdims=True))
            alpha = jnp.exp2(m - m_new)
            p = jnp.exp2(s - m_new)
            l = alpha * l + jnp.sum(p, axis=-1, keepdims=True)
            acc = alpha * acc + jnp.dot(p, vj, preferred_element_type=F32)
            out.append((m_new, l, acc))
        return tuple(out)

    init = tuple((jnp.full((2 * tq, 1), NEG_INF, F32), jnp.zeros((2 * tq, 1), F32),
                  jnp.zeros((2 * tq, HEAD_W), F32)) for _ in range(nh))
    carry = lax.fori_loop(0, i, lambda j, c: block(j, c, False), init)
    carry = block(i, carry, True)
    for h in range(nh):
        m, l, acc = carry[h]
        o = acc[:tq] / l[:tq] - lam * (acc[tq:] / l[tq:])
        o_ref[:, h * HEAD_W:(h + 1) * HEAD_W] = _sub_ln(o, g_ref[...])


def _attn(q, k, v, lam_vecs, g, B, S, tq, nh):
    nq = S // tq
    vec = pl.BlockSpec((1, ATT_HEAD_DIM), lambda b, h, i: (0, 0))
    kv = pl.BlockSpec((S, nh * HEAD_W), lambda b, h, i: (b, h))
    qo = pl.BlockSpec((tq, nh * HEAD_W), lambda b, h, i: (b * nq + i, h))
    return pl.pallas_call(
        functools.partial(_attn_kernel, tq=tq),
        grid=(B, N_ATT_HEADS // nh, nq),
        in_specs=[vec, vec, vec, vec, pl.BlockSpec((1, HEAD_W), lambda b, h, i: (0, 0)), qo, kv, kv],
        out_specs=qo,
        out_shape=jax.ShapeDtypeStruct((B * S, D_ATT), F32),
        compiler_params=_params("parallel", "parallel", "parallel"),
        name="attn",
    )(*lam_vecs, g, q, k, v)


def _decode_kernel(pt_ref, lq1_ref, lk1_ref, lq2_ref, lk2_ref, g_ref, q_ref, kn_ref, vn_ref, *rest):
    del pt_ref
    P = PAGES_PER_STEP
    k_refs, v_refs = rest[:P], rest[P:2 * P]
    o_ref, m_scr, l_scr, acc_scr = rest[2 * P:]
    j = pl.program_id(1)
    H = N_ATT_HEADS

    @pl.when(j == 0)
    def _():
        m_scr[...] = jnp.full(m_scr.shape, NEG_INF, F32)
        l_scr[...] = jnp.zeros(l_scr.shape, F32)
        acc_scr[...] = jnp.zeros(acc_scr.shape, F32)

    q = q_ref[...]
    lane = lax.broadcasted_iota(jnp.int32, q.shape, 1)
    q8 = jnp.concatenate([jnp.where(lane < ATT_HEAD_DIM, q, 0.0), jnp.where(lane >= ATT_HEAD_DIM, q, 0.0)], axis=0)
    rows = k_refs[0].shape[0]
    r = lax.broadcasted_iota(jnp.int32, (2 * H, rows), 0)
    col = lax.broadcasted_iota(jnp.int32, (2 * H, rows), 1)
    own = (col % H) == (r % H)
    s = jnp.concatenate(
        [jnp.where(own, lax.dot_general(q8, kr[...], (((1,), (1,)), ((), ())), preferred_element_type=F32), NEG_INF)
         for kr in k_refs], axis=1)
    m = m_scr[...]
    m_new = jnp.maximum(m, jnp.max(s, axis=-1, keepdims=True))
    alpha = jnp.exp(m - m_new)
    p = jnp.exp(s - m_new)
    pv = jnp.dot(p[:, :rows], v_refs[0][...], preferred_element_type=F32)
    for i in range(1, P):
        pv = pv + jnp.dot(p[:, i * rows:(i + 1) * rows], v_refs[i][...], preferred_element_type=F32)
    m_scr[...] = m_new
    l_scr[...] = alpha * l_scr[...] + jnp.sum(p, axis=-1, keepdims=True)
    acc_scr[...] = alpha * acc_scr[...] + pv

    @pl.when(j == pl.num_programs(1) - 1)
    def _():
        lam = _lambda(lq1_ref, lk1_ref, lq2_ref, lk2_ref)
        kn = jnp.concatenate([kn_ref[...], kn_ref[...]], axis=0)
        vn = jnp.concatenate([vn_ref[...], vn_ref[...]], axis=0)
        s_new = jnp.sum(q8 * kn, axis=-1, keepdims=True)
        m_old = m_scr[...]
        m_fin = jnp.maximum(m_old, s_new)
        a = jnp.exp(m_old - m_fin)
        p_new = jnp.exp(s_new - m_fin)
        l = a * l_scr[...] + p_new
        acc = (a * acc_scr[...] + p_new * vn) / l
        o_ref[...] = _sub_ln(acc[:H] - lam * acc[H:], g_ref[...])


def _decode_attn(q, k_new, v_new, cache_k, cache_v, page_table, lam_vecs, g):
    Bs, n_pages = page_table.shape
    n_pool, page = cache_k.shape[0], cache_k.shape[1]
    P, H = PAGES_PER_STEP, N_ATT_HEADS
    ck = cache_k.reshape(n_pool, page * H, HEAD_W)
    cv = cache_v.reshape(n_pool, page * H, HEAD_W)
    vec = pl.BlockSpec((1, ATT_HEAD_DIM), lambda b, j, pt: (0, 0))
    tok = pl.BlockSpec((None, H, HEAD_W), lambda b, j, pt: (b, 0, 0))
    page_spec = lambda i: pl.BlockSpec((None, page * H, HEAD_W), lambda b, j, pt: (pt[b, j * P + i], 0, 0))
    grid_spec = pltpu.PrefetchScalarGridSpec(
        num_scalar_prefetch=1,
        grid=(Bs, n_pages // P),
        in_specs=[vec, vec, vec, vec, pl.BlockSpec((1, HEAD_W), lambda b, j, pt: (0, 0)), tok, tok, tok]
                 + [page_spec(i) for i in range(P)] * 2,
        out_specs=tok,
        scratch_shapes=[pltpu.VMEM((2 * H, 1), F32), pltpu.VMEM((2 * H, 1), F32), pltpu.VMEM((2 * H, HEAD_W), F32)],
    )
    tok3 = lambda a: a.reshape(Bs, H, HEAD_W)
    o = pl.pallas_call(
        _decode_kernel,
        grid_spec=grid_spec,
        out_shape=jax.ShapeDtypeStruct((Bs, H, HEAD_W), F32),
        compiler_params=_params("parallel", "arbitrary"),
        name="decode_attn",
    )(page_table, *lam_vecs, g, tok3(q), tok3(k_new), tok3(v_new), *([ck] * P), *([cv] * P))
    return o.reshape(Bs, D_ATT)


SCAN_COLS = 512
B_GROUPS = 16
C_GROUPS = 8


def _ssm_kernel(u_ref, bb_ref, lre_ref, lim_ref, c_ref, d_ref, wglu_ref, h0re_ref, h0im_ref,
                z_ref, hre_ref, him_ref, bu_scr, hs_scr, *, tc, bk):
    t0 = pl.program_id(0)

    @pl.when(t0 == 0)
    def _():
        hre_ref[...] = h0re_ref[...]
        him_ref[...] = h0im_ref[...]

    rows = tc * bk
    u = u_ref[...].reshape(rows, D_SSM)
    ub = u.astype(BF16)
    part_w = B_GROUPS * SSM_STATE
    half_w = 2 * part_w
    for half in range(N_SSM_GROUPS // B_GROUPS):
        u_half = ub[:, half * B_GROUPS * SSM_GROUP:(half + 1) * B_GROUPS * SSM_GROUP]
        bu = jnp.dot(u_half, bb_ref[half], preferred_element_type=F32)
        bu_scr[:, :, half * half_w:(half + 1) * half_w] = bu.reshape(tc, bk, half_w)

    for half in range(N_SSM_GROUPS // B_GROUPS):
        for cg in range(part_w // SCAN_COLS):
            st = slice(half * part_w + cg * SCAN_COLS, half * part_w + (cg + 1) * SCAN_COLS)
            re = slice(half * half_w + cg * SCAN_COLS, half * half_w + (cg + 1) * SCAN_COLS)
            im = slice(re.start + part_w, re.stop + part_w)
            lr = jnp.broadcast_to(lre_ref[:, st], (bk, SCAN_COLS))
            li = jnp.broadcast_to(lim_ref[:, st], (bk, SCAN_COLS))

            def step(t, carry, lr=lr, li=li, re=re, im=im):
                hr, hi = carry
                nhr = lr * hr - li * hi + bu_scr[t, :, re]
                nhi = lr * hi + li * hr + bu_scr[t, :, im]
                hs_scr[t, :, re] = nhr
                hs_scr[t, :, im] = nhi
                return nhr, nhi

            hr, hi = lax.fori_loop(0, tc, step, (hre_ref[:, st], him_ref[:, st]))
            hre_ref[:, st] = hr
            him_ref[:, st] = hi

    slab_w = C_GROUPS * SSM_STATE
    y_slabs = []
    for s in range(N_SSM_GROUPS // C_GROUPS):
        re0 = (s * C_GROUPS // B_GROUPS) * half_w + (s * C_GROUPS % B_GROUPS) * SSM_STATE
        h_re = hs_scr[:, :, re0:re0 + slab_w].reshape(rows, slab_w).astype(BF16)
        h_im = hs_scr[:, :, re0 + part_w:re0 + part_w + slab_w].reshape(rows, slab_w).astype(BF16)
        y_slabs.append(jnp.dot(h_re, c_ref[s, 0], preferred_element_type=F32)
                       + jnp.dot(h_im, c_ref[s, 1], preferred_element_type=F32))
    y = jnp.concatenate(y_slabs, axis=1) + d_ref[...] * u
    y = jax.nn.gelu(y)
    gate = jnp.dot(y.astype(BF16), wglu_ref[...], preferred_element_type=F32)
    z_ref[...] = (y * jax.nn.sigmoid(gate)).reshape(tc, bk, D_SSM)


def _ssm(u3, bb, lb_re, lb_im, c_full, d, w_glu, h0_re, h0_im, tc):
    S, bk, _ = u3.shape
    const = lambda shape: pl.BlockSpec(shape, lambda t: (0,) * len(shape))
    seq = pl.BlockSpec((tc, bk, D_SSM), lambda t: (t, 0, 0))
    z, h_re, h_im = pl.pallas_call(
        functools.partial(_ssm_kernel, tc=tc, bk=bk),
        grid=(S // tc,),
        in_specs=[seq, const(bb.shape), const((1, N_STATE)), const((1, N_STATE)),
                  const(c_full.shape), const((1, D_SSM)), const((D_SSM, D_SSM)),
                  const((bk, N_STATE)), const((bk, N_STATE))],
        out_specs=[seq, const((bk, N_STATE)), const((bk, N_STATE))],
        out_shape=[jax.ShapeDtypeStruct((S, bk, D_SSM), F32), jax.ShapeDtypeStruct((bk, N_STATE), F32),
                   jax.ShapeDtypeStruct((bk, N_STATE), F32)],
        scratch_shapes=[pltpu.VMEM((tc, bk, 2 * N_STATE), F32), pltpu.VMEM((tc, bk, 2 * N_STATE), F32)],
        compiler_params=_params("arbitrary"),
        name="ssm",
    )(u3, bb, lb_re.reshape(1, N_STATE), lb_im.reshape(1, N_STATE), c_full, d.reshape(1, D_SSM), w_glu, h0_re, h0_im)
    return z, h_re, h_im


def _ffn_kernel(*refs, tm, decode):
    if decode:
        (x_ref, o_ref, z_ref, wo_ref, gn_ref, wu_ref, cw_ref, cb_ref, wd_ref, fn_ref, p2_ref, p1_ref,
         y_ref, a_out_ref) = refs
    else:
        (x_ref, o_ref, z_ref, wo_ref, gn_ref, wu_ref, cw_ref, cb_ref, wd_ref, fn_ref,
         y_ref, cs_ref, carry_scr) = refs

        @pl.when(pl.program_id(1) == 0)
        def _():
            carry_scr[...] = jnp.zeros(carry_scr.shape, F32)

    mix = (jnp.dot(o_ref[...].astype(BF16), wo_ref[:D_ATT, :], preferred_element_type=F32)
           + jnp.dot(z_ref[...].astype(BF16), wo_ref[D_ATT:, :], preferred_element_type=F32))
    x1 = x_ref[...] + mix
    r = lax.rsqrt(jnp.mean(x1 * x1, axis=-1, keepdims=True) + NORM_EPS)
    hn = (x1 * r * gn_ref[...]).astype(BF16)
    acc = x1
    for j in range(N_FF_CHUNKS):
        cols = slice(j * FF_CHUNK, (j + 1) * FF_CHUNK)
        gcols = slice(D_FF + j * FF_CHUNK, D_FF + (j + 1) * FF_CHUNK)
        a = jnp.dot(hn, wu_ref[:, cols], preferred_element_type=F32)
        gate = jnp.dot(hn, wu_ref[:, gcols], preferred_element_type=F32)
        if decode:
            am2, am1 = p2_ref[:, cols], p1_ref[:, cols]
            a_out_ref[:, cols] = a
        else:
            prev = carry_scr[:, cols]
            p2 = jnp.broadcast_to(prev[CARRY_ROWS - 2:CARRY_ROWS - 1], a.shape)
            p1 = jnp.broadcast_to(prev[CARRY_ROWS - 1:], a.shape)
            row = lax.broadcasted_iota(jnp.int32, a.shape, 0)
            am1 = jnp.where(row < 1, p1, pltpu.roll(a, 1, 0))
            am2 = jnp.where(row < 1, p2, jnp.where(row < 2, p1, pltpu.roll(a, 2, 0)))
            carry_scr[:, cols] = a[tm - CARRY_ROWS:]
        cw = cw_ref[:, cols]
        c = cb_ref[:, cols] + (cw[0:1] * am2 + cw[1:2] * am1 + cw[2:3] * a)
        hidden = (jax.nn.silu(c) * gate).astype(BF16)
        acc = acc + jnp.dot(hidden, wd_ref[cols, :], preferred_element_type=F32)
    r2 = lax.rsqrt(jnp.mean(acc * acc, axis=-1, keepdims=True) + NORM_EPS)
    y_ref[...] = acc * r2 * fn_ref[...]
    if not decode:
        cs_ref[...] = carry_scr[...]


def _ffn(x, o, z, weights, tm, prev=None):
    B, S, D = x.shape
    ns = S // tm
    decode = prev is not None
    wo, gn, wu, cw, cb, wd, fn = weights
    const = lambda shape: pl.BlockSpec(shape, lambda b, s: (0,) * len(shape), pipeline_mode=pl.Buffered(1))
    xs = pl.BlockSpec((None, tm, D), lambda b, s: (b, s, 0))
    in_specs = [xs, pl.BlockSpec((tm, D_ATT), lambda b, s: (b * ns + s, 0)),
                pl.BlockSpec((tm, D_SSM), lambda b, s: (s, b)),
                const((D, D)), const((1, D)), const((D, 2 * D_FF)), const((CONV_W, D_FF)), const((1, D_FF)),
                const((D_FF, D)), const((1, D))]
    args = [x, o, z, wo, gn, wu, cw, cb, wd, fn]
    scratch = []
    if decode:
        rows = pl.BlockSpec((tm, D_FF), lambda b, s: (0, 0))
        in_specs += [rows, rows]
        args += list(prev)
        out_specs = [xs, rows]
        out_shape = [jax.ShapeDtypeStruct((B, S, D), F32), jax.ShapeDtypeStruct((tm, D_FF), F32)]
    else:
        out_specs = [xs, pl.BlockSpec((None, CARRY_ROWS, D_FF), lambda b, s: (b, 0, 0))]
        out_shape = [jax.ShapeDtypeStruct((B, S, D), F32), jax.ShapeDtypeStruct((B, CARRY_ROWS, D_FF), F32)]
        scratch.append(pltpu.VMEM((CARRY_ROWS, D_FF), F32))
    y, extra = pl.pallas_call(
        functools.partial(_ffn_kernel, tm=tm, decode=decode),
        grid=(B, ns),
        in_specs=in_specs, out_specs=out_specs, out_shape=out_shape, scratch_shapes=scratch,
        compiler_params=_params("parallel", "arbitrary"),
        name="ffn_decode" if decode else "ffn",
    )(*args)
    return y, extra


def kernel(x_prompt, x_sample, cache_k, cache_v, state_ssm_re, state_ssm_im, state_conv, page_table, norm_mix, w_in, lambda_q1, lambda_k1, lambda_q2, lambda_k2, subln_g, ssm_a_re, ssm_a_im, ssm_log_dt, ssm_b_re, ssm_b_im, ssm_c_re, ssm_c_im, ssm_d, w_glu, w_o, norm_ffn, w_up, conv_w, conv_b, w_down, final_norm):
    assert norm_mix.shape[0] == 1, "single layer"
    Bp, Sp, D = x_prompt.shape
    Bs = x_sample.shape[0]
    G, N, C = N_SSM_GROUPS, SSM_STATE, SSM_GROUP
    past_len = page_table.shape[1] * cache_k.shape[2]

    w_in_b = w_in[0].astype(BF16)
    wo_b = w_o[0].astype(BF16)
    wglu_b = w_glu[0].astype(BF16)
    ffn_w = (wo_b, norm_ffn[0].reshape(1, D), w_up[0].astype(BF16), conv_w[0], conv_b[0].reshape(1, D_FF),
             w_down[0].astype(BF16), final_norm.reshape(1, D))
    lam_vecs = [v.reshape(1, ATT_HEAD_DIM) for v in (lambda_q1[0], lambda_k1[0], lambda_q2[0], lambda_k2[0])]
    g_sub = subln_g[0].reshape(1, HEAD_W)

    lb_re, lb_im, bbt_re, bbt_im = _ssm_prep(ssm_a_re[0], ssm_a_im[0], ssm_log_dt[0], ssm_b_re[0], ssm_b_im[0])
    eye_b = jnp.eye(B_GROUPS, dtype=F32)[None, :, None, :, None]
    place_b = lambda bbt: (bbt.reshape(G // B_GROUPS, B_GROUPS, C, 1, N) * eye_b).reshape(
        G // B_GROUPS, B_GROUPS * C, B_GROUPS * N)
    bb_blk = jnp.concatenate([place_b(bbt_re), place_b(bbt_im)], axis=2).astype(BF16)
    eye_c = jnp.eye(C_GROUPS, dtype=F32)[None, :, None, :, None]
    place_c = lambda c: (jnp.transpose(c, (0, 2, 1)).reshape(G // C_GROUPS, C_GROUPS, N, 1, C) * eye_c).reshape(
        G // C_GROUPS, C_GROUPS * N, C_GROUPS * C)
    c_blk = jnp.stack([place_c(ssm_c_re[0]), -place_c(ssm_c_im[0])], axis=1).astype(BF16)

    tabs_p = _rope_tables(jnp.arange(Sp, dtype=jnp.int32))
    q, k, v, u = _in_proj(x_prompt, norm_mix[0], w_in_b, tabs_p, tm=512)
    o = _attn(q, k, v, lam_vecs, g_sub, Bp, Sp, tq=256, nh=2)
    zeros_h = jnp.zeros((Bp, N_STATE), F32)
    z, hre_p, him_p = _ssm(u.reshape(Sp, Bp, D_SSM), bb_blk, lb_re, lb_im, c_blk, ssm_d[0], wglu_b,
                           zeros_h, zeros_h, tc=64)
    y_prompt, cs_p = _ffn(x_prompt, o, z.reshape(Sp, Bp * D_SSM), ffn_w, tm=512)
    conv_prompt = cs_p[:, CARRY_ROWS - (CONV_W - 1):]

    tabs_s = _rope_tables(jnp.full((Bs,), past_len, jnp.int32))
    qs, ks, vs, us = _in_proj(x_sample.reshape(1, Bs, D), norm_mix[0], w_in_b, tabs_s, tm=Bs)
    os_ = _decode_attn(qs, ks, vs, cache_k[0], cache_v[0], page_table, lam_vecs, g_sub)
    zs, hre_s, him_s = _ssm(us.reshape(1, Bs, D_SSM), bb_blk, lb_re, lb_im, c_blk, ssm_d[0], wglu_b,
                            state_ssm_re[0].reshape(Bs, N_STATE), state_ssm_im[0].reshape(Bs, N_STATE), tc=1)
    sc = state_conv[0]
    y_s, a_s = _ffn(x_sample.reshape(1, Bs, D), os_, zs.reshape(Bs, D_SSM), ffn_w, tm=Bs, prev=(sc[:, 0], sc[:, 1]))
    conv_sample = jnp.stack([sc[:, 1], a_s], axis=1)

    st = lambda h, b: h.reshape(1, b, G, N)
    return (y_prompt, y_s.reshape(Bs, 1, D),
            k.reshape(1, Bp, Sp, N_ATT_HEADS, HEAD_W), v.reshape(1, Bp, Sp, N_ATT_HEADS, HEAD_W),
            st(hre_p, Bp), st(him_p, Bp), conv_prompt[None],
            ks.reshape(1, Bs, 1, N_ATT_HEADS, HEAD_W), vs.reshape(1, Bs, 1, N_ATT_HEADS, HEAD_W),
            st(hre_s, Bs), st(him_s, Bs), conv_sample[None])
```

```python
import functools
import math

import jax
import jax.numpy as jnp
from jax import lax
from jax.experimental import pallas as pl
from jax.experimental.pallas import tpu as pltpu

F32 = jnp.float32
BF16 = jnp.bfloat16

D_MODEL = 1024
D_ATT = 512
D_SSM = 512
ATT_HEAD_DIM = 64
N_ATT_HEADS = 4
HEAD_W = 2 * ATT_HEAD_DIM
ROT_DIM = 16
ROPE_THETA = 500000.0
SSM_GROUP = 16
N_SSM_GROUPS = 32
SSM_STATE = 64
N_STATE = N_SSM_GROUPS * SSM_STATE
D_FF = 2816
CONV_W = 3
NORM_EPS = 1e-6
SUBLN_EPS = 1e-5
NEG_INF = -1e30
D_PROJ = 3 * D_ATT + D_SSM
LAMBDA_INIT = 0.8 - 0.6 * math.exp(-0.3 * 0)
LOG2_E = math.log2(math.e)

V7X_VMEM_LIMIT = 56 * 1024 * 1024
FF_CHUNK = 256
N_FF_CHUNKS = D_FF // FF_CHUNK
PAGES_PER_STEP = 16
CARRY_ROWS = 8


def _params(*sem):
    return pltpu.CompilerParams(dimension_semantics=sem, vmem_limit_bytes=V7X_VMEM_LIMIT)


def _lambda(lq1_ref, lk1_ref, lq2_ref, lk2_ref):
    s1 = jnp.sum(lq1_ref[...] * lk1_ref[...], axis=-1, keepdims=True)
    s2 = jnp.sum(lq2_ref[...] * lk2_ref[...], axis=-1, keepdims=True)
    return jnp.exp(s1) - jnp.exp(s2) + LAMBDA_INIT


def _sub_ln(o, g):
    r = lax.rsqrt(jnp.mean(o * o, axis=-1, keepdims=True) + SUBLN_EPS)
    return (o * r * g) * (1.0 - LAMBDA_INIT)


def _ssm_prep_kernel(ar_ref, ai_ref, ldt_ref, art_ref, ait_ref, brt_ref, bit_ref,
                     lbre_ref, lbim_ref, bbre_ref, bbim_ref):
    dt = jnp.exp(ldt_ref[...])

    def disc(ar, ai):
        mag = jnp.exp(ar * dt)
        return mag * jnp.cos(ai * dt), mag * jnp.sin(ai * dt)

    lb_re, lb_im = disc(ar_ref[...], ai_ref[...])
    lbre_ref[...] = lb_re
    lbim_ref[...] = lb_im
    ar, ai = art_ref[...], ait_ref[...]
    t_re, t_im = disc(ar, ai)
    den = ar * ar + ai * ai
    nr, ni = t_re - 1.0, t_im
    f_re = (nr * ar + ni * ai) / den
    f_im = (ni * ar - nr * ai) / den
    br, bi = brt_ref[...], bit_ref[...]
    bbre_ref[...] = f_re * br - f_im * bi
    bbim_ref[...] = f_re * bi + f_im * br


def _ssm_prep(a_re, a_im, log_dt, b_re, b_im):
    G, N, C = N_SSM_GROUPS, SSM_STATE, SSM_GROUP
    tile = lambda a: jnp.tile(a, (1, C))
    flat = lambda b: jnp.transpose(b, (0, 2, 1)).reshape(G, C * N)
    outs = pl.pallas_call(
        _ssm_prep_kernel,
        out_shape=[jax.ShapeDtypeStruct((G, N), F32)] * 2 + [jax.ShapeDtypeStruct((G, C * N), F32)] * 2,
        name="ssm_prep",
    )(a_re, a_im, log_dt.reshape(G, 1), tile(a_re), tile(a_im), flat(b_re), flat(b_im))
    return outs


def _in_proj_kernel(x_ref, g_ref, w_ref, cos_ref, sa_ref, sb_ref, q_ref, k_ref, v_ref, u_ref):
    x = x_ref[...]
    r = lax.rsqrt(jnp.mean(x * x, axis=-1, keepdims=True) + NORM_EPS)
    hn = (x * r * g_ref[...]).astype(BF16)
    proj = jnp.dot(hn, w_ref[...], preferred_element_type=F32)
    cos, sa, sb = cos_ref[...], sa_ref[...], sb_ref[...]
    half = ROT_DIM // 2

    def rope(xh):
        return xh * cos + pltpu.roll(xh, HEAD_W - half, 1) * sa + pltpu.roll(xh, half, 1) * sb

    for h in range(N_ATT_HEADS):
        lo, hi = h * HEAD_W, (h + 1) * HEAD_W
        q_ref[:, lo:hi] = rope(proj[:, lo:hi]) * (ATT_HEAD_DIM ** -0.5)
        k_ref[:, lo:hi] = rope(proj[:, D_ATT + lo:D_ATT + hi])
    v_ref[...] = proj[:, 2 * D_ATT:3 * D_ATT]
    u_ref[...] = proj[:, 3 * D_ATT:]


def _rope_tables(pos):
    half = ROT_DIM // 2
    inv = ROPE_THETA ** (-jnp.arange(0, ROT_DIM, 2, dtype=F32) / ROT_DIM)
    ang = pos.astype(F32)[:, None] * inv[None, :]
    cos, sin = jnp.cos(ang), jnp.sin(ang)
    S = pos.shape[0]
    pad = jnp.zeros((S, ATT_HEAD_DIM - ROT_DIM), F32)
    zh = jnp.zeros((S, half), F32)
    comp = lambda a, b, c: jnp.concatenate([a, b, c], axis=1)
    two = lambda t: jnp.concatenate([t, t], axis=1)
    return (two(comp(cos, cos, pad + 1.0)), two(comp(-sin, zh, pad)), two(comp(zh, sin, pad)))


def _in_proj(x, g, w_bf16, tables, tm):
    B, S, D = x.shape
    ns = S // tm
    row = lambda b, s: (b * ns + s, 0)
    tab = pl.BlockSpec((tm, HEAD_W), lambda b, s: (s, 0))
    qkv_spec = pl.BlockSpec((tm, D_ATT), row)
    q, k, v, u = pl.pallas_call(
        _in_proj_kernel,
        grid=(B, ns),
        in_specs=[pl.BlockSpec((None, tm, D), lambda b, s: (b, s, 0)),
                  pl.BlockSpec((1, D), lambda b, s: (0, 0)),
                  pl.BlockSpec((D, D_PROJ), lambda b, s: (0, 0)),
                  tab, tab, tab],
        out_specs=[qkv_spec, qkv_spec, qkv_spec, pl.BlockSpec((tm, D_SSM), lambda b, s: (s, b))],
        out_shape=[jax.ShapeDtypeStruct((B * S, D_ATT), F32)] * 3 + [jax.ShapeDtypeStruct((S, B * D_SSM), F32)],
        compiler_params=_params("parallel", "parallel"),
        name="in_proj",
    )(x, g.reshape(1, D), w_bf16, *tables)
    return q, k, v, u


def _attn_kernel(lq1_ref, lk1_ref, lq2_ref, lk2_ref, g_ref, q_ref, k_ref, v_ref, o_ref, *, tq):
    i = pl.program_id(2)
    lam = _lambda(lq1_ref, lk1_ref, lq2_ref, lk2_ref)
    nh = q_ref.shape[1] // HEAD_W
    feat = lax.broadcasted_iota(jnp.int32, (HEAD_W, tq), 0)
    units = [(h, c) for h in range(nh) for c in range(2)]
    hs = lambda h: slice(h * HEAD_W, (h + 1) * HEAD_W)
    q_t = []
    for h, c in units:
        qh_t = (q_ref[:, hs(h)] * LOG2_E).T
        own = (feat < ATT_HEAD_DIM) if c == 0 else (feat >= ATT_HEAD_DIM)
        q_t.append(jnp.where(own, qh_t, 0.0))

    ones_rows = jnp.ones((8, tq), F32)

    def block(j, carry, masked):
        start = pl.multiple_of(j * tq, tq)
        n = range(len(units))
        s = [jnp.dot(k_ref[pl.ds(start, tq), hs(h)], q_t[u], preferred_element_type=F32)
             for u, (h, c) in enumerate(units)]
        if masked:
            key = lax.broadcasted_iota(jnp.int32, (tq, tq), 0)
            qry = lax.broadcasted_iota(jnp.int32, (tq, tq), 1)
            s = [jnp.where(key <= qry, su, NEG_INF) for su in s]
        m_new = [jnp.maximum(carry[u][0], jnp.max(s[u], axis=0, keepdims=True)) for u in n]
        alpha = [jnp.exp2(carry[u][0] - m_new[u]) for u in n]
        p = [jnp.exp2(s[u] - m_new[u]) for u in n]
        pv = [jnp.dot(jnp.concatenate([v_ref[pl.ds(start, tq), hs(h)].T, ones_rows], axis=0), p[u],
                      preferred_element_type=F32) for u, (h, c) in enumerate(units)]
        return tuple((m_new[u], alpha[u] * carry[u][1] + pv[u][HEAD_W:HEAD_W + 1],
                      alpha[u] * carry[u][2] + pv[u][:HEAD_W]) for u in n)

    init = tuple((jnp.full((1, tq), NEG_INF, F32), jnp.zeros((1, tq), F32), jnp.zeros((HEAD_W, tq), F32))
                 for _ in units)
    carry = lax.fori_loop(0, i, lambda j, c: block(j, c, False), init)
    carry = block(i, carry, True)
    g_col = g_ref[...]
    for h in range(nh):
        (_, l1, acc1), (_, l2, acc2) = carry[2 * h], carry[2 * h + 1]
        o_t = acc1 / l1 - lam * (acc2 / l2)
        r = lax.rsqrt(jnp.mean(o_t * o_t, axis=0, keepdims=True) + SUBLN_EPS)
        o_ref[:, h * HEAD_W:(h + 1) * HEAD_W] = ((o_t * r * g_col) * (1.0 - LAMBDA_INIT)).T


def _attn(q, k, v, lam_vecs, g, B, S, tq, nh):
    nq = S // tq
    vec = pl.BlockSpec((1, ATT_HEAD_DIM), lambda b, h, i: (0, 0))
    kv = pl.BlockSpec((S, nh * HEAD_W), lambda b, h, i: (b, h))
    qo = pl.BlockSpec((tq, nh * HEAD_W), lambda b, h, i: (b * nq + i, h))
    return pl.pallas_call(
        functools.partial(_attn_kernel, tq=tq),
        grid=(B, N_ATT_HEADS // nh, nq),
        in_specs=[vec, vec, vec, vec, pl.BlockSpec((HEAD_W, 1), lambda b, h, i: (0, 0)), qo, kv, kv],
        out_specs=qo,
        out_shape=jax.ShapeDtypeStruct((B * S, D_ATT), F32),
        compiler_params=_params("parallel", "parallel", "parallel"),
        name="attn",
    )(*lam_vecs, g.reshape(HEAD_W, 1), q, k, v)


def _decode_kernel(pt_ref, lq1_ref, lk1_ref, lq2_ref, lk2_ref, g_ref, q_ref, kn_ref, vn_ref, *rest):
    del pt_ref
    P = PAGES_PER_STEP
    k_refs, v_refs = rest[:P], rest[P:2 * P]
    o_ref, m_scr, l_scr, acc_scr = rest[2 * P:]
    j = pl.program_id(1)
    H = N_ATT_HEADS

    @pl.when(j == 0)
    def _():
        m_scr[...] = jnp.full(m_scr.shape, NEG_INF, F32)
        l_scr[...] = jnp.zeros(l_scr.shape, F32)
        acc_scr[...] = jnp.zeros(acc_scr.shape, F32)

    q = q_ref[...]
    lane = lax.broadcasted_iota(jnp.int32, q.shape, 1)
    q8 = jnp.concatenate([jnp.where(lane < ATT_HEAD_DIM, q, 0.0), jnp.where(lane >= ATT_HEAD_DIM, q, 0.0)], axis=0)
    rows = k_refs[0].shape[0]
    r = lax.broadcasted_iota(jnp.int32, (2 * H, rows), 0)
    col = lax.broadcasted_iota(jnp.int32, (2 * H, rows), 1)
    own = (col % H) == (r % H)
    s = jnp.concatenate(
        [jnp.where(own, lax.dot_general(q8, kr[...], (((1,), (1,)), ((), ())), preferred_element_type=F32), NEG_INF)
         for kr in k_refs], axis=1)
    m = m_scr[...]
    m_new = jnp.maximum(m, jnp.max(s, axis=-1, keepdims=True))
    alpha = jnp.exp(m - m_new)
    p = jnp.exp(s - m_new)
    pv = jnp.dot(p[:, :rows], v_refs[0][...], preferred_element_type=F32)
    for i in range(1, P):
        pv = pv + jnp.dot(p[:, i * rows:(i + 1) * rows], v_refs[i][...], preferred_element_type=F32)
    m_scr[...] = m_new
    l_scr[...] = alpha * l_scr[...] + jnp.sum(p, axis=-1, keepdims=True)
    acc_scr[...] = alpha * acc_scr[...] + pv

    @pl.when(j == pl.num_programs(1) - 1)
    def _():
        lam = _lambda(lq1_ref, lk1_ref, lq2_ref, lk2_ref)
        kn = jnp.concatenate([kn_ref[...], kn_ref[...]], axis=0)
        vn = jnp.concatenate([vn_ref[...], vn_ref[...]], axis=0)
        s_new = jnp.sum(q8 * kn, axis=-1, keepdims=True)
        m_old = m_scr[...]
        m_fin = jnp.maximum(m_old, s_new)
        a = jnp.exp(m_old - m_fin)
        p_new = jnp.exp(s_new - m_fin)
        l = a * l_scr[...] + p_new
        acc = (a * acc_scr[...] + p_new * vn) / l
        o_ref[...] = _sub_ln(acc[:H] - lam * acc[H:], g_ref[...])


def _decode_attn(q, k_new, v_new, cache_k, cache_v, page_table, lam_vecs, g):
    Bs, n_pages = page_table.shape
    n_pool, page = cache_k.shape[0], cache_k.shape[1]
    P, H = PAGES_PER_STEP, N_ATT_HEADS
    ck = cache_k.reshape(n_pool, page * H, HEAD_W)
    cv = cache_v.reshape(n_pool, page * H, HEAD_W)
    vec = pl.BlockSpec((1, ATT_HEAD_DIM), lambda b, j, pt: (0, 0))
    tok = pl.BlockSpec((None, H, HEAD_W), lambda b, j, pt: (b, 0, 0))
    page_spec = lambda i: pl.BlockSpec((None, page * H, HEAD_W), lambda b, j, pt: (pt[b, j * P + i], 0, 0))
    grid_spec = pltpu.PrefetchScalarGridSpec(
        num_scalar_prefetch=1,
        grid=(Bs, n_pages // P),
        in_specs=[vec, vec, vec, vec, pl.BlockSpec((1, HEAD_W), lambda b, j, pt: (0, 0)), tok, tok, tok]
                 + [page_spec(i) for i in range(P)] * 2,
        out_specs=tok,
        scratch_shapes=[pltpu.VMEM((2 * H, 1), F32), pltpu.VMEM((2 * H, 1), F32), pltpu.VMEM((2 * H, HEAD_W), F32)],
    )
    tok3 = lambda a: a.reshape(Bs, H, HEAD_W)
    o = pl.pallas_call(
        _decode_kernel,
        grid_spec=grid_spec,
        out_shape=jax.ShapeDtypeStruct((Bs, H, HEAD_W), F32),
        compiler_params=_params("parallel", "arbitrary"),
        name="decode_attn",
    )(page_table, *lam_vecs, g, tok3(q), tok3(k_new), tok3(v_new), *([ck] * P), *([cv] * P))
    return o.reshape(Bs, D_ATT)


SCAN_COLS = 512
B_GROUPS = 16
C_GROUPS = 8


def _ssm_kernel(u_ref, bb_ref, lre_ref, lim_ref, c_ref, d_ref, wglu_ref, h0re_ref, h0im_ref,
                z_ref, hre_ref, him_ref, bu_scr, hs_scr, *, tc, bk):
    t0 = pl.program_id(0)

    @pl.when(t0 == 0)
    def _():
        hre_ref[...] = h0re_ref[...]
        him_ref[...] = h0im_ref[...]

    rows = tc * bk
    u = u_ref[...].reshape(rows, D_SSM)
    ub = u.astype(BF16)
    part_w = B_GROUPS * SSM_STATE
    half_w = 2 * part_w
    for half in range(N_SSM_GROUPS // B_GROUPS):
        u_half = ub[:, half * B_GROUPS * SSM_GROUP:(half + 1) * B_GROUPS * SSM_GROUP]
        bu = jnp.dot(u_half, bb_ref[half], preferred_element_type=F32)
        bu_scr[:, :, half * half_w:(half + 1) * half_w] = bu.reshape(tc, bk, half_w)

    for half in range(N_SSM_GROUPS // B_GROUPS):
        for cg in range(part_w // SCAN_COLS):
            st = slice(half * part_w + cg * SCAN_COLS, half * part_w + (cg + 1) * SCAN_COLS)
            re = slice(half * half_w + cg * SCAN_COLS, half * half_w + (cg + 1) * SCAN_COLS)
            im = slice(re.start + part_w, re.stop + part_w)
            lr = jnp.broadcast_to(lre_ref[:, st], (bk, SCAN_COLS))
            li = jnp.broadcast_to(lim_ref[:, st], (bk, SCAN_COLS))

            def step(t, carry, lr=lr, li=li, re=re, im=im):
                hr, hi = carry
                nhr = lr * hr - li * hi + bu_scr[t, :, re]
                nhi = lr * hi + li * hr + bu_scr[t, :, im]
                hs_scr[t, :, re] = nhr
                hs_scr[t, :, im] = nhi
                return nhr, nhi

            hr, hi = lax.fori_loop(0, tc, step, (hre_ref[:, st], him_ref[:, st]))
            hre_ref[:, st] = hr
            him_ref[:, st] = hi

    slab_w = C_GROUPS * SSM_STATE
    y_slabs = []
    for s in range(N_SSM_GROUPS // C_GROUPS):
        re0 = (s * C_GROUPS // B_GROUPS) * half_w + (s * C_GROUPS % B_GROUPS) * SSM_STATE
        h_re = hs_scr[:, :, re0:re0 + slab_w].reshape(rows, slab_w).astype(BF16)
        h_im = hs_scr[:, :, re0 + part_w:re0 + part_w + slab_w].reshape(rows, slab_w).astype(BF16)
        y_slabs.append(jnp.dot(h_re, c_ref[s, 0], preferred_element_type=F32)
                       + jnp.dot(h_im, c_ref[s, 1], preferred_element_type=F32))
    y = jnp.concatenate(y_slabs, axis=1) + d_ref[...] * u
    y = jax.nn.gelu(y)
    gate = jnp.dot(y.astype(BF16), wglu_ref[...], preferred_element_type=F32)
    z_ref[...] = (y * jax.nn.sigmoid(gate)).reshape(tc, bk, D_SSM)


def _ssm(u3, bb, lb_re, lb_im, c_full, d, w_glu, h0_re, h0_im, tc):
    S, bk, _ = u3.shape
    const = lambda shape: pl.BlockSpec(shape, lambda t: (0,) * len(shape))
    seq = pl.BlockSpec((tc, bk, D_SSM), lambda t: (t, 0, 0))
    z, h_re, h_im = pl.pallas_call(
        functools.partial(_ssm_kernel, tc=tc, bk=bk),
        grid=(S // tc,),
        in_specs=[seq, const(bb.shape), const((1, N_STATE)), const((1, N_STATE)),
                  const(c_full.shape), const((1, D_SSM)), const((D_SSM, D_SSM)),
                  const((bk, N_STATE)), const((bk, N_STATE))],
        out_specs=[seq, const((bk, N_STATE)), const((bk, N_STATE))],
        out_shape=[jax.ShapeDtypeStruct((S, bk, D_SSM), F32), jax.ShapeDtypeStruct((bk, N_STATE), F32),
                   jax.ShapeDtypeStruct((bk, N_STATE), F32)],
        scratch_shapes=[pltpu.VMEM((tc, bk, 2 * N_STATE), F32), pltpu.VMEM((tc, bk, 2 * N_STATE), F32)],
        compiler_params=_params("arbitrary"),
        name="ssm",
    )(u3, bb, lb_re.reshape(1, N_STATE), lb_im.reshape(1, N_STATE), c_full, d.reshape(1, D_SSM), w_glu, h0_re, h0_im)
    return z, h_re, h_im


def _ffn_kernel(*refs, tm, decode):
    if decode:
        (x_ref, o_ref, z_ref, wo_ref, gn_ref, wu_ref, cw_ref, cb_ref, wd_ref, fn_ref, p2_ref, p1_ref,
         y_ref, a_out_ref, hid_scr) = refs
    else:
        (x_ref, o_ref, z_ref, wo_ref, gn_ref, wu_ref, cw_ref, cb_ref, wd_ref, fn_ref,
         y_ref, cs_ref, hid_scr, carry_scr) = refs

        @pl.when(pl.program_id(1) == 0)
        def _():
            carry_scr[...] = jnp.zeros(carry_scr.shape, F32)

    mix = (jnp.dot(o_ref[...].astype(BF16), wo_ref[:D_ATT, :], preferred_element_type=F32)
           + jnp.dot(z_ref[...].astype(BF16), wo_ref[D_ATT:, :], preferred_element_type=F32))
    x1 = x_ref[...] + mix
    r = lax.rsqrt(jnp.mean(x1 * x1, axis=-1, keepdims=True) + NORM_EPS)
    hn = (x1 * r * gn_ref[...]).astype(BF16)
    for j in range(N_FF_CHUNKS):
        cols = slice(j * FF_CHUNK, (j + 1) * FF_CHUNK)
        gcols = slice(D_FF + j * FF_CHUNK, D_FF + (j + 1) * FF_CHUNK)
        a = jnp.dot(hn, wu_ref[:, cols], preferred_element_type=F32)
        gate = jnp.dot(hn, wu_ref[:, gcols], preferred_element_type=F32)
        if decode:
            am2, am1 = p2_ref[:, cols], p1_ref[:, cols]
            a_out_ref[:, cols] = a
        else:
            prev = carry_scr[:, cols]
            p2 = jnp.broadcast_to(prev[CARRY_ROWS - 2:CARRY_ROWS - 1], a.shape)
            p1 = jnp.broadcast_to(prev[CARRY_ROWS - 1:], a.shape)
            row = lax.broadcasted_iota(jnp.int32, a.shape, 0)
            am1 = jnp.where(row < 1, p1, pltpu.roll(a, 1, 0))
            am2 = jnp.where(row < 1, p2, jnp.where(row < 2, p1, pltpu.roll(a, 2, 0)))
            carry_scr[:, cols] = a[tm - CARRY_ROWS:]
        cw = cw_ref[:, cols]
        c = cb_ref[:, cols] + (cw[0:1] * am2 + cw[1:2] * am1 + cw[2:3] * a)
        hid_scr[:, cols] = (jax.nn.silu(c) * gate).astype(BF16)
    acc = x1 + jnp.dot(hid_scr[...], wd_ref[...], preferred_element_type=F32)
    r2 = lax.rsqrt(jnp.mean(acc * acc, axis=-1, keepdims=True) + NORM_EPS)
    y_ref[...] = acc * r2 * fn_ref[...]
    if not decode:
        cs_ref[...] = carry_scr[...]


def _ffn(x, o, z, weights, tm, prev=None):
    B, S, D = x.shape
    ns = S // tm
    decode = prev is not None
    wo, gn, wu, cw, cb, wd, fn = weights
    const = lambda shape: pl.BlockSpec(shape, lambda b, s: (0,) * len(shape), pipeline_mode=pl.Buffered(1))
    xs = pl.BlockSpec((None, tm, D), lambda b, s: (b, s, 0))
    in_specs = [xs, pl.BlockSpec((tm, D_ATT), lambda b, s: (b * ns + s, 0)),
                pl.BlockSpec((tm, D_SSM), lambda b, s: (s, b)),
                const((D, D)), const((1, D)), const((D, 2 * D_FF)), const((CONV_W, D_FF)), const((1, D_FF)),
                const((D_FF, D)), const((1, D))]
    args = [x, o, z, wo, gn, wu, cw, cb, wd, fn]
    scratch = [pltpu.VMEM((tm, D_FF), BF16)]
    if decode:
        rows = pl.BlockSpec((tm, D_FF), lambda b, s: (0, 0))
        in_specs += [rows, rows]
        args += list(prev)
        out_specs = [xs, rows]
        out_shape = [jax.ShapeDtypeStruct((B, S, D), F32), jax.ShapeDtypeStruct((tm, D_FF), F32)]
    else:
        out_specs = [xs, pl.BlockSpec((None, CARRY_ROWS, D_FF), lambda b, s: (b, 0, 0))]
        out_shape = [jax.ShapeDtypeStruct((B, S, D), F32), jax.ShapeDtypeStruct((B, CARRY_ROWS, D_FF), F32)]
        scratch.append(pltpu.VMEM((CARRY_ROWS, D_FF), F32))
    y, extra = pl.pallas_call(
        functools.partial(_ffn_kernel, tm=tm, decode=decode),
        grid=(B, ns),
        in_specs=in_specs, out_specs=out_specs, out_shape=out_shape, scratch_shapes=scratch,
        compiler_params=_params("parallel", "arbitrary"),
        name="ffn_decode" if decode else "ffn",
    )(*args)
    return y, extra


def kernel(x_prompt, x_sample, cache_k, cache_v, state_ssm_re, state_ssm_im, state_conv, page_table, norm_mix, w_in, lambda_q1, lambda_k1, lambda_q2, lambda_k2, subln_g, ssm_a_re, ssm_a_im, ssm_log_dt, ssm_b_re, ssm_b_im, ssm_c_re, ssm_c_im, ssm_d, w_glu, w_o, norm_ffn, w_up, conv_w, conv_b, w_down, final_norm):
    assert norm_mix.shape[0] == 1, "single layer"
    Bp, Sp, D = x_prompt.shape
    Bs = x_sample.shape[0]
    G, N, C = N_SSM_GROUPS, SSM_STATE, SSM_GROUP
    past_len = page_table.shape[1] * cache_k.shape[2]

    w_in_b = w_in[0].astype(BF16)
    wo_b = w_o[0].astype(BF16)
    wglu_b = w_glu[0].astype(BF16)
    ffn_w = (wo_b, norm_ffn[0].reshape(1, D), w_up[0].astype(BF16), conv_w[0], conv_b[0].reshape(1, D_FF),
             w_down[0].astype(BF16), final_norm.reshape(1, D))
    lam_vecs = [v.reshape(1, ATT_HEAD_DIM) for v in (lambda_q1[0], lambda_k1[0], lambda_q2[0], lambda_k2[0])]
    g_sub = subln_g[0].reshape(1, HEAD_W)

    lb_re, lb_im, bbt_re, bbt_im = _ssm_prep(ssm_a_re[0], ssm_a_im[0], ssm_log_dt[0], ssm_b_re[0], ssm_b_im[0])
    eye_b = jnp.eye(B_GROUPS, dtype=F32)[None, :, None, :, None]
    place_b = lambda bbt: (bbt.reshape(G // B_GROUPS, B_GROUPS, C, 1, N) * eye_b).reshape(
        G // B_GROUPS, B_GROUPS * C, B_GROUPS * N)
    bb_blk = jnp.concatenate([place_b(bbt_re), place_b(bbt_im)], axis=2).astype(BF16)
    eye_c = jnp.eye(C_GROUPS, dtype=F32)[None, :, None, :, None]
    place_c = lambda c: (jnp.transpose(c, (0, 2, 1)).reshape(G // C_GROUPS, C_GROUPS, N, 1, C) * eye_c).reshape(
        G // C_GROUPS, C_GROUPS * N, C_GROUPS * C)
    c_blk = jnp.stack([place_c(ssm_c_re[0]), -place_c(ssm_c_im[0])], axis=1).astype(BF16)

    tabs_p = _rope_tables(jnp.arange(Sp, dtype=jnp.int32))
    q, k, v, u = _in_proj(x_prompt, norm_mix[0], w_in_b, tabs_p, tm=512)
    o = _attn(q, k, v, lam_vecs, g_sub, Bp, Sp, tq=256, nh=4)
    zeros_h = jnp.zeros((Bp, N_STATE), F32)
    z, hre_p, him_p = _ssm(u.reshape(Sp, Bp, D_SSM), bb_blk, lb_re, lb_im, c_blk, ssm_d[0], wglu_b,
                           zeros_h, zeros_h, tc=64)
    y_prompt, cs_p = _ffn(x_prompt, o, z.reshape(Sp, Bp * D_SSM), ffn_w, tm=512)
    conv_prompt = cs_p[:, CARRY_ROWS - (CONV_W - 1):]

    tabs_s = _rope_tables(jnp.full((Bs,), past_len, jnp.int32))
    qs, ks, vs, us = _in_proj(x_sample.reshape(1, Bs, D), norm_mix[0], w_in_b, tabs_s, tm=Bs)
    os_ = _decode_attn(qs, ks, vs, cache_k[0], cache_v[0], page_table, lam_vecs, g_sub)
    zs, hre_s, him_s = _ssm(us.reshape(1, Bs, D_SSM), bb_blk, lb_re, lb_im, c_blk, ssm_d[0], wglu_b,
                            state_ssm_re[0].reshape(Bs, N_STATE), state_ssm_im[0].reshape(Bs, N_STATE), tc=1)
    sc = state_conv[0]
    y_s, a_s = _ffn(x_sample.reshape(1, Bs, D), os_, zs.reshape(Bs, D_SSM), ffn_w, tm=Bs, prev=(sc[:, 0], sc[:, 1]))
    conv_sample = jnp.stack([sc[:, 1], a_s], axis=1)

    st = lambda h, b: h.reshape(1, b, G, N)
    return (y_prompt, y_s.reshape(Bs, 1, D),
            k.reshape(1, Bp, Sp, N_ATT_HEADS, HEAD_W), v.reshape(1, Bp, Sp, N_ATT_HEADS, HEAD_W),
            st(hre_p, Bp), st(him_p, Bp), conv_prompt[None],
            ks.reshape(1, Bs, 1, N_ATT_HEADS, HEAD_W), vs.reshape(1, Bs, 1, N_ATT_HEADS, HEAD_W),
            st(hre_s, Bs), st(him_s, Bs), conv_sample[None])
```

```python
import functools
import math

import jax
import jax.numpy as jnp
from jax import lax
from jax.experimental import pallas as pl
from jax.experimental.pallas import tpu as pltpu

F32 = jnp.float32
BF16 = jnp.bfloat16

D_MODEL = 1024
D_ATT = 512
D_SSM = 512
ATT_HEAD_DIM = 64
N_ATT_HEADS = 4
HEAD_W = 2 * ATT_HEAD_DIM
ROT_DIM = 16
ROPE_THETA = 500000.0
SSM_GROUP = 16
N_SSM_GROUPS = 32
SSM_STATE = 64
N_STATE = N_SSM_GROUPS * SSM_STATE
D_FF = 2816
CONV_W = 3
NORM_EPS = 1e-6
SUBLN_EPS = 1e-5
NEG_INF = -1e30
D_PROJ = 3 * D_ATT + D_SSM
LAMBDA_INIT = 0.8 - 0.6 * math.exp(-0.3 * 0)
LOG2_E = math.log2(math.e)

V7X_VMEM_LIMIT = 56 * 1024 * 1024
FF_CHUNK = 256
N_FF_CHUNKS = D_FF // FF_CHUNK
PAGES_PER_STEP = 32
CARRY_ROWS = 8


def _params(*sem):
    return pltpu.CompilerParams(dimension_semantics=sem, vmem_limit_bytes=V7X_VMEM_LIMIT)


def _lambda(lq1_ref, lk1_ref, lq2_ref, lk2_ref):
    s1 = jnp.sum(lq1_ref[...] * lk1_ref[...], axis=-1, keepdims=True)
    s2 = jnp.sum(lq2_ref[...] * lk2_ref[...], axis=-1, keepdims=True)
    return jnp.exp(s1) - jnp.exp(s2) + LAMBDA_INIT


def _sub_ln(o, g):
    r = lax.rsqrt(jnp.mean(o * o, axis=-1, keepdims=True) + SUBLN_EPS)
    return (o * r * g) * (1.0 - LAMBDA_INIT)


def _ssm_prep_kernel(ar_ref, ai_ref, ldt_ref, art_ref, ait_ref, brt_ref, bit_ref,
                     lbre_ref, lbim_ref, bbre_ref, bbim_ref):
    dt = jnp.exp(ldt_ref[...])

    def disc(ar, ai):
        mag = jnp.exp(ar * dt)
        return mag * jnp.cos(ai * dt), mag * jnp.sin(ai * dt)

    lb_re, lb_im = disc(ar_ref[...], ai_ref[...])
    lbre_ref[...] = lb_re
    lbim_ref[...] = lb_im
    ar, ai = art_ref[...], ait_ref[...]
    t_re, t_im = disc(ar, ai)
    den = ar * ar + ai * ai
    nr, ni = t_re - 1.0, t_im
    f_re = (nr * ar + ni * ai) / den
    f_im = (ni * ar - nr * ai) / den
    br, bi = brt_ref[...], bit_ref[...]
    bbre_ref[...] = f_re * br - f_im * bi
    bbim_ref[...] = f_re * bi + f_im * br


def _ssm_prep(a_re, a_im, log_dt, b_re, b_im):
    G, N, C = N_SSM_GROUPS, SSM_STATE, SSM_GROUP
    tile = lambda a: jnp.tile(a, (1, C))
    flat = lambda b: jnp.transpose(b, (0, 2, 1)).reshape(G, C * N)
    outs = pl.pallas_call(
        _ssm_prep_kernel,
        out_shape=[jax.ShapeDtypeStruct((G, N), F32)] * 2 + [jax.ShapeDtypeStruct((G, C * N), F32)] * 2,
        name="ssm_prep",
    )(a_re, a_im, log_dt.reshape(G, 1), tile(a_re), tile(a_im), flat(b_re), flat(b_im))
    return outs


def _in_proj_kernel(x_ref, g_ref, w_ref, cos_ref, sa_ref, sb_ref, q_ref, k_ref, v_ref, u_ref):
    tm = x_ref.shape[0]
    x = x_ref[...]
    r = lax.rsqrt(jnp.mean(x * x, axis=-1, keepdims=True) + NORM_EPS)
    hn = (x * r * g_ref[...]).astype(BF16)
    proj = jnp.dot(hn, w_ref[...], preferred_element_type=F32)
    cos, sa, sb = cos_ref[...], sa_ref[...], sb_ref[...]
    half = ROT_DIM // 2

    def rope(xh):
        return xh * cos + pltpu.roll(xh, HEAD_W - half, 1) * sa + pltpu.roll(xh, half, 1) * sb

    for h in range(N_ATT_HEADS):
        lo, hi = h * HEAD_W, (h + 1) * HEAD_W
        q_ref[:, lo:hi] = rope(proj[:, lo:hi]) * (ATT_HEAD_DIM ** -0.5)
        k_ref[pl.ds(h, tm, stride=N_ATT_HEADS), :] = rope(proj[:, D_ATT + lo:D_ATT + hi])
        v_ref[pl.ds(h, tm, stride=N_ATT_HEADS), :] = proj[:, 2 * D_ATT + lo:2 * D_ATT + hi]
    u_ref[...] = proj[:, 3 * D_ATT:]


def _rope_tables(pos):
    half = ROT_DIM // 2
    inv = ROPE_THETA ** (-jnp.arange(0, ROT_DIM, 2, dtype=F32) / ROT_DIM)
    ang = pos.astype(F32)[:, None] * inv[None, :]
    cos, sin = jnp.cos(ang), jnp.sin(ang)
    S = pos.shape[0]
    pad = jnp.zeros((S, ATT_HEAD_DIM - ROT_DIM), F32)
    zh = jnp.zeros((S, half), F32)
    comp = lambda a, b, c: jnp.concatenate([a, b, c], axis=1)
    two = lambda t: jnp.concatenate([t, t], axis=1)
    return (two(comp(cos, cos, pad + 1.0)), two(comp(-sin, zh, pad)), two(comp(zh, sin, pad)))


def _in_proj(x, g, w_bf16, tables, tm):
    B, S, D = x.shape
    ns = S // tm
    row = lambda b, s: (b * ns + s, 0)
    tab = pl.BlockSpec((tm, HEAD_W), lambda b, s: (s, 0))
    q_spec = pl.BlockSpec((tm, D_ATT), row)
    kv_spec = pl.BlockSpec((tm * N_ATT_HEADS, HEAD_W), row)
    kv_shape = jax.ShapeDtypeStruct((B * S * N_ATT_HEADS, HEAD_W), F32)
    q, k, v, u = pl.pallas_call(
        _in_proj_kernel,
        grid=(B, ns),
        in_specs=[pl.BlockSpec((None, tm, D), lambda b, s: (b, s, 0)),
                  pl.BlockSpec((1, D), lambda b, s: (0, 0)),
                  pl.BlockSpec((D, D_PROJ), lambda b, s: (0, 0)),
                  tab, tab, tab],
        out_specs=[q_spec, kv_spec, kv_spec, pl.BlockSpec((tm, D_SSM), lambda b, s: (s, b))],
        out_shape=[jax.ShapeDtypeStruct((B * S, D_ATT), F32), kv_shape, kv_shape,
                   jax.ShapeDtypeStruct((S, B * D_SSM), F32)],
        compiler_params=_params("parallel", "parallel"),
        name="in_proj",
    )(x, g.reshape(1, D), w_bf16, *tables)
    return q, k, v, u


def _attn_kernel(lq1_ref, lk1_ref, lq2_ref, lk2_ref, g_ref, q_ref, k_ref, v_ref, o_ref, *, tq):
    i = pl.program_id(2)
    lam = _lambda(lq1_ref, lk1_ref, lq2_ref, lk2_ref)
    nh = q_ref.shape[1] // HEAD_W
    feat = lax.broadcasted_iota(jnp.int32, (HEAD_W, tq), 0)
    units = [(h, c) for h in range(nh) for c in range(2)]
    hs = lambda h: slice(h * HEAD_W, (h + 1) * HEAD_W)
    q_t = []
    for h, c in units:
        qh_t = (q_ref[:, hs(h)] * LOG2_E).T
        own = (feat < ATT_HEAD_DIM) if c == 0 else (feat >= ATT_HEAD_DIM)
        q_t.append(jnp.where(own, qh_t, 0.0))

    ones_rows = jnp.ones((8, tq), F32)

    def block(j, carry, masked):
        start = pl.multiple_of(j * tq, tq)
        n = range(len(units))
        rows = lambda h: pl.ds(start * N_ATT_HEADS + h, tq, stride=N_ATT_HEADS)
        s = [jnp.dot(k_ref[rows(h), :], q_t[u], preferred_element_type=F32)
             for u, (h, c) in enumerate(units)]
        if masked:
            key = lax.broadcasted_iota(jnp.int32, (tq, tq), 0)
            qry = lax.broadcasted_iota(jnp.int32, (tq, tq), 1)
            s = [jnp.where(key <= qry, su, NEG_INF) for su in s]
        m_new = [jnp.maximum(carry[u][0], jnp.max(s[u], axis=0, keepdims=True)) for u in n]
        alpha = [jnp.exp2(carry[u][0] - m_new[u]) for u in n]
        p = [jnp.exp2(s[u] - m_new[u]) for u in n]
        pv = [jnp.dot(jnp.concatenate([v_ref[rows(h), :].T, ones_rows], axis=0), p[u],
                      preferred_element_type=F32) for u, (h, c) in enumerate(units)]
        return tuple((m_new[u], alpha[u] * carry[u][1] + pv[u][HEAD_W:HEAD_W + 1],
                      alpha[u] * carry[u][2] + pv[u][:HEAD_W]) for u in n)

    init = tuple((jnp.full((1, tq), NEG_INF, F32), jnp.zeros((1, tq), F32), jnp.zeros((HEAD_W, tq), F32))
                 for _ in units)
    carry = lax.fori_loop(0, i, lambda j, c: block(j, c, False), init)
    carry = block(i, carry, True)
    g_col = g_ref[...]
    for h in range(nh):
        (_, l1, acc1), (_, l2, acc2) = carry[2 * h], carry[2 * h + 1]
        o_t = acc1 / l1 - lam * (acc2 / l2)
        r = lax.rsqrt(jnp.mean(o_t * o_t, axis=0, keepdims=True) + SUBLN_EPS)
        o_ref[:, h * HEAD_W:(h + 1) * HEAD_W] = ((o_t * r * g_col) * (1.0 - LAMBDA_INIT)).T


def _attn(q, k, v, lam_vecs, g, B, S, tq, nh):
    nq = S // tq
    vec = pl.BlockSpec((1, ATT_HEAD_DIM), lambda b, h, i: (0, 0))
    assert nh == N_ATT_HEADS, "k / v blocks hold all heads of a token"
    kv = pl.BlockSpec((S * N_ATT_HEADS, HEAD_W), lambda b, h, i: (b, 0))
    qo = pl.BlockSpec((tq, nh * HEAD_W), lambda b, h, i: (b * nq + i, h))
    return pl.pallas_call(
        functools.partial(_attn_kernel, tq=tq),
        grid=(B, N_ATT_HEADS // nh, nq),
        in_specs=[vec, vec, vec, vec, pl.BlockSpec((HEAD_W, 1), lambda b, h, i: (0, 0)), qo, kv, kv],
        out_specs=qo,
        out_shape=jax.ShapeDtypeStruct((B * S, D_ATT), F32),
        compiler_params=_params("parallel", "parallel", "parallel"),
        name="attn",
    )(*lam_vecs, g.reshape(HEAD_W, 1), q, k, v)


def _decode_kernel(pt_ref, lq1_ref, lk1_ref, lq2_ref, lk2_ref, g_ref, q_ref, kn_ref, vn_ref, *rest):
    del pt_ref
    P = PAGES_PER_STEP
    k_refs, v_refs = rest[:P], rest[P:2 * P]
    o_ref, m_scr, l_scr, acc_scr = rest[2 * P:]
    j = pl.program_id(1)
    H = N_ATT_HEADS

    @pl.when(j == 0)
    def _():
        m_scr[...] = jnp.full(m_scr.shape, NEG_INF, F32)
        l_scr[...] = jnp.zeros(l_scr.shape, F32)
        acc_scr[...] = jnp.zeros(acc_scr.shape, F32)

    q = q_ref[...]
    lane = lax.broadcasted_iota(jnp.int32, q.shape, 1)
    q8 = jnp.concatenate([jnp.where(lane < ATT_HEAD_DIM, q, 0.0), jnp.where(lane >= ATT_HEAD_DIM, q, 0.0)], axis=0)
    rows = k_refs[0].shape[0]
    r = lax.broadcasted_iota(jnp.int32, (2 * H, rows), 0)
    col = lax.broadcasted_iota(jnp.int32, (2 * H, rows), 1)
    own = (col % H) == (r % H)
    s = jnp.concatenate(
        [jnp.where(own, lax.dot_general(q8, kr[...], (((1,), (1,)), ((), ())), preferred_element_type=F32), NEG_INF)
         for kr in k_refs], axis=1)
    m = m_scr[...]
    m_new = jnp.maximum(m, jnp.max(s, axis=-1, keepdims=True))
    alpha = jnp.exp(m - m_new)
    p = jnp.exp(s - m_new)
    pv = jnp.dot(p[:, :rows], v_refs[0][...], preferred_element_type=F32)
    for i in range(1, P):
        pv = pv + jnp.dot(p[:, i * rows:(i + 1) * rows], v_refs[i][...], preferred_element_type=F32)
    m_scr[...] = m_new
    l_scr[...] = alpha * l_scr[...] + jnp.sum(p, axis=-1, keepdims=True)
    acc_scr[...] = alpha * acc_scr[...] + pv

    @pl.when(j == pl.num_programs(1) - 1)
    def _():
        lam = _lambda(lq1_ref, lk1_ref, lq2_ref, lk2_ref)
        kn = jnp.concatenate([kn_ref[...], kn_ref[...]], axis=0)
        vn = jnp.concatenate([vn_ref[...], vn_ref[...]], axis=0)
        s_new = jnp.sum(q8 * kn, axis=-1, keepdims=True)
        m_old = m_scr[...]
        m_fin = jnp.maximum(m_old, s_new)
        a = jnp.exp(m_old - m_fin)
        p_new = jnp.exp(s_new - m_fin)
        l = a * l_scr[...] + p_new
        acc = (a * acc_scr[...] + p_new * vn) / l
        o_ref[...] = _sub_ln(acc[:H] - lam * acc[H:], g_ref[...])


def _decode_attn(q, k_new, v_new, cache_k, cache_v, page_table, lam_vecs, g):
    Bs, n_pages = page_table.shape
    n_pool, page = cache_k.shape[0], cache_k.shape[1]
    P, H = PAGES_PER_STEP, N_ATT_HEADS
    ck = cache_k.reshape(n_pool, page * H, HEAD_W)
    cv = cache_v.reshape(n_pool, page * H, HEAD_W)
    vec = pl.BlockSpec((1, ATT_HEAD_DIM), lambda b, j, pt: (0, 0))
    tok = pl.BlockSpec((None, H, HEAD_W), lambda b, j, pt: (b, 0, 0))
    page_spec = lambda i: pl.BlockSpec((None, page * H, HEAD_W), lambda b, j, pt: (pt[b, j * P + i], 0, 0))
    grid_spec = pltpu.PrefetchScalarGridSpec(
        num_scalar_prefetch=1,
        grid=(Bs, n_pages // P),
        in_specs=[vec, vec, vec, vec, pl.BlockSpec((1, HEAD_W), lambda b, j, pt: (0, 0)), tok, tok, tok]
                 + [page_spec(i) for i in range(P)] * 2,
        out_specs=tok,
        scratch_shapes=[pltpu.VMEM((2 * H, 1), F32), pltpu.VMEM((2 * H, 1), F32), pltpu.VMEM((2 * H, HEAD_W), F32)],
    )
    tok3 = lambda a: a.reshape(Bs, H, HEAD_W)
    o = pl.pallas_call(
        _decode_kernel,
        grid_spec=grid_spec,
        out_shape=jax.ShapeDtypeStruct((Bs, H, HEAD_W), F32),
        compiler_params=_params("parallel", "arbitrary"),
        name="decode_attn",
    )(page_table, *lam_vecs, g, tok3(q), tok3(k_new), tok3(v_new), *([ck] * P), *([cv] * P))
    return o.reshape(Bs, D_ATT)


SCAN_COLS = 1024
B_GROUPS = 16
C_GROUPS = 8


def _ssm_kernel(u_ref, bb_ref, lre_ref, lim_ref, c_ref, d_ref, wglu_ref, h0re_ref, h0im_ref,
                z_ref, hre_ref, him_ref, bu_scr, hs_scr, *, tc, bk):
    t0 = pl.program_id(0)

    @pl.when(t0 == 0)
    def _():
        hre_ref[...] = h0re_ref[...]
        him_ref[...] = h0im_ref[...]

    rows = tc * bk
    u = u_ref[...].reshape(rows, D_SSM)
    ub = u.astype(BF16)
    part_w = B_GROUPS * SSM_STATE
    half_w = 2 * part_w
    for half in range(N_SSM_GROUPS // B_GROUPS):
        u_half = ub[:, half * B_GROUPS * SSM_GROUP:(half + 1) * B_GROUPS * SSM_GROUP]
        bu = jnp.dot(u_half, bb_ref[half], preferred_element_type=F32)
        bu_scr[:, :, half * half_w:(half + 1) * half_w] = bu.reshape(tc, bk, half_w)

    for half in range(N_SSM_GROUPS // B_GROUPS):
        for cg in range(part_w // SCAN_COLS):
            st = slice(half * part_w + cg * SCAN_COLS, half * part_w + (cg + 1) * SCAN_COLS)
            re = slice(half * half_w + cg * SCAN_COLS, half * half_w + (cg + 1) * SCAN_COLS)
            im = slice(re.start + part_w, re.stop + part_w)
            lr, li = lre_ref[:, st], lim_ref[:, st]

            def step(t, carry, lr=lr, li=li, re=re, im=im):
                hr, hi = carry
                nhr = lr * hr - li * hi + bu_scr[t, :, re]
                nhi = lr * hi + li * hr + bu_scr[t, :, im]
                hs_scr[t, :, re] = nhr
                hs_scr[t, :, im] = nhi
                return nhr, nhi

            hr, hi = lax.fori_loop(0, tc, step, (hre_ref[:, st], him_ref[:, st]))
            hre_ref[:, st] = hr
            him_ref[:, st] = hi

    slab_w = C_GROUPS * SSM_STATE
    y_slabs = []
    for s in range(N_SSM_GROUPS // C_GROUPS):
        re0 = (s * C_GROUPS // B_GROUPS) * half_w + (s * C_GROUPS % B_GROUPS) * SSM_STATE
        h_re = hs_scr[:, :, re0:re0 + slab_w].reshape(rows, slab_w).astype(BF16)
        h_im = hs_scr[:, :, re0 + part_w:re0 + part_w + slab_w].reshape(rows, slab_w).astype(BF16)
        y_slabs.append(jnp.dot(h_re, c_ref[s, 0], preferred_element_type=F32)
                       + jnp.dot(h_im, c_ref[s, 1], preferred_element_type=F32))
    y = jnp.concatenate(y_slabs, axis=1) + d_ref[...] * u
    y = jax.nn.gelu(y)
    gate = jnp.dot(y.astype(BF16), wglu_ref[...], preferred_element_type=F32)
    z_ref[...] = (y * jax.nn.sigmoid(gate)).reshape(tc, bk, D_SSM)


def _ssm(u3, bb, lb_re, lb_im, c_full, d, w_glu, h0_re, h0_im, tc):
    S, bk, _ = u3.shape
    const = lambda shape: pl.BlockSpec(shape, lambda t: (0,) * len(shape))
    seq = pl.BlockSpec((tc, bk, D_SSM), lambda t: (t, 0, 0))
    rep = lambda a: jnp.broadcast_to(a.reshape(1, N_STATE), (bk, N_STATE))
    z, h_re, h_im = pl.pallas_call(
        functools.partial(_ssm_kernel, tc=tc, bk=bk),
        grid=(S // tc,),
        in_specs=[seq, const(bb.shape), const((bk, N_STATE)), const((bk, N_STATE)),
                  const(c_full.shape), const((1, D_SSM)), const((D_SSM, D_SSM)),
                  const((bk, N_STATE)), const((bk, N_STATE))],
        out_specs=[seq, const((bk, N_STATE)), const((bk, N_STATE))],
        out_shape=[jax.ShapeDtypeStruct((S, bk, D_SSM), F32), jax.ShapeDtypeStruct((bk, N_STATE), F32),
                   jax.ShapeDtypeStruct((bk, N_STATE), F32)],
        scratch_shapes=[pltpu.VMEM((tc, bk, 2 * N_STATE), F32), pltpu.VMEM((tc, bk, 2 * N_STATE), F32)],
        compiler_params=_params("arbitrary"),
        name="ssm",
    )(u3, bb, rep(lb_re), rep(lb_im), c_full, d.reshape(1, D_SSM), w_glu, h0_re, h0_im)
    return z, h_re, h_im


def _ffn_kernel(*refs, tm, decode):
    if decode:
        (x_ref, o_ref, z_ref, wo_ref, gn_ref, wu_ref, cw_ref, cb_ref, wd_ref, fn_ref, p2_ref, p1_ref,
         y_ref, a_out_ref, hid_scr) = refs
    else:
        (x_ref, o_ref, z_ref, wo_ref, gn_ref, wu_ref, cw_ref, cb_ref, wd_ref, fn_ref,
         y_ref, cs_ref, hid_scr, carry_scr) = refs

        @pl.when(pl.program_id(1) == 0)
        def _():
            carry_scr[...] = jnp.zeros(carry_scr.shape, F32)

    mix = (jnp.dot(o_ref[...].astype(BF16), wo_ref[:D_ATT, :], preferred_element_type=F32)
           + jnp.dot(z_ref[...].astype(BF16), wo_ref[D_ATT:, :], preferred_element_type=F32))
    x1 = x_ref[...] + mix
    r = lax.rsqrt(jnp.mean(x1 * x1, axis=-1, keepdims=True) + NORM_EPS)
    hn = (x1 * r * gn_ref[...]).astype(BF16)
    for j in range(N_FF_CHUNKS):
        cols = slice(j * FF_CHUNK, (j + 1) * FF_CHUNK)
        gcols = slice(D_FF + j * FF_CHUNK, D_FF + (j + 1) * FF_CHUNK)
        a = jnp.dot(hn, wu_ref[:, cols], preferred_element_type=F32)
        gate = jnp.dot(hn, wu_ref[:, gcols], preferred_element_type=F32)
        if decode:
            am2, am1 = p2_ref[:, cols], p1_ref[:, cols]
            a_out_ref[:, cols] = a
        else:
            prev = carry_scr[:, cols]
            p2 = jnp.broadcast_to(prev[CARRY_ROWS - 2:CARRY_ROWS - 1], a.shape)
            p1 = jnp.broadcast_to(prev[CARRY_ROWS - 1:], a.shape)
            row = lax.broadcasted_iota(jnp.int32, a.shape, 0)
            am1 = jnp.where(row < 1, p1, pltpu.roll(a, 1, 0))
            am2 = jnp.where(row < 1, p2, jnp.where(row < 2, p1, pltpu.roll(a, 2, 0)))
            carry_scr[:, cols] = a[tm - CARRY_ROWS:]
        cw = cw_ref[:, cols]
        c = cb_ref[:, cols] + (cw[0:1] * am2 + cw[1:2] * am1 + cw[2:3] * a)
        hid_scr[:, cols] = (jax.nn.silu(c) * gate).astype(BF16)
    acc = x1 + jnp.dot(hid_scr[...], wd_ref[...], preferred_element_type=F32)
    r2 = lax.rsqrt(jnp.mean(acc * acc, axis=-1, keepdims=True) + NORM_EPS)
    y_ref[...] = acc * r2 * fn_ref[...]
    if not decode:
        cs_ref[...] = carry_scr[...]


def _ffn(x, o, z, weights, tm, prev=None):
    B, S, D = x.shape
    ns = S // tm
    decode = prev is not None
    wo, gn, wu, cw, cb, wd, fn = weights
    const = lambda shape: pl.BlockSpec(shape, lambda b, s: (0,) * len(shape), pipeline_mode=pl.Buffered(1))
    xs = pl.BlockSpec((None, tm, D), lambda b, s: (b, s, 0))
    in_specs = [xs, pl.BlockSpec((tm, D_ATT), lambda b, s: (b * ns + s, 0)),
                pl.BlockSpec((tm, D_SSM), lambda b, s: (s, b)),
                const((D, D)), const((1, D)), const((D, 2 * D_FF)), const((CONV_W, D_FF)), const((1, D_FF)),
                const((D_FF, D)), const((1, D))]
    args = [x, o, z, wo, gn, wu, cw, cb, wd, fn]
    scratch = [pltpu.VMEM((tm, D_FF), BF16)]
    if decode:
        rows = pl.BlockSpec((tm, D_FF), lambda b, s: (0, 0))
        in_specs += [rows, rows]
        args += list(prev)
        out_specs = [xs, rows]
        out_shape = [jax.ShapeDtypeStruct((B, S, D), F32), jax.ShapeDtypeStruct((tm, D_FF), F32)]
    else:
        out_specs = [xs, pl.BlockSpec((None, CARRY_ROWS, D_FF), lambda b, s: (b, 0, 0))]
        out_shape = [jax.ShapeDtypeStruct((B, S, D), F32), jax.ShapeDtypeStruct((B, CARRY_ROWS, D_FF), F32)]
        scratch.append(pltpu.VMEM((CARRY_ROWS, D_FF), F32))
    y, extra = pl.pallas_call(
        functools.partial(_ffn_kernel, tm=tm, decode=decode),
        grid=(B, ns),
        in_specs=in_specs, out_specs=out_specs, out_shape=out_shape, scratch_shapes=scratch,
        compiler_params=_params("parallel", "arbitrary"),
        name="ffn_decode" if decode else "ffn",
    )(*args)
    return y, extra


def kernel(x_prompt, x_sample, cache_k, cache_v, state_ssm_re, state_ssm_im, state_conv, page_table, norm_mix, w_in, lambda_q1, lambda_k1, lambda_q2, lambda_k2, subln_g, ssm_a_re, ssm_a_im, ssm_log_dt, ssm_b_re, ssm_b_im, ssm_c_re, ssm_c_im, ssm_d, w_glu, w_o, norm_ffn, w_up, conv_w, conv_b, w_down, final_norm):
    assert norm_mix.shape[0] == 1, "single layer"
    Bp, Sp, D = x_prompt.shape
    Bs = x_sample.shape[0]
    G, N, C = N_SSM_GROUPS, SSM_STATE, SSM_GROUP
    past_len = page_table.shape[1] * cache_k.shape[2]

    w_in_b = w_in[0].astype(BF16)
    wo_b = w_o[0].astype(BF16)
    wglu_b = w_glu[0].astype(BF16)
    ffn_w = (wo_b, norm_ffn[0].reshape(1, D), w_up[0].astype(BF16), conv_w[0], conv_b[0].reshape(1, D_FF),
             w_down[0].astype(BF16), final_norm.reshape(1, D))
    lam_vecs = [v.reshape(1, ATT_HEAD_DIM) for v in (lambda_q1[0], lambda_k1[0], lambda_q2[0], lambda_k2[0])]
    g_sub = subln_g[0].reshape(1, HEAD_W)

    lb_re, lb_im, bbt_re, bbt_im = _ssm_prep(ssm_a_re[0], ssm_a_im[0], ssm_log_dt[0], ssm_b_re[0], ssm_b_im[0])
    eye_b = jnp.eye(B_GROUPS, dtype=F32)[None, :, None, :, None]
    place_b = lambda bbt: (bbt.reshape(G // B_GROUPS, B_GROUPS, C, 1, N) * eye_b).reshape(
        G // B_GROUPS, B_GROUPS * C, B_GROUPS * N)
    bb_blk = jnp.concatenate([place_b(bbt_re), place_b(bbt_im)], axis=2).astype(BF16)
    eye_c = jnp.eye(C_GROUPS, dtype=F32)[None, :, None, :, None]
    place_c = lambda c: (jnp.transpose(c, (0, 2, 1)).reshape(G // C_GROUPS, C_GROUPS, N, 1, C) * eye_c).reshape(
        G // C_GROUPS, C_GROUPS * N, C_GROUPS * C)
    c_blk = jnp.stack([place_c(ssm_c_re[0]), -place_c(ssm_c_im[0])], axis=1).astype(BF16)

    tabs_p = _rope_tables(jnp.arange(Sp, dtype=jnp.int32))
    q, k, v, u = _in_proj(x_prompt, norm_mix[0], w_in_b, tabs_p, tm=512)
    o = _attn(q, k, v, lam_vecs, g_sub, Bp, Sp, tq=256, nh=4)
    zeros_h = jnp.zeros((Bp, N_STATE), F32)
    z, hre_p, him_p = _ssm(u.reshape(Sp, Bp, D_SSM), bb_blk, lb_re, lb_im, c_blk, ssm_d[0], wglu_b,
                           zeros_h, zeros_h, tc=64)
    y_prompt, cs_p = _ffn(x_prompt, o, z.reshape(Sp, Bp * D_SSM), ffn_w, tm=512)
    conv_prompt = cs_p[:, CARRY_ROWS - (CONV_W - 1):]

    tabs_s = _rope_tables(jnp.full((Bs,), past_len, jnp.int32))
    qs, ks, vs, us = _in_proj(x_sample.reshape(1, Bs, D), norm_mix[0], w_in_b, tabs_s, tm=Bs)
    os_ = _decode_attn(qs, ks, vs, cache_k[0], cache_v[0], page_table, lam_vecs, g_sub)
    zs, hre_s, him_s = _ssm(us.reshape(1, Bs, D_SSM), bb_blk, lb_re, lb_im, c_blk, ssm_d[0], wglu_b,
                            state_ssm_re[0].reshape(Bs, N_STATE), state_ssm_im[0].reshape(Bs, N_STATE), tc=1)
    sc = state_conv[0]
    y_s, a_s = _ffn(x_sample.reshape(1, Bs, D), os_, zs.reshape(Bs, D_SSM), ffn_w, tm=Bs, prev=(sc[:, 0], sc[:, 1]))
    conv_sample = jnp.stack([sc[:, 1], a_s], axis=1)

    st = lambda h, b: h.reshape(1, b, G, N)
    return (y_prompt, y_s.reshape(Bs, 1, D),
            k.reshape(1, Bp, Sp, N_ATT_HEADS, HEAD_W), v.reshape(1, Bp, Sp, N_ATT_HEADS, HEAD_W),
            st(hre_p, Bp), st(him_p, Bp), conv_prompt[None],
            ks.reshape(1, Bs, 1, N_ATT_HEADS, HEAD_W), vs.reshape(1, Bs, 1, N_ATT_HEADS, HEAD_W),
            st(hre_s, Bs), st(him_s, Bs), conv_sample[None])
```

```python
import functools
import math

import jax
import jax.numpy as jnp
import numpy as np
from jax import lax
from jax.experimental import pallas as pl
from jax.experimental.pallas import tpu as pltpu

F32 = jnp.float32
BF16 = jnp.bfloat16

D_MODEL = 1024
D_ATT = 512
D_SSM = 512
ATT_HEAD_DIM = 64
N_ATT_HEADS = 4
HEAD_W = 2 * ATT_HEAD_DIM
ROT_DIM = 16
ROPE_THETA = 500000.0
SSM_GROUP = 16
N_SSM_GROUPS = 32
SSM_STATE = 64
N_STATE = N_SSM_GROUPS * SSM_STATE
D_FF = 2816
CONV_W = 3
NORM_EPS = 1e-6
SUBLN_EPS = 1e-5
NEG_INF = -1e30
D_PROJ = 3 * D_ATT + D_SSM
LAMBDA_INIT = 0.8 - 0.6 * math.exp(-0.3 * 0)
LOG2_E = math.log2(math.e)

V7X_VMEM_LIMIT = 56 * 1024 * 1024
FF_CHUNK = 256
N_FF_CHUNKS = D_FF // FF_CHUNK
CARRY_ROWS = 8


def _params(*sem):
    return pltpu.CompilerParams(dimension_semantics=sem, vmem_limit_bytes=V7X_VMEM_LIMIT)


def _lambda(lq1_ref, lk1_ref, lq2_ref, lk2_ref):
    s1 = jnp.sum(lq1_ref[...] * lk1_ref[...], axis=-1, keepdims=True)
    s2 = jnp.sum(lq2_ref[...] * lk2_ref[...], axis=-1, keepdims=True)
    return jnp.exp(s1) - jnp.exp(s2) + LAMBDA_INIT


def _sub_ln(o, g):
    r = lax.rsqrt(jnp.mean(o * o, axis=-1, keepdims=True) + SUBLN_EPS)
    return (o * r * g) * (1.0 - LAMBDA_INIT)


def _ssm_prep_kernel(ar_ref, ai_ref, ldt_ref, art_ref, ait_ref, brt_ref, bit_ref,
                     lbre_ref, lbim_ref, bbre_ref, bbim_ref):
    dt = jnp.exp(ldt_ref[...])

    def disc(ar, ai):
        mag = jnp.exp(ar * dt)
        return mag * jnp.cos(ai * dt), mag * jnp.sin(ai * dt)

    lb_re, lb_im = disc(ar_ref[...], ai_ref[...])
    lbre_ref[...] = lb_re
    lbim_ref[...] = lb_im
    ar, ai = art_ref[...], ait_ref[...]
    t_re, t_im = disc(ar, ai)
    den = ar * ar + ai * ai
    nr, ni = t_re - 1.0, t_im
    f_re = (nr * ar + ni * ai) / den
    f_im = (ni * ar - nr * ai) / den
    br, bi = brt_ref[...], bit_ref[...]
    bbre_ref[...] = f_re * br - f_im * bi
    bbim_ref[...] = f_re * bi + f_im * br


def _ssm_prep(a_re, a_im, log_dt, b_re, b_im):
    G, N, C = N_SSM_GROUPS, SSM_STATE, SSM_GROUP
    tile = lambda a: jnp.tile(a, (1, C))
    flat = lambda b: jnp.transpose(b, (0, 2, 1)).reshape(G, C * N)
    outs = pl.pallas_call(
        _ssm_prep_kernel,
        out_shape=[jax.ShapeDtypeStruct((G, N), F32)] * 2 + [jax.ShapeDtypeStruct((G, C * N), F32)] * 2,
        name="ssm_prep",
    )(a_re, a_im, log_dt.reshape(G, 1), tile(a_re), tile(a_im), flat(b_re), flat(b_im))
    return outs


def _in_proj_kernel(x_ref, g_ref, w_ref, cos_ref, sa_ref, sb_ref, q_ref, k_ref, v_ref, u_ref):
    tm = x_ref.shape[0]
    x = x_ref[...]
    r = lax.rsqrt(jnp.mean(x * x, axis=-1, keepdims=True) + NORM_EPS)
    hn = (x * r * g_ref[...]).astype(BF16)
    proj = jnp.dot(hn, w_ref[...], preferred_element_type=F32)
    cos, sa, sb = cos_ref[...], sa_ref[...], sb_ref[...]
    half = ROT_DIM // 2

    def rope(xh):
        return xh * cos + pltpu.roll(xh, HEAD_W - half, 1) * sa + pltpu.roll(xh, half, 1) * sb

    for h in range(N_ATT_HEADS):
        lo, hi = h * HEAD_W, (h + 1) * HEAD_W
        q_ref[:, lo:hi] = rope(proj[:, lo:hi]) * (ATT_HEAD_DIM ** -0.5)
        k_ref[pl.ds(h, tm, stride=N_ATT_HEADS), :] = rope(proj[:, D_ATT + lo:D_ATT + hi])
        v_ref[pl.ds(h, tm, stride=N_ATT_HEADS), :] = proj[:, 2 * D_ATT + lo:2 * D_ATT + hi]
    u_ref[...] = proj[:, 3 * D_ATT:]


def _rope_tables(pos):
    half = ROT_DIM // 2
    inv = ROPE_THETA ** (-np.arange(0, ROT_DIM, 2, dtype=np.float64) / ROT_DIM)
    ang = np.asarray(pos, np.float64)[:, None] * inv[None, :]
    cos, sin = np.cos(ang), np.sin(ang)
    S = ang.shape[0]
    pad = np.zeros((S, ATT_HEAD_DIM - ROT_DIM))
    zh = np.zeros((S, half))
    comp = lambda a, b, c: np.concatenate([a, b, c], axis=1)
    two = lambda t: jnp.asarray(np.concatenate([t, t], axis=1), F32)
    return (two(comp(cos, cos, pad + 1.0)), two(comp(-sin, zh, pad)), two(comp(zh, sin, pad)))


def _in_proj(x, g, w_bf16, tables, tm):
    B, S, D = x.shape
    ns = S // tm
    row = lambda b, s: (b * ns + s, 0)
    tab = pl.BlockSpec((tm, HEAD_W), lambda b, s: (s, 0))
    q_spec = pl.BlockSpec((tm, D_ATT), row)
    kv_spec = pl.BlockSpec((tm * N_ATT_HEADS, HEAD_W), row)
    kv_shape = jax.ShapeDtypeStruct((B * S * N_ATT_HEADS, HEAD_W), F32)
    q, k, v, u = pl.pallas_call(
        _in_proj_kernel,
        grid=(B, ns),
        in_specs=[pl.BlockSpec((None, tm, D), lambda b, s: (b, s, 0)),
                  pl.BlockSpec((1, D), lambda b, s: (0, 0)),
                  pl.BlockSpec((D, D_PROJ), lambda b, s: (0, 0)),
                  tab, tab, tab],
        out_specs=[q_spec, kv_spec, kv_spec, pl.BlockSpec((tm, D_SSM), lambda b, s: (s, b))],
        out_shape=[jax.ShapeDtypeStruct((B * S, D_ATT), F32), kv_shape, kv_shape,
                   jax.ShapeDtypeStruct((S, B * D_SSM), F32)],
        compiler_params=_params("parallel", "parallel"),
        name="in_proj",
    )(x, g.reshape(1, D), w_bf16, *tables)
    return q, k, v, u


def _prompt_tile(i, lam, g_col, q_ref, k_ref, v_ref, o_ref, tq):
    nh = N_ATT_HEADS
    feat = lax.broadcasted_iota(jnp.int32, (HEAD_W, tq), 0)
    units = [(h, c) for h in range(nh) for c in range(2)]
    hs = lambda h: slice(h * HEAD_W, (h + 1) * HEAD_W)
    q_t = []
    for h, c in units:
        qh_t = (q_ref[:, hs(h)] * LOG2_E).T
        own = (feat < ATT_HEAD_DIM) if c == 0 else (feat >= ATT_HEAD_DIM)
        q_t.append(jnp.where(own, qh_t, 0.0))

    ones_rows = jnp.ones((8, tq), F32)

    def block(j, carry, masked):
        start = pl.multiple_of(j * tq, tq)
        n = range(len(units))
        rows = lambda h: pl.ds(start * N_ATT_HEADS + h, tq, stride=N_ATT_HEADS)
        s = [jnp.dot(k_ref[rows(h), :], q_t[u], preferred_element_type=F32)
             for u, (h, c) in enumerate(units)]
        if masked:
            key = lax.broadcasted_iota(jnp.int32, (tq, tq), 0)
            qry = lax.broadcasted_iota(jnp.int32, (tq, tq), 1)
            s = [jnp.where(key <= qry, su, NEG_INF) for su in s]
        m_new = [jnp.maximum(carry[u][0], jnp.max(s[u], axis=0, keepdims=True)) for u in n]
        alpha = [jnp.exp2(carry[u][0] - m_new[u]) for u in n]
        p = [jnp.exp2(s[u] - m_new[u]) for u in n]
        pv = [jnp.dot(jnp.concatenate([v_ref[rows(h), :].T, ones_rows], axis=0), p[u],
                      preferred_element_type=F32) for u, (h, c) in enumerate(units)]
        return tuple((m_new[u], alpha[u] * carry[u][1] + pv[u][HEAD_W:HEAD_W + 1],
                      alpha[u] * carry[u][2] + pv[u][:HEAD_W]) for u in n)

    init = tuple((jnp.full((1, tq), NEG_INF, F32), jnp.zeros((1, tq), F32), jnp.zeros((HEAD_W, tq), F32))
                 for _ in units)
    carry = lax.fori_loop(0, i, lambda j, c: block(j, c, False), init)
    carry = block(i, carry, True)
    for h in range(nh):
        (_, l1, acc1), (_, l2, acc2) = carry[2 * h], carry[2 * h + 1]
        o_t = acc1 / l1 - lam * (acc2 / l2)
        r = lax.rsqrt(jnp.mean(o_t * o_t, axis=0, keepdims=True) + SUBLN_EPS)
        o_ref[:, h * HEAD_W:(h + 1) * HEAD_W] = ((o_t * r * g_col) * (1.0 - LAMBDA_INIT)).T


def _decode_step(j, n_steps, lam, g_row, q_ref, kn_ref, vn_ref, k_refs, v_refs, o_ref, m_scr, l_scr, acc_scr):
    H = N_ATT_HEADS

    @pl.when(j == 0)
    def _():
        m_scr[...] = jnp.full(m_scr.shape, NEG_INF, F32)
        l_scr[...] = jnp.zeros(l_scr.shape, F32)
        acc_scr[...] = jnp.zeros(acc_scr.shape, F32)

    q = q_ref[...]
    lane = lax.broadcasted_iota(jnp.int32, q.shape, 1)
    q8 = jnp.concatenate([jnp.where(lane < ATT_HEAD_DIM, q, 0.0), jnp.where(lane >= ATT_HEAD_DIM, q, 0.0)], axis=0)
    rows = k_refs[0].shape[0]
    r = lax.broadcasted_iota(jnp.int32, (2 * H, rows), 0)
    col = lax.broadcasted_iota(jnp.int32, (2 * H, rows), 1)
    own = (col % H) == (r % H)
    s = jnp.concatenate(
        [jnp.where(own, lax.dot_general(q8, kr[...], (((1,), (1,)), ((), ())), preferred_element_type=F32), NEG_INF)
         for kr in k_refs], axis=1)
    m = m_scr[...]
    m_new = jnp.maximum(m, jnp.max(s, axis=-1, keepdims=True))
    alpha = jnp.exp(m - m_new)
    p = jnp.exp(s - m_new)
    pv = jnp.dot(p[:, :rows], v_refs[0][...], preferred_element_type=F32)
    for i in range(1, len(v_refs)):
        pv = pv + jnp.dot(p[:, i * rows:(i + 1) * rows], v_refs[i][...], preferred_element_type=F32)
    m_scr[...] = m_new
    l_scr[...] = alpha * l_scr[...] + jnp.sum(p, axis=-1, keepdims=True)
    acc_scr[...] = alpha * acc_scr[...] + pv

    @pl.when(j == n_steps - 1)
    def _():
        kn = jnp.concatenate([kn_ref[...], kn_ref[...]], axis=0)
        vn = jnp.concatenate([vn_ref[...], vn_ref[...]], axis=0)
        s_new = jnp.sum(q8 * kn, axis=-1, keepdims=True)
        m_old = m_scr[...]
        m_fin = jnp.maximum(m_old, s_new)
        a = jnp.exp(m_old - m_fin)
        p_new = jnp.exp(s_new - m_fin)
        l = a * l_scr[...] + p_new
        acc = (a * acc_scr[...] + p_new * vn) / l
        o_ref[...] = _sub_ln(acc[:H] - lam * acc[H:], g_row)


def _attn_kernel(pt_ref, lq1_ref, lk1_ref, lq2_ref, lk2_ref, gcol_ref, grow_ref, q_ref, k_ref, v_ref,
                 qs_ref, kn_ref, vn_ref, *rest, tq, steps_per_seq, pages):
    del pt_ref
    k_pages, v_pages = rest[:pages], rest[pages:2 * pages]
    o_ref, os_ref, m_scr, l_scr, acc_scr = rest[2 * pages:]
    i = pl.program_id(1)
    step = pl.program_id(0) * pl.num_programs(1) + i
    lam = _lambda(lq1_ref, lk1_ref, lq2_ref, lk2_ref)
    _prompt_tile(i, lam, gcol_ref[...], q_ref, k_ref, v_ref, o_ref, tq)
    _decode_step(step % steps_per_seq, steps_per_seq, lam, grow_ref[...], qs_ref, kn_ref, vn_ref,
                 k_pages, v_pages, os_ref, m_scr, l_scr, acc_scr)


def _attn(q, k, v, qs, ks, vs, cache_k, cache_v, page_table, lam_vecs, g, B, S, tq):
    H = N_ATT_HEADS
    nq = S // tq
    Bs, n_pages = page_table.shape
    n_pool, page = cache_k.shape[0], cache_k.shape[1]
    steps_per_seq = (B * nq) // Bs
    assert steps_per_seq * Bs == B * nq and n_pages % steps_per_seq == 0
    P = n_pages // steps_per_seq
    ck = cache_k.reshape(n_pool, page * H, HEAD_W)
    cv = cache_v.reshape(n_pool, page * H, HEAD_W)
    seq = lambda b, i: (b * nq + i) // steps_per_seq
    vec = pl.BlockSpec((1, ATT_HEAD_DIM), lambda b, i, pt: (0, 0))
    kv = pl.BlockSpec((S * H, HEAD_W), lambda b, i, pt: (b, 0))
    qo = pl.BlockSpec((tq, D_ATT), lambda b, i, pt: (b * nq + i, 0))
    tok = pl.BlockSpec((None, H, HEAD_W), lambda b, i, pt: (seq(b, i), 0, 0))
    page_spec = lambda p: pl.BlockSpec(
        (None, page * H, HEAD_W), lambda b, i, pt: (pt[seq(b, i), ((b * nq + i) % steps_per_seq) * P + p], 0, 0))
    grid_spec = pltpu.PrefetchScalarGridSpec(
        num_scalar_prefetch=1,
        grid=(B, nq),
        in_specs=[vec, vec, vec, vec, pl.BlockSpec((HEAD_W, 1), lambda b, i, pt: (0, 0)),
                  pl.BlockSpec((1, HEAD_W), lambda b, i, pt: (0, 0)), qo, kv, kv, tok, tok, tok]
                 + [page_spec(p) for p in range(P)] * 2,
        out_specs=[qo, tok],
        scratch_shapes=[pltpu.VMEM((2 * H, 1), F32), pltpu.VMEM((2 * H, 1), F32), pltpu.VMEM((2 * H, HEAD_W), F32)],
    )
    tok3 = lambda a: a.reshape(Bs, H, HEAD_W)
    o, o_s = pl.pallas_call(
        functools.partial(_attn_kernel, tq=tq, steps_per_seq=steps_per_seq, pages=P),
        grid_spec=grid_spec,
        out_shape=[jax.ShapeDtypeStruct((B * S, D_ATT), F32), jax.ShapeDtypeStruct((Bs, H, HEAD_W), F32)],
        compiler_params=_params("arbitrary", "arbitrary"),
        name="attn",
    )(page_table, *lam_vecs, g.reshape(HEAD_W, 1), g.reshape(1, HEAD_W), q, k, v, tok3(qs), tok3(ks), tok3(vs),
      *([ck] * P), *([cv] * P))
    return o, o_s.reshape(Bs, D_ATT)


SCAN_COLS = 1024
B_GROUPS = 16
C_GROUPS = 8


def _ssm_kernel(u_ref, bb_ref, lre_ref, lim_ref, c_ref, d_ref, wglu_ref, h0re_ref, h0im_ref,
                z_ref, hre_ref, him_ref, bu_scr, hs_scr, *, tc, bk):
    t0 = pl.program_id(0)

    @pl.when(t0 == 0)
    def _():
        hre_ref[...] = h0re_ref[...]
        him_ref[...] = h0im_ref[...]

    rows = tc * bk
    u = u_ref[...].reshape(rows, D_SSM)
    ub = u.astype(BF16)
    part_w = B_GROUPS * SSM_STATE
    half_w = 2 * part_w
    for half in range(N_SSM_GROUPS // B_GROUPS):
        u_half = ub[:, half * B_GROUPS * SSM_GROUP:(half + 1) * B_GROUPS * SSM_GROUP]
        bu = jnp.dot(u_half, bb_ref[half], preferred_element_type=F32)
        bu_scr[:, :, half * half_w:(half + 1) * half_w] = bu.reshape(tc, bk, half_w)

    for half in range(N_SSM_GROUPS // B_GROUPS):
        for cg in range(part_w // SCAN_COLS):
            st = slice(half * part_w + cg * SCAN_COLS, half * part_w + (cg + 1) * SCAN_COLS)
            re = slice(half * half_w + cg * SCAN_COLS, half * half_w + (cg + 1) * SCAN_COLS)
            im = slice(re.start + part_w, re.stop + part_w)
            lr, li = lre_ref[:, st], lim_ref[:, st]

            def step(t, carry, lr=lr, li=li, re=re, im=im):
                hr, hi = carry
                nhr = lr * hr - li * hi + bu_scr[t, :, re]
                nhi = lr * hi + li * hr + bu_scr[t, :, im]
                hs_scr[t, :, re] = nhr
                hs_scr[t, :, im] = nhi
                return nhr, nhi

            hr, hi = lax.fori_loop(0, tc, step, (hre_ref[:, st], him_ref[:, st]))
            hre_ref[:, st] = hr
            him_ref[:, st] = hi

    slab_w = C_GROUPS * SSM_STATE
    y_slabs = []
    for s in range(N_SSM_GROUPS // C_GROUPS):
        re0 = (s * C_GROUPS // B_GROUPS) * half_w + (s * C_GROUPS % B_GROUPS) * SSM_STATE
        h_re = hs_scr[:, :, re0:re0 + slab_w].reshape(rows, slab_w).astype(BF16)
        h_im = hs_scr[:, :, re0 + part_w:re0 + part_w + slab_w].reshape(rows, slab_w).astype(BF16)
        y_slabs.append(jnp.dot(h_re, c_ref[s, 0], preferred_element_type=F32)
                       + jnp.dot(h_im, c_ref[s, 1], preferred_element_type=F32))
    y = jnp.concatenate(y_slabs, axis=1) + d_ref[...] * u
    y = jax.nn.gelu(y)
    gate = jnp.dot(y.astype(BF16), wglu_ref[...], preferred_element_type=F32)
    z_ref[...] = (y * jax.nn.sigmoid(gate)).reshape(tc, bk, D_SSM)


def _ssm(u3, bb, lb_re, lb_im, c_full, d, w_glu, h0_re, h0_im, tc):
    S, bk, _ = u3.shape
    const = lambda shape: pl.BlockSpec(shape, lambda t: (0,) * len(shape))
    seq = pl.BlockSpec((tc, bk, D_SSM), lambda t: (t, 0, 0))
    rep = lambda a: jnp.broadcast_to(a.reshape(1, N_STATE), (bk, N_STATE))
    z, h_re, h_im = pl.pallas_call(
        functools.partial(_ssm_kernel, tc=tc, bk=bk),
        grid=(S // tc,),
        in_specs=[seq, const(bb.shape), const((bk, N_STATE)), const((bk, N_STATE)),
                  const(c_full.shape), const((1, D_SSM)), const((D_SSM, D_SSM)),
                  const((bk, N_STATE)), const((bk, N_STATE))],
        out_specs=[seq, const((bk, N_STATE)), const((bk, N_STATE))],
        out_shape=[jax.ShapeDtypeStruct((S, bk, D_SSM), F32), jax.ShapeDtypeStruct((bk, N_STATE), F32),
                   jax.ShapeDtypeStruct((bk, N_STATE), F32)],
        scratch_shapes=[pltpu.VMEM((tc, bk, 2 * N_STATE), F32), pltpu.VMEM((tc, bk, 2 * N_STATE), F32)],
        compiler_params=_params("arbitrary"),
        name="ssm",
    )(u3, bb, rep(lb_re), rep(lb_im), c_full, d.reshape(1, D_SSM), w_glu, h0_re, h0_im)
    return z, h_re, h_im


def _ffn_kernel(*refs, tm, decode):
    if decode:
        (x_ref, o_ref, z_ref, wo_ref, gn_ref, wu_ref, cw_ref, cb_ref, wd_ref, fn_ref, p2_ref, p1_ref,
         y_ref, a_out_ref, hid_scr) = refs
    else:
        (x_ref, o_ref, z_ref, wo_ref, gn_ref, wu_ref, cw_ref, cb_ref, wd_ref, fn_ref,
         y_ref, cs_ref, hid_scr, carry_scr) = refs

        @pl.when(pl.program_id(1) == 0)
        def _():
            carry_scr[...] = jnp.zeros(carry_scr.shape, F32)

    mix = (jnp.dot(o_ref[...].astype(BF16), wo_ref[:D_ATT, :], preferred_element_type=F32)
           + jnp.dot(z_ref[...].astype(BF16), wo_ref[D_ATT:, :], preferred_element_type=F32))
    x1 = x_ref[...] + mix
    r = lax.rsqrt(jnp.mean(x1 * x1, axis=-1, keepdims=True) + NORM_EPS)
    hn = (x1 * r * gn_ref[...]).astype(BF16)
    for j in range(N_FF_CHUNKS):
        cols = slice(j * FF_CHUNK, (j + 1) * FF_CHUNK)
        gcols = slice(D_FF + j * FF_CHUNK, D_FF + (j + 1) * FF_CHUNK)
        a = jnp.dot(hn, wu_ref[:, cols], preferred_element_type=F32)
        gate = jnp.dot(hn, wu_ref[:, gcols], preferred_element_type=F32)
        if decode:
            am2, am1 = p2_ref[:, cols], p1_ref[:, cols]
            a_out_ref[:, cols] = a
        else:
            prev = carry_scr[:, cols]
            p2 = jnp.broadcast_to(prev[CARRY_ROWS - 2:CARRY_ROWS - 1], a.shape)
            p1 = jnp.broadcast_to(prev[CARRY_ROWS - 1:], a.shape)
            row = lax.broadcasted_iota(jnp.int32, a.shape, 0)
            am1 = jnp.where(row < 1, p1, pltpu.roll(a, 1, 0))
            am2 = jnp.where(row < 1, p2, jnp.where(row < 2, p1, pltpu.roll(a, 2, 0)))
            carry_scr[:, cols] = a[tm - CARRY_ROWS:]
        cw = cw_ref[:, cols]
        c = cb_ref[:, cols] + (cw[0:1] * am2 + cw[1:2] * am1 + cw[2:3] * a)
        hid_scr[:, cols] = (jax.nn.silu(c) * gate).astype(BF16)
    acc = x1 + jnp.dot(hid_scr[...], wd_ref[...], preferred_element_type=F32)
    r2 = lax.rsqrt(jnp.mean(acc * acc, axis=-1, keepdims=True) + NORM_EPS)
    y_ref[...] = acc * r2 * fn_ref[...]
    if not decode:
        cs_ref[...] = carry_scr[...]


def _ffn(x, o, z, weights, tm, prev=None):
    B, S, D = x.shape
    ns = S // tm
    decode = prev is not None
    wo, gn, wu, cw, cb, wd, fn = weights
    const = lambda shape: pl.BlockSpec(shape, lambda b, s: (0,) * len(shape), pipeline_mode=pl.Buffered(1))
    xs = pl.BlockSpec((None, tm, D), lambda b, s: (b, s, 0))
    in_specs = [xs, pl.BlockSpec((tm, D_ATT), lambda b, s: (b * ns + s, 0)),
                pl.BlockSpec((tm, D_SSM), lambda b, s: (s, b)),
                const((D, D)), const((1, D)), const((D, 2 * D_FF)), const((CONV_W, D_FF)), const((1, D_FF)),
                const((D_FF, D)), const((1, D))]
    args = [x, o, z, wo, gn, wu, cw, cb, wd, fn]
    scratch = [pltpu.VMEM((tm, D_FF), BF16)]
    if decode:
        rows = pl.BlockSpec((tm, D_FF), lambda b, s: (0, 0))
        in_specs += [rows, rows]
        args += list(prev)
        out_specs = [xs, rows]
        out_shape = [jax.ShapeDtypeStruct((B, S, D), F32), jax.ShapeDtypeStruct((tm, D_FF), F32)]
    else:
        out_specs = [xs, pl.BlockSpec((None, CARRY_ROWS, D_FF), lambda b, s: (b, 0, 0))]
        out_shape = [jax.ShapeDtypeStruct((B, S, D), F32), jax.ShapeDtypeStruct((B, CARRY_ROWS, D_FF), F32)]
        scratch.append(pltpu.VMEM((CARRY_ROWS, D_FF), F32))
    y, extra = pl.pallas_call(
        functools.partial(_ffn_kernel, tm=tm, decode=decode),
        grid=(B, ns),
        in_specs=in_specs, out_specs=out_specs, out_shape=out_shape, scratch_shapes=scratch,
        compiler_params=_params("parallel", "arbitrary"),
        name="ffn_decode" if decode else "ffn",
    )(*args)
    return y, extra


def kernel(x_prompt, x_sample, cache_k, cache_v, state_ssm_re, state_ssm_im, state_conv, page_table, norm_mix, w_in, lambda_q1, lambda_k1, lambda_q2, lambda_k2, subln_g, ssm_a_re, ssm_a_im, ssm_log_dt, ssm_b_re, ssm_b_im, ssm_c_re, ssm_c_im, ssm_d, w_glu, w_o, norm_ffn, w_up, conv_w, conv_b, w_down, final_norm):
    assert norm_mix.shape[0] == 1, "single layer"
    Bp, Sp, D = x_prompt.shape
    Bs = x_sample.shape[0]
    G, N, C = N_SSM_GROUPS, SSM_STATE, SSM_GROUP
    past_len = page_table.shape[1] * cache_k.shape[2]

    w_in_b = w_in[0].astype(BF16)
    wo_b = w_o[0].astype(BF16)
    wglu_b = w_glu[0].astype(BF16)
    ffn_w = (wo_b, norm_ffn[0].reshape(1, D), w_up[0].astype(BF16), conv_w[0], conv_b[0].reshape(1, D_FF),
             w_down[0].astype(BF16), final_norm.reshape(1, D))
    lam_vecs = [v.reshape(1, ATT_HEAD_DIM) for v in (lambda_q1[0], lambda_k1[0], lambda_q2[0], lambda_k2[0])]
    g_sub = subln_g[0].reshape(1, HEAD_W)

    lb_re, lb_im, bbt_re, bbt_im = _ssm_prep(ssm_a_re[0], ssm_a_im[0], ssm_log_dt[0], ssm_b_re[0], ssm_b_im[0])
    eye_b = jnp.eye(B_GROUPS, dtype=F32)[None, :, None, :, None]
    place_b = lambda bbt: (bbt.reshape(G // B_GROUPS, B_GROUPS, C, 1, N) * eye_b).reshape(
        G // B_GROUPS, B_GROUPS * C, B_GROUPS * N)
    bb_blk = jnp.concatenate([place_b(bbt_re), place_b(bbt_im)], axis=2).astype(BF16)
    eye_c = jnp.eye(C_GROUPS, dtype=F32)[None, :, None, :, None]
    place_c = lambda c: (jnp.transpose(c, (0, 2, 1)).reshape(G // C_GROUPS, C_GROUPS, N, 1, C) * eye_c).reshape(
        G // C_GROUPS, C_GROUPS * N, C_GROUPS * C)
    c_blk = jnp.stack([place_c(ssm_c_re[0]), -place_c(ssm_c_im[0])], axis=1).astype(BF16)

    tabs_p = _rope_tables(np.arange(Sp))
    q, k, v, u = _in_proj(x_prompt, norm_mix[0], w_in_b, tabs_p, tm=512)
    tabs_s = _rope_tables(np.full((Bs,), past_len))
    qs, ks, vs, us = _in_proj(x_sample.reshape(1, Bs, D), norm_mix[0], w_in_b, tabs_s, tm=Bs)

    o, os_ = _attn(q, k, v, qs, ks, vs, cache_k[0], cache_v[0], page_table, lam_vecs, g_sub, Bp, Sp, tq=256)

    zeros_h = jnp.zeros((Bp, N_STATE), F32)
    z, hre_p, him_p = _ssm(u.reshape(Sp, Bp, D_SSM), bb_blk, lb_re, lb_im, c_blk, ssm_d[0], wglu_b,
                           zeros_h, zeros_h, tc=64)
    y_prompt, cs_p = _ffn(x_prompt, o, z.reshape(Sp, Bp * D_SSM), ffn_w, tm=512)
    conv_prompt = cs_p[:, CARRY_ROWS - (CONV_W - 1):]

    zs, hre_s, him_s = _ssm(us.reshape(1, Bs, D_SSM), bb_blk, lb_re, lb_im, c_blk, ssm_d[0], wglu_b,
                            state_ssm_re[0].reshape(Bs, N_STATE), state_ssm_im[0].reshape(Bs, N_STATE), tc=1)
    sc = state_conv[0]
    y_s, a_s = _ffn(x_sample.reshape(1, Bs, D), os_, zs.reshape(Bs, D_SSM), ffn_w, tm=Bs, prev=(sc[:, 0], sc[:, 1]))
    conv_sample = jnp.stack([sc[:, 1], a_s], axis=1)

    st = lambda h, b: h.reshape(1, b, G, N)
    return (y_prompt, y_s.reshape(Bs, 1, D),
            k.reshape(1, Bp, Sp, N_ATT_HEADS, HEAD_W), v.reshape(1, Bp, Sp, N_ATT_HEADS, HEAD_W),
            st(hre_p, Bp), st(him_p, Bp), conv_prompt[None],
            ks.reshape(1, Bs, 1, N_ATT_HEADS, HEAD_W), vs.reshape(1, Bs, 1, N_ATT_HEADS, HEAD_W),
            st(hre_s, Bs), st(him_s, Bs), conv_sample[None])
```

```python
import functools
import math

import jax
import jax.numpy as jnp
import numpy as np
from jax import lax
from jax.experimental import pallas as pl
from jax.experimental.pallas import tpu as pltpu

F32 = jnp.float32
BF16 = jnp.bfloat16

D_MODEL = 1024
D_ATT = 512
D_SSM = 512
ATT_HEAD_DIM = 64
N_ATT_HEADS = 4
HEAD_W = 2 * ATT_HEAD_DIM
ROT_DIM = 16
ROPE_THETA = 500000.0
SSM_GROUP = 16
N_SSM_GROUPS = 32
SSM_STATE = 64
N_STATE = N_SSM_GROUPS * SSM_STATE
D_FF = 2816
CONV_W = 3
NORM_EPS = 1e-6
SUBLN_EPS = 1e-5
NEG_INF = -1e30
D_PROJ = 3 * D_ATT + D_SSM
LAMBDA_INIT = 0.8 - 0.6 * math.exp(-0.3 * 0)
LOG2_E = math.log2(math.e)

V7X_VMEM_LIMIT = 56 * 1024 * 1024
FF_CHUNK = 256
N_FF_CHUNKS = D_FF // FF_CHUNK
DECODE_GROUPS = 4
CARRY_ROWS = 8


def _params(*sem):
    return pltpu.CompilerParams(dimension_semantics=sem, vmem_limit_bytes=V7X_VMEM_LIMIT)


def _lambda(lq1_ref, lk1_ref, lq2_ref, lk2_ref):
    s1 = jnp.sum(lq1_ref[...] * lk1_ref[...], axis=-1, keepdims=True)
    s2 = jnp.sum(lq2_ref[...] * lk2_ref[...], axis=-1, keepdims=True)
    return jnp.exp(s1) - jnp.exp(s2) + LAMBDA_INIT


def _sub_ln(o, g):
    r = lax.rsqrt(jnp.mean(o * o, axis=-1, keepdims=True) + SUBLN_EPS)
    return (o * r * g) * (1.0 - LAMBDA_INIT)


def _ssm_prep_kernel(ar_ref, ai_ref, ldt_ref, art_ref, ait_ref, brt_ref, bit_ref,
                     lbre_ref, lbim_ref, bbre_ref, bbim_ref):
    dt = jnp.exp(ldt_ref[...])

    def disc(ar, ai):
        mag = jnp.exp(ar * dt)
        return mag * jnp.cos(ai * dt), mag * jnp.sin(ai * dt)

    lb_re, lb_im = disc(ar_ref[...], ai_ref[...])
    lbre_ref[...] = lb_re
    lbim_ref[...] = lb_im
    ar, ai = art_ref[...], ait_ref[...]
    t_re, t_im = disc(ar, ai)
    den = ar * ar + ai * ai
    nr, ni = t_re - 1.0, t_im
    f_re = (nr * ar + ni * ai) / den
    f_im = (ni * ar - nr * ai) / den
    br, bi = brt_ref[...], bit_ref[...]
    bbre_ref[...] = f_re * br - f_im * bi
    bbim_ref[...] = f_re * bi + f_im * br


def _ssm_prep(a_re, a_im, log_dt, b_re, b_im):
    G, N, C = N_SSM_GROUPS, SSM_STATE, SSM_GROUP
    tile = lambda a: jnp.tile(a, (1, C))
    flat = lambda b: jnp.transpose(b, (0, 2, 1)).reshape(G, C * N)
    outs = pl.pallas_call(
        _ssm_prep_kernel,
        out_shape=[jax.ShapeDtypeStruct((G, N), F32)] * 2 + [jax.ShapeDtypeStruct((G, C * N), F32)] * 2,
        name="ssm_prep",
    )(a_re, a_im, log_dt.reshape(G, 1), tile(a_re), tile(a_im), flat(b_re), flat(b_im))
    return outs


def _in_proj_kernel(x_ref, g_ref, w_ref, cos_ref, sa_ref, sb_ref, q_ref, k_ref, v_ref, u_ref):
    tm = x_ref.shape[0]
    x = x_ref[...]
    r = lax.rsqrt(jnp.mean(x * x, axis=-1, keepdims=True) + NORM_EPS)
    hn = (x * r * g_ref[...]).astype(BF16)
    proj = jnp.dot(hn, w_ref[...], preferred_element_type=F32)
    cos, sa, sb = cos_ref[...], sa_ref[...], sb_ref[...]
    half = ROT_DIM // 2

    def rope(xh):
        return xh * cos + pltpu.roll(xh, HEAD_W - half, 1) * sa + pltpu.roll(xh, half, 1) * sb

    for h in range(N_ATT_HEADS):
        lo, hi = h * HEAD_W, (h + 1) * HEAD_W
        q_ref[:, lo:hi] = rope(proj[:, lo:hi]) * (ATT_HEAD_DIM ** -0.5)
        k_ref[pl.ds(h, tm, stride=N_ATT_HEADS), :] = rope(proj[:, D_ATT + lo:D_ATT + hi])
        v_ref[pl.ds(h, tm, stride=N_ATT_HEADS), :] = proj[:, 2 * D_ATT + lo:2 * D_ATT + hi]
    u_ref[...] = proj[:, 3 * D_ATT:]


def _rope_tables(pos):
    half = ROT_DIM // 2
    inv = ROPE_THETA ** (-np.arange(0, ROT_DIM, 2, dtype=np.float64) / ROT_DIM)
    ang = np.asarray(pos, np.float64)[:, None] * inv[None, :]
    cos, sin = np.cos(ang), np.sin(ang)
    S = ang.shape[0]
    pad = np.zeros((S, ATT_HEAD_DIM - ROT_DIM))
    zh = np.zeros((S, half))
    comp = lambda a, b, c: np.concatenate([a, b, c], axis=1)
    two = lambda t: jnp.asarray(np.concatenate([t, t], axis=1), F32)
    return (two(comp(cos, cos, pad + 1.0)), two(comp(-sin, zh, pad)), two(comp(zh, sin, pad)))


def _in_proj(x, g, w_bf16, tables, tm):
    B, S, D = x.shape
    ns = S // tm
    row = lambda b, s: (b * ns + s, 0)
    tab = pl.BlockSpec((tm, HEAD_W), lambda b, s: (s, 0))
    q_spec = pl.BlockSpec((tm, D_ATT), row)
    kv_spec = pl.BlockSpec((tm * N_ATT_HEADS, HEAD_W), row)
    kv_shape = jax.ShapeDtypeStruct((B * S * N_ATT_HEADS, HEAD_W), F32)
    q, k, v, u = pl.pallas_call(
        _in_proj_kernel,
        grid=(B, ns),
        in_specs=[pl.BlockSpec((None, tm, D), lambda b, s: (b, s, 0)),
                  pl.BlockSpec((1, D), lambda b, s: (0, 0)),
                  pl.BlockSpec((D, D_PROJ), lambda b, s: (0, 0)),
                  tab, tab, tab],
        out_specs=[q_spec, kv_spec, kv_spec, pl.BlockSpec((tm, D_SSM), lambda b, s: (s, b))],
        out_shape=[jax.ShapeDtypeStruct((B * S, D_ATT), F32), kv_shape, kv_shape,
                   jax.ShapeDtypeStruct((S, B * D_SSM), F32)],
        compiler_params=_params("parallel", "parallel"),
        name="in_proj",
    )(x, g.reshape(1, D), w_bf16, *tables)
    return q, k, v, u


def _prompt_tile(i, lam, g_col, q_ref, k_ref, v_ref, o_ref, tq):
    nh = N_ATT_HEADS
    feat = lax.broadcasted_iota(jnp.int32, (HEAD_W, tq), 0)
    units = [(h, c) for h in range(nh) for c in range(2)]
    hs = lambda h: slice(h * HEAD_W, (h + 1) * HEAD_W)
    q_t = []
    for h, c in units:
        qh_t = (q_ref[:, hs(h)] * LOG2_E).T
        own = (feat < ATT_HEAD_DIM) if c == 0 else (feat >= ATT_HEAD_DIM)
        q_t.append(jnp.where(own, qh_t, 0.0))

    ones_rows = jnp.ones((8, tq), F32)

    def block(j, carry, masked):
        start = pl.multiple_of(j * tq, tq)
        n = range(len(units))
        rows = lambda h: pl.ds(start * N_ATT_HEADS + h, tq, stride=N_ATT_HEADS)
        s = [jnp.dot(k_ref[rows(h), :], q_t[u], preferred_element_type=F32)
             for u, (h, c) in enumerate(units)]
        if masked:
            key = lax.broadcasted_iota(jnp.int32, (tq, tq), 0)
            qry = lax.broadcasted_iota(jnp.int32, (tq, tq), 1)
            s = [jnp.where(key <= qry, su, NEG_INF) for su in s]
        m_new = [jnp.maximum(carry[u][0], jnp.max(s[u], axis=0, keepdims=True)) for u in n]
        alpha = [jnp.exp2(carry[u][0] - m_new[u]) for u in n]
        p = [jnp.exp2(s[u] - m_new[u]) for u in n]
        pv = [jnp.dot(jnp.concatenate([v_ref[rows(h), :].T, ones_rows], axis=0), p[u],
                      preferred_element_type=F32) for u, (h, c) in enumerate(units)]
        return tuple((m_new[u], alpha[u] * carry[u][1] + pv[u][HEAD_W:HEAD_W + 1],
                      alpha[u] * carry[u][2] + pv[u][:HEAD_W]) for u in n)

    init = tuple((jnp.full((1, tq), NEG_INF, F32), jnp.zeros((1, tq), F32), jnp.zeros((HEAD_W, tq), F32))
                 for _ in units)
    carry = lax.fori_loop(0, i, lambda j, c: block(j, c, False), init)
    carry = block(i, carry, True)
    for h in range(nh):
        (_, l1, acc1), (_, l2, acc2) = carry[2 * h], carry[2 * h + 1]
        o_t = acc1 / l1 - lam * (acc2 / l2)
        r = lax.rsqrt(jnp.mean(o_t * o_t, axis=0, keepdims=True) + SUBLN_EPS)
        o_ref[:, h * HEAD_W:(h + 1) * HEAD_W] = ((o_t * r * g_col) * (1.0 - LAMBDA_INIT)).T


def _decode_step(j, n_steps, lam, g_row, q_ref, kn_ref, vn_ref, k_refs, v_refs, o_ref, m_scr, l_scr, acc_scr):
    H = N_ATT_HEADS

    @pl.when(j == 0)
    def _():
        m_scr[...] = jnp.full(m_scr.shape, NEG_INF, F32)
        l_scr[...] = jnp.zeros(l_scr.shape, F32)
        acc_scr[...] = jnp.zeros(acc_scr.shape, F32)

    q = q_ref[...]
    lane = lax.broadcasted_iota(jnp.int32, q.shape, 1)
    q8 = jnp.concatenate([jnp.where(lane < ATT_HEAD_DIM, q, 0.0), jnp.where(lane >= ATT_HEAD_DIM, q, 0.0)], axis=0)
    rows = k_refs[0].shape[0]
    zero = jnp.zeros_like(q8)
    pad = jnp.zeros((HEAD_W - 4 * H, 2 * HEAD_W), F32)
    q_pair = jnp.concatenate([jnp.concatenate([q8, zero], axis=1), jnp.concatenate([zero, q8], axis=1), pad], axis=0)
    r = lax.broadcasted_iota(jnp.int32, (4 * H, rows), 0)
    col = lax.broadcasted_iota(jnp.int32, (4 * H, rows), 1)
    own = (col % H) == (r % H)
    both = lambda x: jnp.concatenate([x, x], axis=0)
    halves = lambda x, op: op(x[:2 * H], x[2 * H:])

    def scores(group):
        s = []
        for a in group:
            k_pair = jnp.concatenate([k_refs[a][...], k_refs[a + 1][...]], axis=1)
            s_t = lax.dot_general(k_pair, q_pair, (((1,), (1,)), ((), ())), preferred_element_type=F32)
            s.append(jnp.where(own, s_t.T[:4 * H], NEG_INF))
        return jnp.concatenate(s, axis=1)

    def weighted_values(group, p):
        pv = None
        for n, a in enumerate(group):
            blk = slice(n * rows, (n + 1) * rows)
            part = (jnp.dot(p[:2 * H, blk], v_refs[a][...], preferred_element_type=F32)
                    + jnp.dot(p[2 * H:, blk], v_refs[a + 1][...], preferred_element_type=F32))
            pv = part if pv is None else pv + part
        return pv

    pairs = list(range(0, len(k_refs), 2))
    groups = [pairs[g::DECODE_GROUPS] for g in range(DECODE_GROUPS)]
    s = [scores(g) for g in groups]
    m_g = [halves(jnp.max(sg, axis=-1, keepdims=True), jnp.maximum) for sg in s]
    p = [jnp.exp(sg - both(mg)) for sg, mg in zip(s, m_g)]
    l_g = [halves(jnp.sum(pg, axis=-1, keepdims=True), jnp.add) for pg in p]
    pv_g = [weighted_values(g, pg) for g, pg in zip(groups, p)]
    m = m_scr[...]
    m_new = functools.reduce(jnp.maximum, m_g, m)
    alpha = jnp.exp(m - m_new)
    l, acc = alpha * l_scr[...], alpha * acc_scr[...]
    for mg, lg, pvg in zip(m_g, l_g, pv_g):
        w = jnp.exp(mg - m_new)
        l, acc = l + w * lg, acc + w * pvg
    m_scr[...] = m_new
    l_scr[...] = l
    acc_scr[...] = acc

    @pl.when(j == n_steps - 1)
    def _():
        kn = jnp.concatenate([kn_ref[...], kn_ref[...]], axis=0)
        vn = jnp.concatenate([vn_ref[...], vn_ref[...]], axis=0)
        s_new = jnp.sum(q8 * kn, axis=-1, keepdims=True)
        m_old = m_scr[...]
        m_fin = jnp.maximum(m_old, s_new)
        a = jnp.exp(m_old - m_fin)
        p_new = jnp.exp(s_new - m_fin)
        l = a * l_scr[...] + p_new
        acc = (a * acc_scr[...] + p_new * vn) / l
        o_ref[...] = _sub_ln(acc[:H] - lam * acc[H:], g_row)


def _attn_kernel(pt_ref, lq1_ref, lk1_ref, lq2_ref, lk2_ref, gcol_ref, grow_ref, q_ref, k_ref, v_ref,
                 qs_ref, kn_ref, vn_ref, ck_hbm, cv_hbm, o_ref, os_ref, m_scr, l_scr, acc_scr, kbuf, vbuf, sem,
                 *, tq, steps_per_seq, pages):
    i = pl.program_id(1)
    step = pl.program_id(0) * pl.num_programs(1) + i
    n_steps = pl.num_programs(0) * pl.num_programs(1)

    def page_copies(of_step, p):
        slot = of_step % 2
        pid = pt_ref[of_step // steps_per_seq, (of_step % steps_per_seq) * pages + p]
        return (pltpu.make_async_copy(ck_hbm.at[pid], kbuf.at[slot, p], sem.at[slot, 0]),
                pltpu.make_async_copy(cv_hbm.at[pid], vbuf.at[slot, p], sem.at[slot, 1]))

    def start_fetch(of_step):
        def body(p, _):
            for c in page_copies(of_step, p):
                c.start()
            return 0
        lax.fori_loop(0, pages, body, 0)

    def wait_fetch(of_step):
        def body(p, _):
            for c in page_copies(of_step, p):
                c.wait()
            return 0
        lax.fori_loop(0, pages, body, 0)

    @pl.when(step == 0)
    def _():
        start_fetch(step)

    @pl.when(step + 1 < n_steps)
    def _():
        start_fetch(step + 1)

    lam = _lambda(lq1_ref, lk1_ref, lq2_ref, lk2_ref)
    _prompt_tile(i, lam, gcol_ref[...], q_ref, k_ref, v_ref, o_ref, tq)
    wait_fetch(step)
    slot = step % 2
    _decode_step(step % steps_per_seq, steps_per_seq, lam, grow_ref[...], qs_ref, kn_ref, vn_ref,
                 [kbuf.at[slot, p] for p in range(pages)], [vbuf.at[slot, p] for p in range(pages)],
                 os_ref, m_scr, l_scr, acc_scr)


def _attn(q, k, v, qs, ks, vs, cache_k, cache_v, page_table, lam_vecs, g, B, S, tq):
    H = N_ATT_HEADS
    nq = S // tq
    Bs, n_pages = page_table.shape
    n_pool, page = cache_k.shape[0], cache_k.shape[1]
    steps_per_seq = (B * nq) // Bs
    assert steps_per_seq * Bs == B * nq and n_pages % steps_per_seq == 0
    P = n_pages // steps_per_seq
    ck = cache_k.reshape(n_pool, page * H, HEAD_W)
    cv = cache_v.reshape(n_pool, page * H, HEAD_W)
    seq = lambda b, i: (b * nq + i) // steps_per_seq
    vec = pl.BlockSpec((1, ATT_HEAD_DIM), lambda b, i, pt: (0, 0))
    kv = pl.BlockSpec((S * H, HEAD_W), lambda b, i, pt: (b, 0))
    qo = pl.BlockSpec((tq, D_ATT), lambda b, i, pt: (b * nq + i, 0))
    tok = pl.BlockSpec((None, H, HEAD_W), lambda b, i, pt: (seq(b, i), 0, 0))
    hbm = pl.BlockSpec(memory_space=pl.ANY)
    page_buf = pltpu.VMEM((2, P, page * H, HEAD_W), F32)
    grid_spec = pltpu.PrefetchScalarGridSpec(
        num_scalar_prefetch=1,
        grid=(B, nq),
        in_specs=[vec, vec, vec, vec, pl.BlockSpec((HEAD_W, 1), lambda b, i, pt: (0, 0)),
                  pl.BlockSpec((1, HEAD_W), lambda b, i, pt: (0, 0)), qo, kv, kv, tok, tok, tok, hbm, hbm],
        out_specs=[qo, tok],
        scratch_shapes=[pltpu.VMEM((2 * H, 1), F32), pltpu.VMEM((2 * H, 1), F32), pltpu.VMEM((2 * H, HEAD_W), F32),
                        page_buf, page_buf, pltpu.SemaphoreType.DMA((2, 2))],
    )
    tok3 = lambda a: a.reshape(Bs, H, HEAD_W)
    o, o_s = pl.pallas_call(
        functools.partial(_attn_kernel, tq=tq, steps_per_seq=steps_per_seq, pages=P),
        grid_spec=grid_spec,
        out_shape=[jax.ShapeDtypeStruct((B * S, D_ATT), F32), jax.ShapeDtypeStruct((Bs, H, HEAD_W), F32)],
        compiler_params=_params("arbitrary", "arbitrary"),
        name="attn",
    )(page_table, *lam_vecs, g.reshape(HEAD_W, 1), g.reshape(1, HEAD_W), q, k, v, tok3(qs), tok3(ks), tok3(vs), ck, cv)
    return o, o_s.reshape(Bs, D_ATT)


SCAN_COLS = 1024
B_GROUPS = 16
C_GROUPS = 8


def _ssm_kernel(u_ref, bb_ref, lre_ref, lim_ref, c_ref, d_ref, wglu_ref, h0re_ref, h0im_ref,
                z_ref, hre_ref, him_ref, bu_scr, hs_scr, *, tc, bk):
    t0 = pl.program_id(0)

    @pl.when(t0 == 0)
    def _():
        hre_ref[...] = h0re_ref[...]
        him_ref[...] = h0im_ref[...]

    rows = tc * bk
    u = u_ref[...].reshape(rows, D_SSM)
    ub = u.astype(BF16)
    part_w = B_GROUPS * SSM_STATE
    half_w = 2 * part_w
    for half in range(N_SSM_GROUPS // B_GROUPS):
        u_half = ub[:, half * B_GROUPS * SSM_GROUP:(half + 1) * B_GROUPS * SSM_GROUP]
        bu = jnp.dot(u_half, bb_ref[half], preferred_element_type=F32)
        bu_scr[:, :, half * half_w:(half + 1) * half_w] = bu.reshape(tc, bk, half_w)

    for half in range(N_SSM_GROUPS // B_GROUPS):
        for cg in range(part_w // SCAN_COLS):
            st = slice(half * part_w + cg * SCAN_COLS, half * part_w + (cg + 1) * SCAN_COLS)
            re = slice(half * half_w + cg * SCAN_COLS, half * half_w + (cg + 1) * SCAN_COLS)
            im = slice(re.start + part_w, re.stop + part_w)
            lr, li = lre_ref[:, st], lim_ref[:, st]

            def step(t, carry, lr=lr, li=li, re=re, im=im):
                hr, hi = carry
                nhr = lr * hr - li * hi + bu_scr[t, :, re]
                nhi = lr * hi + li * hr + bu_scr[t, :, im]
                hs_scr[t, :, re] = nhr
                hs_scr[t, :, im] = nhi
                return nhr, nhi

            hr, hi = lax.fori_loop(0, tc, step, (hre_ref[:, st], him_ref[:, st]))
            hre_ref[:, st] = hr
            him_ref[:, st] = hi

    slab_w = C_GROUPS * SSM_STATE
    y_slabs = []
    for s in range(N_SSM_GROUPS // C_GROUPS):
        re0 = (s * C_GROUPS // B_GROUPS) * half_w + (s * C_GROUPS % B_GROUPS) * SSM_STATE
        h_re = hs_scr[:, :, re0:re0 + slab_w].reshape(rows, slab_w).astype(BF16)
        h_im = hs_scr[:, :, re0 + part_w:re0 + part_w + slab_w].reshape(rows, slab_w).astype(BF16)
        y_slabs.append(jnp.dot(h_re, c_ref[s, 0], preferred_element_type=F32)
                       + jnp.dot(h_im, c_ref[s, 1], preferred_element_type=F32))
    y = jnp.concatenate(y_slabs, axis=1) + d_ref[...] * u
    y = jax.nn.gelu(y)
    gate = jnp.dot(y.astype(BF16), wglu_ref[...], preferred_element_type=F32)
    z_ref[...] = (y * jax.nn.sigmoid(gate)).reshape(tc, bk, D_SSM)


def _ssm(u3, bb, lb_re, lb_im, c_full, d, w_glu, h0_re, h0_im, tc):
    S, bk, _ = u3.shape
    const = lambda shape: pl.BlockSpec(shape, lambda t: (0,) * len(shape))
    seq = pl.BlockSpec((tc, bk, D_SSM), lambda t: (t, 0, 0))
    rep = lambda a: jnp.broadcast_to(a.reshape(1, N_STATE), (bk, N_STATE))
    z, h_re, h_im = pl.pallas_call(
        functools.partial(_ssm_kernel, tc=tc, bk=bk),
        grid=(S // tc,),
        in_specs=[seq, const(bb.shape), const((bk, N_STATE)), const((bk, N_STATE)),
                  const(c_full.shape), const((1, D_SSM)), const((D_SSM, D_SSM)),
                  const((bk, N_STATE)), const((bk, N_STATE))],
        out_specs=[seq, const((bk, N_STATE)), const((bk, N_STATE))],
        out_shape=[jax.ShapeDtypeStruct((S, bk, D_SSM), F32), jax.ShapeDtypeStruct((bk, N_STATE), F32),
                   jax.ShapeDtypeStruct((bk, N_STATE), F32)],
        scratch_shapes=[pltpu.VMEM((tc, bk, 2 * N_STATE), F32), pltpu.VMEM((tc, bk, 2 * N_STATE), F32)],
        compiler_params=_params("arbitrary"),
        name="ssm",
    )(u3, bb, rep(lb_re), rep(lb_im), c_full, d.reshape(1, D_SSM), w_glu, h0_re, h0_im)
    return z, h_re, h_im


def _ffn_kernel(*refs, tm, decode):
    if decode:
        (x_ref, o_ref, z_ref, wo_ref, gn_ref, wu_ref, cw_ref, cb_ref, wd_ref, fn_ref, p2_ref, p1_ref,
         y_ref, a_out_ref, hid_scr) = refs
    else:
        (x_ref, o_ref, z_ref, wo_ref, gn_ref, wu_ref, cw_ref, cb_ref, wd_ref, fn_ref,
         y_ref, cs_ref, hid_scr, carry_scr) = refs

        @pl.when(pl.program_id(1) == 0)
        def _():
            carry_scr[...] = jnp.zeros(carry_scr.shape, F32)

    mix = (jnp.dot(o_ref[...].astype(BF16), wo_ref[:D_ATT, :], preferred_element_type=F32)
           + jnp.dot(z_ref[...].astype(BF16), wo_ref[D_ATT:, :], preferred_element_type=F32))
    x1 = x_ref[...] + mix
    r = lax.rsqrt(jnp.mean(x1 * x1, axis=-1, keepdims=True) + NORM_EPS)
    hn = (x1 * r * gn_ref[...]).astype(BF16)
    for j in range(N_FF_CHUNKS):
        cols = slice(j * FF_CHUNK, (j + 1) * FF_CHUNK)
        gcols = slice(D_FF + j * FF_CHUNK, D_FF + (j + 1) * FF_CHUNK)
        a = jnp.dot(hn, wu_ref[:, cols], preferred_element_type=F32)
        gate = jnp.dot(hn, wu_ref[:, gcols], preferred_element_type=F32)
        if decode:
            am2, am1 = p2_ref[:, cols], p1_ref[:, cols]
            a_out_ref[:, cols] = a
        else:
            prev = carry_scr[:, cols]
            p2 = jnp.broadcast_to(prev[CARRY_ROWS - 2:CARRY_ROWS - 1], a.shape)
            p1 = jnp.broadcast_to(prev[CARRY_ROWS - 1:], a.shape)
            row = lax.broadcasted_iota(jnp.int32, a.shape, 0)
            am1 = jnp.where(row < 1, p1, pltpu.roll(a, 1, 0))
            am2 = jnp.where(row < 1, p2, jnp.where(row < 2, p1, pltpu.roll(a, 2, 0)))
            carry_scr[:, cols] = a[tm - CARRY_ROWS:]
        cw = cw_ref[:, cols]
        c = cb_ref[:, cols] + (cw[0:1] * am2 + cw[1:2] * am1 + cw[2:3] * a)
        hid_scr[:, cols] = (jax.nn.silu(c) * gate).astype(BF16)
    acc = x1 + jnp.dot(hid_scr[...], wd_ref[...], preferred_element_type=F32)
    r2 = lax.rsqrt(jnp.mean(acc * acc, axis=-1, keepdims=True) + NORM_EPS)
    y_ref[...] = acc * r2 * fn_ref[...]
    if not decode:
        cs_ref[...] = carry_scr[...]


def _ffn(x, o, z, weights, tm, prev=None):
    B, S, D = x.shape
    ns = S // tm
    decode = prev is not None
    wo, gn, wu, cw, cb, wd, fn = weights
    const = lambda shape: pl.BlockSpec(shape, lambda b, s: (0,) * len(shape), pipeline_mode=pl.Buffered(1))
    xs = pl.BlockSpec((None, tm, D), lambda b, s: (b, s, 0))
    in_specs = [xs, pl.BlockSpec((tm, D_ATT), lambda b, s: (b * ns + s, 0)),
                pl.BlockSpec((tm, D_SSM), lambda b, s: (s, b)),
                const((D, D)), const((1, D)), const((D, 2 * D_FF)), const((CONV_W, D_FF)), const((1, D_FF)),
                const((D_FF, D)), const((1, D))]
    args = [x, o, z, wo, gn, wu, cw, cb, wd, fn]
    scratch = [pltpu.VMEM((tm, D_FF), BF16)]
    if decode:
        rows = pl.BlockSpec((tm, D_FF), lambda b, s: (0, 0))
        in_specs += [rows, rows]
        args += list(prev)
        out_specs = [xs, rows]
        out_shape = [jax.ShapeDtypeStruct((B, S, D), F32), jax.ShapeDtypeStruct((tm, D_FF), F32)]
    else:
        out_specs = [xs, pl.BlockSpec((None, CARRY_ROWS, D_FF), lambda b, s: (b, 0, 0))]
        out_shape = [jax.ShapeDtypeStruct((B, S, D), F32), jax.ShapeDtypeStruct((B, CARRY_ROWS, D_FF), F32)]
        scratch.append(pltpu.VMEM((CARRY_ROWS, D_FF), F32))
    y, extra = pl.pallas_call(
        functools.partial(_ffn_kernel, tm=tm, decode=decode),
        grid=(B, ns),
        in_specs=in_specs, out_specs=out_specs, out_shape=out_shape, scratch_shapes=scratch,
        compiler_params=_params("parallel", "arbitrary"),
        name="ffn_decode" if decode else "ffn",
    )(*args)
    return y, extra


def kernel(x_prompt, x_sample, cache_k, cache_v, state_ssm_re, state_ssm_im, state_conv, page_table, norm_mix, w_in, lambda_q1, lambda_k1, lambda_q2, lambda_k2, subln_g, ssm_a_re, ssm_a_im, ssm_log_dt, ssm_b_re, ssm_b_im, ssm_c_re, ssm_c_im, ssm_d, w_glu, w_o, norm_ffn, w_up, conv_w, conv_b, w_down, final_norm):
    assert norm_mix.shape[0] == 1, "single layer"
    Bp, Sp, D = x_prompt.shape
    Bs = x_sample.shape[0]
    G, N, C = N_SSM_GROUPS, SSM_STATE, SSM_GROUP
    past_len = page_table.shape[1] * cache_k.shape[2]

    w_in_b = w_in[0].astype(BF16)
    wo_b = w_o[0].astype(BF16)
    wglu_b = w_glu[0].astype(BF16)
    ffn_w = (wo_b, norm_ffn[0].reshape(1, D), w_up[0].astype(BF16), conv_w[0], conv_b[0].reshape(1, D_FF),
             w_down[0].astype(BF16), final_norm.reshape(1, D))
    lam_vecs = [v.reshape(1, ATT_HEAD_DIM) for v in (lambda_q1[0], lambda_k1[0], lambda_q2[0], lambda_k2[0])]
    g_sub = subln_g[0].reshape(1, HEAD_W)

    lb_re, lb_im, bbt_re, bbt_im = _ssm_prep(ssm_a_re[0], ssm_a_im[0], ssm_log_dt[0], ssm_b_re[0], ssm_b_im[0])
    eye_b = jnp.eye(B_GROUPS, dtype=F32)[None, :, None, :, None]
    place_b = lambda bbt: (bbt.reshape(G // B_GROUPS, B_GROUPS, C, 1, N) * eye_b).reshape(
        G // B_GROUPS, B_GROUPS * C, B_GROUPS * N)
    bb_blk = jnp.concatenate([place_b(bbt_re), place_b(bbt_im)], axis=2).astype(BF16)
    eye_c = jnp.eye(C_GROUPS, dtype=F32)[None, :, None, :, None]
    place_c = lambda c: (jnp.transpose(c, (0, 2, 1)).reshape(G // C_GROUPS, C_GROUPS, N, 1, C) * eye_c).reshape(
        G // C_GROUPS, C_GROUPS * N, C_GROUPS * C)
    c_blk = jnp.stack([place_c(ssm_c_re[0]), -place_c(ssm_c_im[0])], axis=1).astype(BF16)

    tabs_p = _rope_tables(np.arange(Sp))
    q, k, v, u = _in_proj(x_prompt, norm_mix[0], w_in_b, tabs_p, tm=512)
    tabs_s = _rope_tables(np.full((Bs,), past_len))
    qs, ks, vs, us = _in_proj(x_sample.reshape(1, Bs, D), norm_mix[0], w_in_b, tabs_s, tm=Bs)

    o, os_ = _attn(q, k, v, qs, ks, vs, cache_k[0], cache_v[0], page_table, lam_vecs, g_sub, Bp, Sp, tq=256)

    zeros_h = jnp.zeros((Bp, N_STATE), F32)
    z, hre_p, him_p = _ssm(u.reshape(Sp, Bp, D_SSM), bb_blk, lb_re, lb_im, c_blk, ssm_d[0], wglu_b,
                           zeros_h, zeros_h, tc=64)
    y_prompt, cs_p = _ffn(x_prompt, o, z.reshape(Sp, Bp * D_SSM), ffn_w, tm=512)
    conv_prompt = cs_p[:, CARRY_ROWS - (CONV_W - 1):]

    zs, hre_s, him_s = _ssm(us.reshape(1, Bs, D_SSM), bb_blk, lb_re, lb_im, c_blk, ssm_d[0], wglu_b,
                            state_ssm_re[0].reshape(Bs, N_STATE), state_ssm_im[0].reshape(Bs, N_STATE), tc=1)
    sc = state_conv[0]
    y_s, a_s = _ffn(x_sample.reshape(1, Bs, D), os_, zs.reshape(Bs, D_SSM), ffn_w, tm=Bs, prev=(sc[:, 0], sc[:, 1]))
    conv_sample = jnp.stack([sc[:, 1], a_s], axis=1)

    st = lambda h, b: h.reshape(1, b, G, N)
    return (y_prompt, y_s.reshape(Bs, 1, D),
            k.reshape(1, Bp, Sp, N_ATT_HEADS, HEAD_W), v.reshape(1, Bp, Sp, N_ATT_HEADS, HEAD_W),
            st(hre_p, Bp), st(him_p, Bp), conv_prompt[None],
            ks.reshape(1, Bs, 1, N_ATT_HEADS, HEAD_W), vs.reshape(1, Bs, 1, N_ATT_HEADS, HEAD_W),
            st(hre_s, Bs), st(him_s, Bs), conv_sample[None])
```

```python
import functools
import math

import jax
import jax.numpy as jnp
import numpy as np
from jax import lax
from jax.experimental import pallas as pl
from jax.experimental.pallas import tpu as pltpu

F32 = jnp.float32
BF16 = jnp.bfloat16

D_MODEL = 1024
D_ATT = 512
D_SSM = 512
ATT_HEAD_DIM = 64
N_ATT_HEADS = 4
HEAD_W = 2 * ATT_HEAD_DIM
ROT_DIM = 16
ROPE_THETA = 500000.0
SSM_GROUP = 16
N_SSM_GROUPS = 32
SSM_STATE = 64
N_STATE = N_SSM_GROUPS * SSM_STATE
D_FF = 2816
CONV_W = 3
NORM_EPS = 1e-6
SUBLN_EPS = 1e-5
NEG_INF = -1e30
D_PROJ = 3 * D_ATT + D_SSM
LAMBDA_INIT = 0.8 - 0.6 * math.exp(-0.3 * 0)
LOG2_E = math.log2(math.e)

V7X_VMEM_LIMIT = 56 * 1024 * 1024
FF_CHUNK = 256
N_FF_CHUNKS = D_FF // FF_CHUNK
DECODE_GROUPS = 4
CARRY_ROWS = 8


def _params(*sem):
    return pltpu.CompilerParams(dimension_semantics=sem, vmem_limit_bytes=V7X_VMEM_LIMIT)


def _lambda(lq1_ref, lk1_ref, lq2_ref, lk2_ref):
    s1 = jnp.sum(lq1_ref[...] * lk1_ref[...], axis=-1, keepdims=True)
    s2 = jnp.sum(lq2_ref[...] * lk2_ref[...], axis=-1, keepdims=True)
    return jnp.exp(s1) - jnp.exp(s2) + LAMBDA_INIT


def _sub_ln(o, g):
    r = lax.rsqrt(jnp.mean(o * o, axis=-1, keepdims=True) + SUBLN_EPS)
    return (o * r * g) * (1.0 - LAMBDA_INIT)


def _ssm_prep_kernel(ar_ref, ai_ref, ldt_ref, art_ref, ait_ref, brt_ref, bit_ref,
                     lbre_ref, lbim_ref, bbre_ref, bbim_ref):
    dt = jnp.exp(ldt_ref[...])

    def disc(ar, ai):
        mag = jnp.exp(ar * dt)
        return mag * jnp.cos(ai * dt), mag * jnp.sin(ai * dt)

    lb_re, lb_im = disc(ar_ref[...], ai_ref[...])
    lbre_ref[...] = lb_re
    lbim_ref[...] = lb_im
    ar, ai = art_ref[...], ait_ref[...]
    t_re, t_im = disc(ar, ai)
    den = ar * ar + ai * ai
    nr, ni = t_re - 1.0, t_im
    f_re = (nr * ar + ni * ai) / den
    f_im = (ni * ar - nr * ai) / den
    br, bi = brt_ref[...], bit_ref[...]
    bbre_ref[...] = f_re * br - f_im * bi
    bbim_ref[...] = f_re * bi + f_im * br


def _ssm_prep(a_re, a_im, log_dt, b_re, b_im):
    G, N, C = N_SSM_GROUPS, SSM_STATE, SSM_GROUP
    tile = lambda a: jnp.tile(a, (1, C))
    flat = lambda b: jnp.transpose(b, (0, 2, 1)).reshape(G, C * N)
    outs = pl.pallas_call(
        _ssm_prep_kernel,
        out_shape=[jax.ShapeDtypeStruct((G, N), F32)] * 2 + [jax.ShapeDtypeStruct((G, C * N), F32)] * 2,
        name="ssm_prep",
    )(a_re, a_im, log_dt.reshape(G, 1), tile(a_re), tile(a_im), flat(b_re), flat(b_im))
    return outs


def _in_proj_kernel(x_ref, g_ref, w_ref, cos_ref, sa_ref, sb_ref, q_ref, k_ref, v_ref, u_ref):
    tm = x_ref.shape[0]
    x = x_ref[...]
    r = lax.rsqrt(jnp.mean(x * x, axis=-1, keepdims=True) + NORM_EPS)
    hn = (x * r * g_ref[...]).astype(BF16)
    proj = jnp.dot(hn, w_ref[...], preferred_element_type=F32)
    cos, sa, sb = cos_ref[...], sa_ref[...], sb_ref[...]
    half = ROT_DIM // 2

    def rope(xh):
        return xh * cos + pltpu.roll(xh, HEAD_W - half, 1) * sa + pltpu.roll(xh, half, 1) * sb

    for h in range(N_ATT_HEADS):
        lo, hi = h * HEAD_W, (h + 1) * HEAD_W
        q_ref[:, lo:hi] = rope(proj[:, lo:hi]) * (ATT_HEAD_DIM ** -0.5)
        k_ref[pl.ds(h, tm, stride=N_ATT_HEADS), :] = rope(proj[:, D_ATT + lo:D_ATT + hi])
        v_ref[pl.ds(h, tm, stride=N_ATT_HEADS), :] = proj[:, 2 * D_ATT + lo:2 * D_ATT + hi]
    u_ref[...] = proj[:, 3 * D_ATT:]


def _rope_tables(pos):
    half = ROT_DIM // 2
    inv = ROPE_THETA ** (-np.arange(0, ROT_DIM, 2, dtype=np.float64) / ROT_DIM)
    ang = np.asarray(pos, np.float64)[:, None] * inv[None, :]
    cos, sin = np.cos(ang), np.sin(ang)
    S = ang.shape[0]
    pad = np.zeros((S, ATT_HEAD_DIM - ROT_DIM))
    zh = np.zeros((S, half))
    comp = lambda a, b, c: np.concatenate([a, b, c], axis=1)
    two = lambda t: jnp.asarray(np.concatenate([t, t], axis=1), F32)
    return (two(comp(cos, cos, pad + 1.0)), two(comp(-sin, zh, pad)), two(comp(zh, sin, pad)))


def _in_proj(x, g, w_bf16, tables, tm):
    B, S, D = x.shape
    ns = S // tm
    row = lambda b, s: (b * ns + s, 0)
    tab = pl.BlockSpec((tm, HEAD_W), lambda b, s: (s, 0))
    q_spec = pl.BlockSpec((tm, D_ATT), row)
    kv_spec = pl.BlockSpec((tm * N_ATT_HEADS, HEAD_W), row)
    kv_shape = jax.ShapeDtypeStruct((B * S * N_ATT_HEADS, HEAD_W), F32)
    q, k, v, u = pl.pallas_call(
        _in_proj_kernel,
        grid=(B, ns),
        in_specs=[pl.BlockSpec((None, tm, D), lambda b, s: (b, s, 0)),
                  pl.BlockSpec((1, D), lambda b, s: (0, 0)),
                  pl.BlockSpec((D, D_PROJ), lambda b, s: (0, 0)),
                  tab, tab, tab],
        out_specs=[q_spec, kv_spec, kv_spec, pl.BlockSpec((tm, D_SSM), lambda b, s: (s, b))],
        out_shape=[jax.ShapeDtypeStruct((B * S, D_ATT), F32), kv_shape, kv_shape,
                   jax.ShapeDtypeStruct((S, B * D_SSM), F32)],
        compiler_params=_params("parallel", "parallel"),
        name="in_proj",
    )(x, g.reshape(1, D), w_bf16, *tables)
    return q, k, v, u


def _prompt_tile(i, lam, g_col, q_ref, k_ref, v_ref, o_ref, tq):
    nh = N_ATT_HEADS
    feat = lax.broadcasted_iota(jnp.int32, (HEAD_W, tq), 0)
    units = [(h, c) for h in range(nh) for c in range(2)]
    hs = lambda h: slice(h * HEAD_W, (h + 1) * HEAD_W)
    q_t = []
    for h, c in units:
        qh_t = (q_ref[:, hs(h)] * LOG2_E).T
        own = (feat < ATT_HEAD_DIM) if c == 0 else (feat >= ATT_HEAD_DIM)
        q_t.append(jnp.where(own, qh_t, 0.0))

    ones_rows = jnp.ones((8, tq), F32)

    n = range(len(units))
    rows = lambda j, h: pl.ds(pl.multiple_of(j * tq, tq) * nh + h, tq, stride=nh)

    def scores(j, masked):
        s = [jnp.dot(k_ref[rows(j, h), :], q_t[u], preferred_element_type=F32)
             for u, (h, c) in enumerate(units)]
        if masked:
            key = lax.broadcasted_iota(jnp.int32, (tq, tq), 0)
            qry = lax.broadcasted_iota(jnp.int32, (tq, tq), 1)
            s = [jnp.where(key <= qry, su, NEG_INF) for su in s]
        return s

    def softmax(s, carry):
        m_new = [jnp.maximum(carry[u][0], jnp.max(s[u], axis=0, keepdims=True)) for u in n]
        alpha = [jnp.exp2(carry[u][0] - m_new[u]) for u in n]
        return m_new, alpha, [jnp.exp2(s[u] - m_new[u]) for u in n]

    def update(j, carry, m_new, alpha, p):
        pv = [jnp.dot(jnp.concatenate([v_ref[rows(j, h), :].T, ones_rows], axis=0), p[u],
                      preferred_element_type=F32) for u, (h, c) in enumerate(units)]
        return tuple((m_new[u], alpha[u] * carry[u][1] + pv[u][HEAD_W:HEAD_W + 1],
                      alpha[u] * carry[u][2] + pv[u][:HEAD_W]) for u in n)

    init = tuple((jnp.full((1, tq), NEG_INF, F32), jnp.zeros((1, tq), F32), jnp.zeros((HEAD_W, tq), F32))
                 for _ in units)
    carry = lax.fori_loop(0, i, lambda j, c: update(j, c, *softmax(scores(j, False), c)), init)
    yield
    s = scores(i, True)
    yield
    stats = softmax(s, carry)
    yield
    carry = update(i, carry, *stats)
    for h in range(nh):
        (_, l1, acc1), (_, l2, acc2) = carry[2 * h], carry[2 * h + 1]
        o_t = acc1 / l1 - lam * (acc2 / l2)
        r = lax.rsqrt(jnp.mean(o_t * o_t, axis=0, keepdims=True) + SUBLN_EPS)
        o_ref[:, h * HEAD_W:(h + 1) * HEAD_W] = ((o_t * r * g_col) * (1.0 - LAMBDA_INIT)).T


def _decode_step(j, n_steps, lam, g_row, q_ref, kn_ref, vn_ref, k_refs, v_refs, o_ref, m_scr, l_scr, acc_scr):
    H = N_ATT_HEADS

    @pl.when(j == 0)
    def _():
        m_scr[...] = jnp.full(m_scr.shape, NEG_INF, F32)
        l_scr[...] = jnp.zeros(l_scr.shape, F32)
        acc_scr[...] = jnp.zeros(acc_scr.shape, F32)

    q = q_ref[...]
    lane = lax.broadcasted_iota(jnp.int32, q.shape, 1)
    q8 = jnp.concatenate([jnp.where(lane < ATT_HEAD_DIM, q, 0.0), jnp.where(lane >= ATT_HEAD_DIM, q, 0.0)], axis=0)
    rows = k_refs[0].shape[0]
    zero = jnp.zeros_like(q8)
    pad = jnp.zeros((HEAD_W - 4 * H, 2 * HEAD_W), F32)
    q_pair = jnp.concatenate([jnp.concatenate([q8, zero], axis=1), jnp.concatenate([zero, q8], axis=1), pad], axis=0)
    r = lax.broadcasted_iota(jnp.int32, (4 * H, rows), 0)
    col = lax.broadcasted_iota(jnp.int32, (4 * H, rows), 1)
    own = (col % H) == (r % H)
    both = lambda x: jnp.concatenate([x, x], axis=0)
    halves = lambda x, op: op(x[:2 * H], x[2 * H:])

    def scores(group):
        k_pairs = jnp.concatenate([jnp.concatenate([k_refs[a][...], k_refs[a + 1][...]], axis=1) for a in group],
                                  axis=0)
        s_t = lax.dot_general(k_pairs, q_pair, (((1,), (1,)), ((), ())), preferred_element_type=F32)
        return jnp.concatenate([jnp.where(own, s_t[n * rows:(n + 1) * rows].T[:4 * H], NEG_INF)
                                for n in range(len(group))], axis=1)

    def weighted_values(group, p):
        pv = None
        for n, a in enumerate(group):
            blk = slice(n * rows, (n + 1) * rows)
            part = (jnp.dot(p[:2 * H, blk], v_refs[a][...], preferred_element_type=F32)
                    + jnp.dot(p[2 * H:, blk], v_refs[a + 1][...], preferred_element_type=F32))
            pv = part if pv is None else pv + part
        return pv

    pairs = list(range(0, len(k_refs), 2))
    groups = [pairs[g::DECODE_GROUPS] for g in range(DECODE_GROUPS)]
    s = [scores(g) for g in groups]
    yield
    m_g = [halves(jnp.max(sg, axis=-1, keepdims=True), jnp.maximum) for sg in s]
    p = [jnp.exp(sg - both(mg)) for sg, mg in zip(s, m_g)]
    l_g = [halves(jnp.sum(pg, axis=-1, keepdims=True), jnp.add) for pg in p]
    yield
    pv_g = [weighted_values(g, pg) for g, pg in zip(groups, p)]
    m = m_scr[...]
    m_new = functools.reduce(jnp.maximum, m_g, m)
    alpha = jnp.exp(m - m_new)
    l, acc = alpha * l_scr[...], alpha * acc_scr[...]
    for mg, lg, pvg in zip(m_g, l_g, pv_g):
        w = jnp.exp(mg - m_new)
        l, acc = l + w * lg, acc + w * pvg
    m_scr[...] = m_new
    l_scr[...] = l
    acc_scr[...] = acc

    @pl.when(j == n_steps - 1)
    def _():
        kn = jnp.concatenate([kn_ref[...], kn_ref[...]], axis=0)
        vn = jnp.concatenate([vn_ref[...], vn_ref[...]], axis=0)
        s_new = jnp.sum(q8 * kn, axis=-1, keepdims=True)
        m_old = m_scr[...]
        m_fin = jnp.maximum(m_old, s_new)
        a = jnp.exp(m_old - m_fin)
        p_new = jnp.exp(s_new - m_fin)
        l = a * l_scr[...] + p_new
        acc = (a * acc_scr[...] + p_new * vn) / l
        o_ref[...] = _sub_ln(acc[:H] - lam * acc[H:], g_row)


def _attn_kernel(pt_ref, lq1_ref, lk1_ref, lq2_ref, lk2_ref, gcol_ref, grow_ref, q_ref, k_ref, v_ref,
                 qs_ref, kn_ref, vn_ref, ck_hbm, cv_hbm, o_ref, os_ref, m_scr, l_scr, acc_scr, kbuf, vbuf, sem,
                 *, tq, steps_per_seq, pages):
    i = pl.program_id(1)
    step = pl.program_id(0) * pl.num_programs(1) + i
    n_steps = pl.num_programs(0) * pl.num_programs(1)

    def page_copies(of_step, p):
        slot = of_step % 2
        pid = pt_ref[of_step // steps_per_seq, (of_step % steps_per_seq) * pages + p]
        return (pltpu.make_async_copy(ck_hbm.at[pid], kbuf.at[slot, p], sem.at[slot, 0]),
                pltpu.make_async_copy(cv_hbm.at[pid], vbuf.at[slot, p], sem.at[slot, 1]))

    def fetch(of_step, do, unrolled):
        def one(p):
            for c in page_copies(of_step, p):
                do(c)
        if unrolled:
            for p in range(pages):
                one(p)
        else:
            lax.fori_loop(0, pages, lambda p, _: (one(p), 0)[1], 0)

    start, wait = (lambda c: c.start()), (lambda c: c.wait())

    @pl.when(step == 0)
    def _():
        fetch(step, start, unrolled=False)

    lam = _lambda(lq1_ref, lk1_ref, lq2_ref, lk2_ref)
    prompt = _prompt_tile(i, lam, gcol_ref[...], q_ref, k_ref, v_ref, o_ref, tq)
    next(prompt)
    following = lax.rem(step + 1, n_steps)
    fetch(following, start, unrolled=True)
    fetch(step, wait, unrolled=True)
    slot = step % 2
    decode = _decode_step(step % steps_per_seq, steps_per_seq, lam, grow_ref[...], qs_ref, kn_ref, vn_ref,
                          [kbuf.at[slot, p] for p in range(pages)], [vbuf.at[slot, p] for p in range(pages)],
                          os_ref, m_scr, l_scr, acc_scr)
    running = [prompt, decode]
    while running:
        for gen in list(running):
            if next(gen, StopIteration) is StopIteration:
                running.remove(gen)

    @pl.when(step == n_steps - 1)
    def _():
        fetch(following, wait, unrolled=False)


def _attn(q, k, v, qs, ks, vs, cache_k, cache_v, page_table, lam_vecs, g, B, S, tq):
    H = N_ATT_HEADS
    nq = S // tq
    Bs, n_pages = page_table.shape
    n_pool, page = cache_k.shape[0], cache_k.shape[1]
    steps_per_seq = (B * nq) // Bs
    assert steps_per_seq * Bs == B * nq and n_pages % steps_per_seq == 0
    P = n_pages // steps_per_seq
    ck = cache_k.reshape(n_pool, page * H, HEAD_W)
    cv = cache_v.reshape(n_pool, page * H, HEAD_W)
    seq = lambda b, i: (b * nq + i) // steps_per_seq
    vec = pl.BlockSpec((1, ATT_HEAD_DIM), lambda b, i, pt: (0, 0))
    kv = pl.BlockSpec((S * H, HEAD_W), lambda b, i, pt: (b, 0))
    qo = pl.BlockSpec((tq, D_ATT), lambda b, i, pt: (b * nq + i, 0))
    tok = pl.BlockSpec((None, H, HEAD_W), lambda b, i, pt: (seq(b, i), 0, 0))
    hbm = pl.BlockSpec(memory_space=pl.ANY)
    page_buf = pltpu.VMEM((2, P, page * H, HEAD_W), F32)
    grid_spec = pltpu.PrefetchScalarGridSpec(
        num_scalar_prefetch=1,
        grid=(B, nq),
        in_specs=[vec, vec, vec, vec, pl.BlockSpec((HEAD_W, 1), lambda b, i, pt: (0, 0)),
                  pl.BlockSpec((1, HEAD_W), lambda b, i, pt: (0, 0)), qo, kv, kv, tok, tok, tok, hbm, hbm],
        out_specs=[qo, tok],
        scratch_shapes=[pltpu.VMEM((2 * H, 1), F32), pltpu.VMEM((2 * H, 1), F32), pltpu.VMEM((2 * H, HEAD_W), F32),
                        page_buf, page_buf, pltpu.SemaphoreType.DMA((2, 2))],
    )
    tok3 = lambda a: a.reshape(Bs, H, HEAD_W)
    o, o_s = pl.pallas_call(
        functools.partial(_attn_kernel, tq=tq, steps_per_seq=steps_per_seq, pages=P),
        grid_spec=grid_spec,
        out_shape=[jax.ShapeDtypeStruct((B * S, D_ATT), F32), jax.ShapeDtypeStruct((Bs, H, HEAD_W), F32)],
        compiler_params=_params("arbitrary", "arbitrary"),
        name="attn",
    )(page_table, *lam_vecs, g.reshape(HEAD_W, 1), g.reshape(1, HEAD_W), q, k, v, tok3(qs), tok3(ks), tok3(vs), ck, cv)
    return o, o_s.reshape(Bs, D_ATT)


SCAN_COLS = 1024
B_GROUPS = 16
C_GROUPS = 8


def _ssm_kernel(u_ref, bb_ref, lre_ref, lim_ref, c_ref, d_ref, wglu_ref, h0re_ref, h0im_ref,
                z_ref, hre_ref, him_ref, bu_scr, hs_scr, *, tc, bk):
    t0 = pl.program_id(0)

    @pl.when(t0 == 0)
    def _():
        hre_ref[...] = h0re_ref[...]
        him_ref[...] = h0im_ref[...]

    rows = tc * bk
    u = u_ref[...].reshape(rows, D_SSM)
    ub = u.astype(BF16)
    part_w = B_GROUPS * SSM_STATE
    half_w = 2 * part_w
    for half in range(N_SSM_GROUPS // B_GROUPS):
        u_half = ub[:, half * B_GROUPS * SSM_GROUP:(half + 1) * B_GROUPS * SSM_GROUP]
        bu = jnp.dot(u_half, bb_ref[half], preferred_element_type=F32)
        bu_scr[:, :, half * half_w:(half + 1) * half_w] = bu.reshape(tc, bk, half_w)

    for half in range(N_SSM_GROUPS // B_GROUPS):
        for cg in range(part_w // SCAN_COLS):
            st = slice(half * part_w + cg * SCAN_COLS, half * part_w + (cg + 1) * SCAN_COLS)
            re = slice(half * half_w + cg * SCAN_COLS, half * half_w + (cg + 1) * SCAN_COLS)
            im = slice(re.start + part_w, re.stop + part_w)
            lr, li = lre_ref[:, st], lim_ref[:, st]

            def step(t, carry, lr=lr, li=li, re=re, im=im):
                hr, hi = carry
                nhr = lr * hr - li * hi + bu_scr[t, :, re]
                nhi = lr * hi + li * hr + bu_scr[t, :, im]
                hs_scr[t, :, re] = nhr
                hs_scr[t, :, im] = nhi
                return nhr, nhi

            hr, hi = lax.fori_loop(0, tc, step, (hre_ref[:, st], him_ref[:, st]))
            hre_ref[:, st] = hr
            him_ref[:, st] = hi

    slab_w = C_GROUPS * SSM_STATE
    y_slabs = []
    for s in range(N_SSM_GROUPS // C_GROUPS):
        re0 = (s * C_GROUPS // B_GROUPS) * half_w + (s * C_GROUPS % B_GROUPS) * SSM_STATE
        h_re = hs_scr[:, :, re0:re0 + slab_w].reshape(rows, slab_w).astype(BF16)
        h_im = hs_scr[:, :, re0 + part_w:re0 + part_w + slab_w].reshape(rows, slab_w).astype(BF16)
        y_slabs.append(jnp.dot(h_re, c_ref[s, 0], preferred_element_type=F32)
                       + jnp.dot(h_im, c_ref[s, 1], preferred_element_type=F32))
    y = jnp.concatenate(y_slabs, axis=1) + d_ref[...] * u
    y = jax.nn.gelu(y)
    gate = jnp.dot(y.astype(BF16), wglu_ref[...], preferred_element_type=F32)
    z_ref[...] = (y * jax.nn.sigmoid(gate)).reshape(tc, bk, D_SSM)


def _ssm(u3, bb, lb_re, lb_im, c_full, d, w_glu, h0_re, h0_im, tc):
    S, bk, _ = u3.shape
    const = lambda shape: pl.BlockSpec(shape, lambda t: (0,) * len(shape))
    seq = pl.BlockSpec((tc, bk, D_SSM), lambda t: (t, 0, 0))
    rep = lambda a: jnp.broadcast_to(a.reshape(1, N_STATE), (bk, N_STATE))
    z, h_re, h_im = pl.pallas_call(
        functools.partial(_ssm_kernel, tc=tc, bk=bk),
        grid=(S // tc,),
        in_specs=[seq, const(bb.shape), const((bk, N_STATE)), const((bk, N_STATE)),
                  const(c_full.shape), const((1, D_SSM)), const((D_SSM, D_SSM)),
                  const((bk, N_STATE)), const((bk, N_STATE))],
        out_specs=[seq, const((bk, N_STATE)), const((bk, N_STATE))],
        out_shape=[jax.ShapeDtypeStruct((S, bk, D_SSM), F32), jax.ShapeDtypeStruct((bk, N_STATE), F32),
                   jax.ShapeDtypeStruct((bk, N_STATE), F32)],
        scratch_shapes=[pltpu.VMEM((tc, bk, 2 * N_STATE), F32), pltpu.VMEM((tc, bk, 2 * N_STATE), F32)],
        compiler_params=_params("arbitrary"),
        name="ssm",
    )(u3, bb, rep(lb_re), rep(lb_im), c_full, d.reshape(1, D_SSM), w_glu, h0_re, h0_im)
    return z, h_re, h_im


def _ffn_kernel(*refs, tm, decode):
    if decode:
        (x_ref, o_ref, z_ref, wo_ref, gn_ref, wu_ref, cw_ref, cb_ref, wd_ref, fn_ref, p2_ref, p1_ref,
         y_ref, a_out_ref, hid_scr) = refs
    else:
        (x_ref, o_ref, z_ref, wo_ref, gn_ref, wu_ref, cw_ref, cb_ref, wd_ref, fn_ref,
         y_ref, cs_ref, hid_scr, carry_scr) = refs

        @pl.when(pl.program_id(1) == 0)
        def _():
            carry_scr[...] = jnp.zeros(carry_scr.shape, F32)

    mix = (jnp.dot(o_ref[...].astype(BF16), wo_ref[:D_ATT, :], preferred_element_type=F32)
           + jnp.dot(z_ref[...].astype(BF16), wo_ref[D_ATT:, :], preferred_element_type=F32))
    x1 = x_ref[...] + mix
    r = lax.rsqrt(jnp.mean(x1 * x1, axis=-1, keepdims=True) + NORM_EPS)
    hn = (x1 * r * gn_ref[...]).astype(BF16)
    for j in range(N_FF_CHUNKS):
        cols = slice(j * FF_CHUNK, (j + 1) * FF_CHUNK)
        gcols = slice(D_FF + j * FF_CHUNK, D_FF + (j + 1) * FF_CHUNK)
        a = jnp.dot(hn, wu_ref[:, cols], preferred_element_type=F32)
        gate = jnp.dot(hn, wu_ref[:, gcols], preferred_element_type=F32)
        if decode:
            am2, am1 = p2_ref[:, cols], p1_ref[:, cols]
            a_out_ref[:, cols] = a
        else:
            prev = carry_scr[:, cols]
            p2 = jnp.broadcast_to(prev[CARRY_ROWS - 2:CARRY_ROWS - 1], a.shape)
            p1 = jnp.broadcast_to(prev[CARRY_ROWS - 1:], a.shape)
            row = lax.broadcasted_iota(jnp.int32, a.shape, 0)
            am1 = jnp.where(row < 1, p1, pltpu.roll(a, 1, 0))
            am2 = jnp.where(row < 1, p2, jnp.where(row < 2, p1, pltpu.roll(a, 2, 0)))
            carry_scr[:, cols] = a[tm - CARRY_ROWS:]
        cw = cw_ref[:, cols]
        c = cb_ref[:, cols] + (cw[0:1] * am2 + cw[1:2] * am1 + cw[2:3] * a)
        hid_scr[:, cols] = (jax.nn.silu(c) * gate).astype(BF16)
    acc = x1 + jnp.dot(hid_scr[...], wd_ref[...], preferred_element_type=F32)
    r2 = lax.rsqrt(jnp.mean(acc * acc, axis=-1, keepdims=True) + NORM_EPS)
    y_ref[...] = acc * r2 * fn_ref[...]
    if not decode:
        cs_ref[...] = carry_scr[...]


def _ffn(x, o, z, weights, tm, prev=None):
    B, S, D = x.shape
    ns = S // tm
    decode = prev is not None
    wo, gn, wu, cw, cb, wd, fn = weights
    const = lambda shape: pl.BlockSpec(shape, lambda b, s: (0,) * len(shape), pipeline_mode=pl.Buffered(1))
    xs = pl.BlockSpec((None, tm, D), lambda b, s: (b, s, 0))
    in_specs = [xs, pl.BlockSpec((tm, D_ATT), lambda b, s: (b * ns + s, 0)),
                pl.BlockSpec((tm, D_SSM), lambda b, s: (s, b)),
                const((D, D)), const((1, D)), const((D, 2 * D_FF)), const((CONV_W, D_FF)), const((1, D_FF)),
                const((D_FF, D)), const((1, D))]
    args = [x, o, z, wo, gn, wu, cw, cb, wd, fn]
    scratch = [pltpu.VMEM((tm, D_FF), BF16)]
    if decode:
        rows = pl.BlockSpec((tm, D_FF), lambda b, s: (0, 0))
        in_specs += [rows, rows]
        args += list(prev)
        out_specs = [xs, rows]
        out_shape = [jax.ShapeDtypeStruct((B, S, D), F32), jax.ShapeDtypeStruct((tm, D_FF), F32)]
    else:
        out_specs = [xs, pl.BlockSpec((None, CARRY_ROWS, D_FF), lambda b, s: (b, 0, 0))]
        out_shape = [jax.ShapeDtypeStruct((B, S, D), F32), jax.ShapeDtypeStruct((B, CARRY_ROWS, D_FF), F32)]
        scratch.append(pltpu.VMEM((CARRY_ROWS, D_FF), F32))
    y, extra = pl.pallas_call(
        functools.partial(_ffn_kernel, tm=tm, decode=decode),
        grid=(B, ns),
        in_specs=in_specs, out_specs=out_specs, out_shape=out_shape, scratch_shapes=scratch,
        compiler_params=_params("parallel", "arbitrary"),
        name="ffn_decode" if decode else "ffn",
    )(*args)
    return y, extra


def kernel(x_prompt, x_sample, cache_k, cache_v, state_ssm_re, state_ssm_im, state_conv, page_table, norm_mix, w_in, lambda_q1, lambda_k1, lambda_q2, lambda_k2, subln_g, ssm_a_re, ssm_a_im, ssm_log_dt, ssm_b_re, ssm_b_im, ssm_c_re, ssm_c_im, ssm_d, w_glu, w_o, norm_ffn, w_up, conv_w, conv_b, w_down, final_norm):
    assert norm_mix.shape[0] == 1, "single layer"
    Bp, Sp, D = x_prompt.shape
    Bs = x_sample.shape[0]
    G, N, C = N_SSM_GROUPS, SSM_STATE, SSM_GROUP
    past_len = page_table.shape[1] * cache_k.shape[2]

    w_in_b = w_in[0].astype(BF16)
    wo_b = w_o[0].astype(BF16)
    wglu_b = w_glu[0].astype(BF16)
    ffn_w = (wo_b, norm_ffn[0].reshape(1, D), w_up[0].astype(BF16), conv_w[0], conv_b[0].reshape(1, D_FF),
             w_down[0].astype(BF16), final_norm.reshape(1, D))
    lam_vecs = [v.reshape(1, ATT_HEAD_DIM) for v in (lambda_q1[0], lambda_k1[0], lambda_q2[0], lambda_k2[0])]
    g_sub = subln_g[0].reshape(1, HEAD_W)

    lb_re, lb_im, bbt_re, bbt_im = _ssm_prep(ssm_a_re[0], ssm_a_im[0], ssm_log_dt[0], ssm_b_re[0], ssm_b_im[0])
    eye_b = jnp.eye(B_GROUPS, dtype=F32)[None, :, None, :, None]
    place_b = lambda bbt: (bbt.reshape(G // B_GROUPS, B_GROUPS, C, 1, N) * eye_b).reshape(
        G // B_GROUPS, B_GROUPS * C, B_GROUPS * N)
    bb_blk = jnp.concatenate([place_b(bbt_re), place_b(bbt_im)], axis=2).astype(BF16)
    eye_c = jnp.eye(C_GROUPS, dtype=F32)[None, :, None, :, None]
    place_c = lambda c: (jnp.transpose(c, (0, 2, 1)).reshape(G // C_GROUPS, C_GROUPS, N, 1, C) * eye_c).reshape(
        G // C_GROUPS, C_GROUPS * N, C_GROUPS * C)
    c_blk = jnp.stack([place_c(ssm_c_re[0]), -place_c(ssm_c_im[0])], axis=1).astype(BF16)

    tabs_p = _rope_tables(np.arange(Sp))
    q, k, v, u = _in_proj(x_prompt, norm_mix[0], w_in_b, tabs_p, tm=512)
    tabs_s = _rope_tables(np.full((Bs,), past_len))
    qs, ks, vs, us = _in_proj(x_sample.reshape(1, Bs, D), norm_mix[0], w_in_b, tabs_s, tm=Bs)

    o, os_ = _attn(q, k, v, qs, ks, vs, cache_k[0], cache_v[0], page_table, lam_vecs, g_sub, Bp, Sp, tq=256)

    zeros_h = jnp.zeros((Bp, N_STATE), F32)
    z, hre_p, him_p = _ssm(u.reshape(Sp, Bp, D_SSM), bb_blk, lb_re, lb_im, c_blk, ssm_d[0], wglu_b,
                           zeros_h, zeros_h, tc=64)
    y_prompt, cs_p = _ffn(x_prompt, o, z.reshape(Sp, Bp * D_SSM), ffn_w, tm=512)
    conv_prompt = cs_p[:, CARRY_ROWS - (CONV_W - 1):]

    zs, hre_s, him_s = _ssm(us.reshape(1, Bs, D_SSM), bb_blk, lb_re, lb_im, c_blk, ssm_d[0], wglu_b,
                            state_ssm_re[0].reshape(Bs, N_STATE), state_ssm_im[0].reshape(Bs, N_STATE), tc=1)
    sc = state_conv[0]
    y_s, a_s = _ffn(x_sample.reshape(1, Bs, D), os_, zs.reshape(Bs, D_SSM), ffn_w, tm=Bs, prev=(sc[:, 0], sc[:, 1]))
    conv_sample = jnp.stack([sc[:, 1], a_s], axis=1)

    st = lambda h, b: h.reshape(1, b, G, N)
    return (y_prompt, y_s.reshape(Bs, 1, D),
            k.reshape(1, Bp, Sp, N_ATT_HEADS, HEAD_W), v.reshape(1, Bp, Sp, N_ATT_HEADS, HEAD_W),
            st(hre_p, Bp), st(him_p, Bp), conv_prompt[None],
            ks.reshape(1, Bs, 1, N_ATT_HEADS, HEAD_W), vs.reshape(1, Bs, 1, N_ATT_HEADS, HEAD_W),
            st(hre_s, Bs), st(him_s, Bs), conv_sample[None])
```

```python
import functools
import math

import jax
import jax.numpy as jnp
import numpy as np
from jax import lax
from jax.experimental import pallas as pl
from jax.experimental.pallas import tpu as pltpu

F32 = jnp.float32
BF16 = jnp.bfloat16

D_MODEL = 1024
D_ATT = 512
D_SSM = 512
ATT_HEAD_DIM = 64
N_ATT_HEADS = 4
HEAD_W = 2 * ATT_HEAD_DIM
ROT_DIM = 16
ROPE_THETA = 500000.0
SSM_GROUP = 16
N_SSM_GROUPS = 32
SSM_STATE = 64
N_STATE = N_SSM_GROUPS * SSM_STATE
D_FF = 2816
CONV_W = 3
NORM_EPS = 1e-6
SUBLN_EPS = 1e-5
NEG_INF = -1e30
D_PROJ = 3 * D_ATT + D_SSM
LAMBDA_INIT = 0.8 - 0.6 * math.exp(-0.3 * 0)
LOG2_E = math.log2(math.e)

V7X_VMEM_LIMIT = 56 * 1024 * 1024
FF_CHUNK = 256
N_FF_CHUNKS = D_FF // FF_CHUNK
DECODE_GROUPS = 4
CARRY_ROWS = 8


def _params(*sem):
    return pltpu.CompilerParams(dimension_semantics=sem, vmem_limit_bytes=V7X_VMEM_LIMIT)


def _lambda(lq1_ref, lk1_ref, lq2_ref, lk2_ref):
    s1 = jnp.sum(lq1_ref[...] * lk1_ref[...], axis=-1, keepdims=True)
    s2 = jnp.sum(lq2_ref[...] * lk2_ref[...], axis=-1, keepdims=True)
    return jnp.exp(s1) - jnp.exp(s2) + LAMBDA_INIT


def _sub_ln(o, g):
    r = lax.rsqrt(jnp.mean(o * o, axis=-1, keepdims=True) + SUBLN_EPS)
    return (o * r * g) * (1.0 - LAMBDA_INIT)


def _ssm_prep_kernel(ar_ref, ai_ref, ldt_ref, art_ref, ait_ref, brt_ref, bit_ref,
                     lbre_ref, lbim_ref, bbre_ref, bbim_ref):
    dt = jnp.exp(ldt_ref[...])

    def disc(ar, ai):
        mag = jnp.exp(ar * dt)
        return mag * jnp.cos(ai * dt), mag * jnp.sin(ai * dt)

    lb_re, lb_im = disc(ar_ref[...], ai_ref[...])
    lbre_ref[...] = lb_re
    lbim_ref[...] = lb_im
    ar, ai = art_ref[...], ait_ref[...]
    t_re, t_im = disc(ar, ai)
    den = ar * ar + ai * ai
    nr, ni = t_re - 1.0, t_im
    f_re = (nr * ar + ni * ai) / den
    f_im = (ni * ar - nr * ai) / den
    br, bi = brt_ref[...], bit_ref[...]
    bbre_ref[...] = f_re * br - f_im * bi
    bbim_ref[...] = f_re * bi + f_im * br


def _ssm_prep(a_re, a_im, log_dt, b_re, b_im):
    G, N, C = N_SSM_GROUPS, SSM_STATE, SSM_GROUP
    tile = lambda a: jnp.tile(a, (1, C))
    flat = lambda b: jnp.transpose(b, (0, 2, 1)).reshape(G, C * N)
    outs = pl.pallas_call(
        _ssm_prep_kernel,
        out_shape=[jax.ShapeDtypeStruct((G, N), F32)] * 2 + [jax.ShapeDtypeStruct((G, C * N), F32)] * 2,
        name="ssm_prep",
    )(a_re, a_im, log_dt.reshape(G, 1), tile(a_re), tile(a_im), flat(b_re), flat(b_im))
    return outs


def _in_proj_kernel(x_ref, g_ref, w_ref, cos_ref, sa_ref, sb_ref, q_ref, k_ref, v_ref, u_ref):
    tm = x_ref.shape[0]
    x = x_ref[...]
    r = lax.rsqrt(jnp.mean(x * x, axis=-1, keepdims=True) + NORM_EPS)
    hn = (x * r * g_ref[...]).astype(BF16)
    proj = jnp.dot(hn, w_ref[...], preferred_element_type=F32)
    cos, sa, sb = cos_ref[...], sa_ref[...], sb_ref[...]
    half = ROT_DIM // 2

    def rope(xh):
        return xh * cos + pltpu.roll(xh, HEAD_W - half, 1) * sa + pltpu.roll(xh, half, 1) * sb

    for h in range(N_ATT_HEADS):
        lo, hi = h * HEAD_W, (h + 1) * HEAD_W
        q_ref[:, lo:hi] = rope(proj[:, lo:hi]) * (ATT_HEAD_DIM ** -0.5)
        k_ref[pl.ds(h, tm, stride=N_ATT_HEADS), :] = rope(proj[:, D_ATT + lo:D_ATT + hi])
        v_ref[pl.ds(h, tm, stride=N_ATT_HEADS), :] = proj[:, 2 * D_ATT + lo:2 * D_ATT + hi]
    u_ref[...] = proj[:, 3 * D_ATT:]


def _rope_tables(pos):
    half = ROT_DIM // 2
    inv = ROPE_THETA ** (-np.arange(0, ROT_DIM, 2, dtype=np.float64) / ROT_DIM)
    ang = np.asarray(pos, np.float64)[:, None] * inv[None, :]
    cos, sin = np.cos(ang), np.sin(ang)
    S = ang.shape[0]
    pad = np.zeros((S, ATT_HEAD_DIM - ROT_DIM))
    zh = np.zeros((S, half))
    comp = lambda a, b, c: np.concatenate([a, b, c], axis=1)
    two = lambda t: jnp.asarray(np.concatenate([t, t], axis=1), F32)
    return (two(comp(cos, cos, pad + 1.0)), two(comp(-sin, zh, pad)), two(comp(zh, sin, pad)))


def _in_proj(x, g, w_bf16, tables, tm):
    B, S, D = x.shape
    ns = S // tm
    row = lambda b, s: (b * ns + s, 0)
    tab = pl.BlockSpec((tm, HEAD_W), lambda b, s: (s, 0))
    q_spec = pl.BlockSpec((tm, D_ATT), row)
    kv_spec = pl.BlockSpec((tm * N_ATT_HEADS, HEAD_W), row)
    kv_shape = jax.ShapeDtypeStruct((B * S * N_ATT_HEADS, HEAD_W), F32)
    q, k, v, u = pl.pallas_call(
        _in_proj_kernel,
        grid=(B, ns),
        in_specs=[pl.BlockSpec((None, tm, D), lambda b, s: (b, s, 0)),
                  pl.BlockSpec((1, D), lambda b, s: (0, 0)),
                  pl.BlockSpec((D, D_PROJ), lambda b, s: (0, 0)),
                  tab, tab, tab],
        out_specs=[q_spec, kv_spec, kv_spec, pl.BlockSpec((tm, D_SSM), lambda b, s: (s, b))],
        out_shape=[jax.ShapeDtypeStruct((B * S, D_ATT), F32), kv_shape, kv_shape,
                   jax.ShapeDtypeStruct((S, B * D_SSM), F32)],
        compiler_params=_params("parallel", "parallel"),
        name="in_proj",
    )(x, g.reshape(1, D), w_bf16, *tables)
    return q, k, v, u


def _prompt_tile(i, lam, g_col, q_ref, k_ref, v_ref, o_ref, tq):
    nh = N_ATT_HEADS
    feat = lax.broadcasted_iota(jnp.int32, (HEAD_W, tq), 0)
    units = [(h, c) for h in range(nh) for c in range(2)]
    hs = lambda h: slice(h * HEAD_W, (h + 1) * HEAD_W)
    q_t = []
    for h, c in units:
        qh_t = (q_ref[:, hs(h)] * LOG2_E).T
        own = (feat < ATT_HEAD_DIM) if c == 0 else (feat >= ATT_HEAD_DIM)
        q_t.append(jnp.where(own, qh_t, 0.0))

    ones_rows = jnp.ones((8, tq), F32)

    n = range(len(units))
    rows = lambda j, h: pl.ds(pl.multiple_of(j * tq, tq) * nh + h, tq, stride=nh)

    def scores(j, masked):
        s = [jnp.dot(k_ref[rows(j, h), :], q_t[u], preferred_element_type=F32)
             for u, (h, c) in enumerate(units)]
        if masked:
            key = lax.broadcasted_iota(jnp.int32, (tq, tq), 0)
            qry = lax.broadcasted_iota(jnp.int32, (tq, tq), 1)
            s = [jnp.where(key <= qry, su, NEG_INF) for su in s]
        return s

    def softmax(s, carry):
        m_new = [jnp.maximum(carry[u][0], jnp.max(s[u], axis=0, keepdims=True)) for u in n]
        alpha = [jnp.exp2(carry[u][0] - m_new[u]) for u in n]
        return m_new, alpha, [jnp.exp2(s[u] - m_new[u]) for u in n]

    def update(j, carry, m_new, alpha, p):
        pv = [jnp.dot(jnp.concatenate([v_ref[rows(j, h), :].T, ones_rows], axis=0), p[u],
                      preferred_element_type=F32) for u, (h, c) in enumerate(units)]
        return tuple((m_new[u], alpha[u] * carry[u][1] + pv[u][HEAD_W:HEAD_W + 1],
                      alpha[u] * carry[u][2] + pv[u][:HEAD_W]) for u in n)

    init = tuple((jnp.full((1, tq), NEG_INF, F32), jnp.zeros((1, tq), F32), jnp.zeros((HEAD_W, tq), F32))
                 for _ in units)
    carry = lax.fori_loop(0, i, lambda j, c: update(j, c, *softmax(scores(j, False), c)), init)
    yield
    s = scores(i, True)
    yield
    stats = softmax(s, carry)
    yield
    carry = update(i, carry, *stats)
    for h in range(nh):
        (_, l1, acc1), (_, l2, acc2) = carry[2 * h], carry[2 * h + 1]
        o_t = acc1 / l1 - lam * (acc2 / l2)
        r = lax.rsqrt(jnp.mean(o_t * o_t, axis=0, keepdims=True) + SUBLN_EPS)
        o_ref[:, h * HEAD_W:(h + 1) * HEAD_W] = ((o_t * r * g_col) * (1.0 - LAMBDA_INIT)).T


def _decode_step(j, n_steps, lam, g_row, q_ref, kn_ref, vn_ref, k_refs, v_refs, o_ref, m_scr, l_scr, acc_scr):
    H = N_ATT_HEADS

    @pl.when(j == 0)
    def _():
        m_scr[...] = jnp.full(m_scr.shape, NEG_INF, F32)
        l_scr[...] = jnp.zeros(l_scr.shape, F32)
        acc_scr[...] = jnp.zeros(acc_scr.shape, F32)

    q = q_ref[...]
    lane = lax.broadcasted_iota(jnp.int32, q.shape, 1)
    q8 = jnp.concatenate([jnp.where(lane < ATT_HEAD_DIM, q, 0.0), jnp.where(lane >= ATT_HEAD_DIM, q, 0.0)], axis=0)
    rows = k_refs[0].shape[0]
    zero = jnp.zeros_like(q8)
    pad = jnp.zeros((HEAD_W - 4 * H, 2 * HEAD_W), F32)
    q_pair = jnp.concatenate([jnp.concatenate([q8, zero], axis=1), jnp.concatenate([zero, q8], axis=1), pad], axis=0)
    r = lax.broadcasted_iota(jnp.int32, (4 * H, rows), 0)
    col = lax.broadcasted_iota(jnp.int32, (4 * H, rows), 1)
    own = (col % H) == (r % H)
    both = lambda x: jnp.concatenate([x, x], axis=0)
    halves = lambda x, op: op(x[:2 * H], x[2 * H:])

    def scores(group):
        k_pairs = jnp.concatenate([jnp.concatenate([k_refs[a][...], k_refs[a + 1][...]], axis=1) for a in group],
                                  axis=0)
        s_t = lax.dot_general(k_pairs, q_pair, (((1,), (1,)), ((), ())), preferred_element_type=F32)
        return jnp.concatenate([jnp.where(own, s_t[n * rows:(n + 1) * rows].T[:4 * H], NEG_INF)
                                for n in range(len(group))], axis=1)

    def weighted_values(group, p):
        pv = None
        for n, a in enumerate(group):
            blk = slice(n * rows, (n + 1) * rows)
            part = (jnp.dot(p[:2 * H, blk], v_refs[a][...], preferred_element_type=F32)
                    + jnp.dot(p[2 * H:, blk], v_refs[a + 1][...], preferred_element_type=F32))
            pv = part if pv is None else pv + part
        return pv

    pairs = list(range(0, len(k_refs), 2))
    groups = [pairs[g::DECODE_GROUPS] for g in range(DECODE_GROUPS)]
    s = [scores(g) for g in groups]
    yield
    m_g = [halves(jnp.max(sg, axis=-1, keepdims=True), jnp.maximum) for sg in s]
    p = [jnp.exp(sg - both(mg)) for sg, mg in zip(s, m_g)]
    l_g = [halves(jnp.sum(pg, axis=-1, keepdims=True), jnp.add) for pg in p]
    yield
    pv_g = [weighted_values(g, pg) for g, pg in zip(groups, p)]
    m = m_scr[...]
    m_new = functools.reduce(jnp.maximum, m_g, m)
    alpha = jnp.exp(m - m_new)
    l, acc = alpha * l_scr[...], alpha * acc_scr[...]
    for mg, lg, pvg in zip(m_g, l_g, pv_g):
        w = jnp.exp(mg - m_new)
        l, acc = l + w * lg, acc + w * pvg
    m_scr[...] = m_new
    l_scr[...] = l
    acc_scr[...] = acc

    @pl.when(j == n_steps - 1)
    def _():
        kn = jnp.concatenate([kn_ref[...], kn_ref[...]], axis=0)
        vn = jnp.concatenate([vn_ref[...], vn_ref[...]], axis=0)
        s_new = jnp.sum(q8 * kn, axis=-1, keepdims=True)
        m_old = m_scr[...]
        m_fin = jnp.maximum(m_old, s_new)
        a = jnp.exp(m_old - m_fin)
        p_new = jnp.exp(s_new - m_fin)
        l = a * l_scr[...] + p_new
        acc = (a * acc_scr[...] + p_new * vn) / l
        o_ref[...] = _sub_ln(acc[:H] - lam * acc[H:], g_row)


def _attn_kernel(pt_ref, lq1_ref, lk1_ref, lq2_ref, lk2_ref, gcol_ref, grow_ref, q_ref, k_ref, v_ref,
                 qs_ref, kn_ref, vn_ref, ck_hbm, cv_hbm, o_ref, os_ref, m_scr, l_scr, acc_scr, kbuf, vbuf, sem,
                 *, tq, steps_per_seq, pages):
    i = pl.program_id(1)
    step = pl.program_id(0) * pl.num_programs(1) + i
    n_steps = pl.num_programs(0) * pl.num_programs(1)

    def page_copies(of_step, p):
        slot = of_step % 2
        pid = pt_ref[of_step // steps_per_seq, (of_step % steps_per_seq) * pages + p]
        return (pltpu.make_async_copy(ck_hbm.at[pid], kbuf.at[slot, p], sem.at[slot, 0]),
                pltpu.make_async_copy(cv_hbm.at[pid], vbuf.at[slot, p], sem.at[slot, 1]))

    def fetch(of_step, do, unrolled):
        def one(p):
            for c in page_copies(of_step, p):
                do(c)
        if unrolled:
            for p in range(pages):
                one(p)
        else:
            lax.fori_loop(0, pages, lambda p, _: (one(p), 0)[1], 0)

    start, wait = (lambda c: c.start()), (lambda c: c.wait())

    @pl.when(step == 0)
    def _():
        fetch(step, start, unrolled=False)

    following = lax.rem(step + 1, n_steps)
    fetch(following, start, unrolled=True)
    lam = _lambda(lq1_ref, lk1_ref, lq2_ref, lk2_ref)
    prompt = _prompt_tile(i, lam, gcol_ref[...], q_ref, k_ref, v_ref, o_ref, tq)
    next(prompt)
    fetch(step, wait, unrolled=True)
    slot = step % 2
    decode = _decode_step(step % steps_per_seq, steps_per_seq, lam, grow_ref[...], qs_ref, kn_ref, vn_ref,
                          [kbuf.at[slot, p] for p in range(pages)], [vbuf.at[slot, p] for p in range(pages)],
                          os_ref, m_scr, l_scr, acc_scr)
    running = [prompt, decode]
    while running:
        for gen in list(running):
            if next(gen, StopIteration) is StopIteration:
                running.remove(gen)

    @pl.when(step == n_steps - 1)
    def _():
        fetch(following, wait, unrolled=False)


def _attn(q, k, v, qs, ks, vs, cache_k, cache_v, page_table, lam_vecs, g, B, S, tq):
    H = N_ATT_HEADS
    nq = S // tq
    Bs, n_pages = page_table.shape
    n_pool, page = cache_k.shape[0], cache_k.shape[1]
    steps_per_seq = (B * nq) // Bs
    assert steps_per_seq * Bs == B * nq and n_pages % steps_per_seq == 0
    P = n_pages // steps_per_seq
    ck = cache_k.reshape(n_pool, page * H, HEAD_W)
    cv = cache_v.reshape(n_pool, page * H, HEAD_W)
    seq = lambda b, i: (b * nq + i) // steps_per_seq
    vec = pl.BlockSpec((1, ATT_HEAD_DIM), lambda b, i, pt: (0, 0))
    kv = pl.BlockSpec((S * H, HEAD_W), lambda b, i, pt: (b, 0))
    qo = pl.BlockSpec((tq, D_ATT), lambda b, i, pt: (b * nq + i, 0))
    tok = pl.BlockSpec((None, H, HEAD_W), lambda b, i, pt: (seq(b, i), 0, 0))
    hbm = pl.BlockSpec(memory_space=pl.ANY)
    page_buf = pltpu.VMEM((2, P, page * H, HEAD_W), F32)
    grid_spec = pltpu.PrefetchScalarGridSpec(
        num_scalar_prefetch=1,
        grid=(B, nq),
        in_specs=[vec, vec, vec, vec, pl.BlockSpec((HEAD_W, 1), lambda b, i, pt: (0, 0)),
                  pl.BlockSpec((1, HEAD_W), lambda b, i, pt: (0, 0)), qo, kv, kv, tok, tok, tok, hbm, hbm],
        out_specs=[qo, tok],
        scratch_shapes=[pltpu.VMEM((2 * H, 1), F32), pltpu.VMEM((2 * H, 1), F32), pltpu.VMEM((2 * H, HEAD_W), F32),
                        page_buf, page_buf, pltpu.SemaphoreType.DMA((2, 2))],
    )
    tok3 = lambda a: a.reshape(Bs, H, HEAD_W)
    o, o_s = pl.pallas_call(
        functools.partial(_attn_kernel, tq=tq, steps_per_seq=steps_per_seq, pages=P),
        grid_spec=grid_spec,
        out_shape=[jax.ShapeDtypeStruct((B * S, D_ATT), F32), jax.ShapeDtypeStruct((Bs, H, HEAD_W), F32)],
        compiler_params=_params("arbitrary", "arbitrary"),
        name="attn",
    )(page_table, *lam_vecs, g.reshape(HEAD_W, 1), g.reshape(1, HEAD_W), q, k, v, tok3(qs), tok3(ks), tok3(vs), ck, cv)
    return o, o_s.reshape(Bs, D_ATT)


SCAN_COLS = 1024
B_GROUPS = 16
C_GROUPS = 8


def _ssm_kernel(u_ref, bb_ref, lre_ref, lim_ref, c_ref, d_ref, wglu_ref, h0re_ref, h0im_ref,
                z_ref, hre_ref, him_ref, bu_scr, hs_scr, *, tc, bk):
    t0 = pl.program_id(0)

    @pl.when(t0 == 0)
    def _():
        hre_ref[...] = h0re_ref[...]
        him_ref[...] = h0im_ref[...]

    rows = tc * bk
    u = u_ref[...].reshape(rows, D_SSM)
    ub = u.astype(BF16)
    part_w = B_GROUPS * SSM_STATE
    half_w = 2 * part_w
    for half in range(N_SSM_GROUPS // B_GROUPS):
        u_half = ub[:, half * B_GROUPS * SSM_GROUP:(half + 1) * B_GROUPS * SSM_GROUP]
        bu = jnp.dot(u_half, bb_ref[half], preferred_element_type=F32)
        bu_scr[:, :, half * half_w:(half + 1) * half_w] = bu.reshape(tc, bk, half_w)

    for half in range(N_SSM_GROUPS // B_GROUPS):
        for cg in range(part_w // SCAN_COLS):
            st = slice(half * part_w + cg * SCAN_COLS, half * part_w + (cg + 1) * SCAN_COLS)
            re = slice(half * half_w + cg * SCAN_COLS, half * half_w + (cg + 1) * SCAN_COLS)
            im = slice(re.start + part_w, re.stop + part_w)
            lr, li = lre_ref[:, st], lim_ref[:, st]

            def step(t, carry, lr=lr, li=li, re=re, im=im):
                hr, hi = carry
                nhr = lr * hr - li * hi + bu_scr[t, :, re]
                nhi = lr * hi + li * hr + bu_scr[t, :, im]
                hs_scr[t, :, re] = nhr
                hs_scr[t, :, im] = nhi
                return nhr, nhi

            hr, hi = lax.fori_loop(0, tc, step, (hre_ref[:, st], him_ref[:, st]))
            hre_ref[:, st] = hr
            him_ref[:, st] = hi

    slab_w = C_GROUPS * SSM_STATE
    y_slabs = []
    for s in range(N_SSM_GROUPS // C_GROUPS):
        re0 = (s * C_GROUPS // B_GROUPS) * half_w + (s * C_GROUPS % B_GROUPS) * SSM_STATE
        h_re = hs_scr[:, :, re0:re0 + slab_w].reshape(rows, slab_w).astype(BF16)
        h_im = hs_scr[:, :, re0 + part_w:re0 + part_w + slab_w].reshape(rows, slab_w).astype(BF16)
        y_slabs.append(jnp.dot(h_re, c_ref[s, 0], preferred_element_type=F32)
                       + jnp.dot(h_im, c_ref[s, 1], preferred_element_type=F32))
    y = jnp.concatenate(y_slabs, axis=1) + d_ref[...] * u
    y = jax.nn.gelu(y)
    gate = jnp.dot(y.astype(BF16), wglu_ref[...], preferred_element_type=F32)
    z_ref[...] = (y * jax.nn.sigmoid(gate)).reshape(tc, bk, D_SSM)


def _ssm(u3, bb, lb_re, lb_im, c_full, d, w_glu, h0_re, h0_im, tc):
    S, bk, _ = u3.shape
    const = lambda shape: pl.BlockSpec(shape, lambda t: (0,) * len(shape))
    seq = pl.BlockSpec((tc, bk, D_SSM), lambda t: (t, 0, 0))
    rep = lambda a: jnp.broadcast_to(a.reshape(1, N_STATE), (bk, N_STATE))
    z, h_re, h_im = pl.pallas_call(
        functools.partial(_ssm_kernel, tc=tc, bk=bk),
        grid=(S // tc,),
        in_specs=[seq, const(bb.shape), const((bk, N_STATE)), const((bk, N_STATE)),
                  const(c_full.shape), const((1, D_SSM)), const((D_SSM, D_SSM)),
                  const((bk, N_STATE)), const((bk, N_STATE))],
        out_specs=[seq, const((bk, N_STATE)), const((bk, N_STATE))],
        out_shape=[jax.ShapeDtypeStruct((S, bk, D_SSM), F32), jax.ShapeDtypeStruct((bk, N_STATE), F32),
                   jax.ShapeDtypeStruct((bk, N_STATE), F32)],
        scratch_shapes=[pltpu.VMEM((tc, bk, 2 * N_STATE), F32), pltpu.VMEM((tc, bk, 2 * N_STATE), F32)],
        compiler_params=_params("arbitrary"),
        name="ssm",
    )(u3, bb, rep(lb_re), rep(lb_im), c_full, d.reshape(1, D_SSM), w_glu, h0_re, h0_im)
    return z, h_re, h_im


def _ffn_kernel(*refs, tm, decode):
    if decode:
        (x_ref, o_ref, z_ref, wo_ref, gn_ref, wu_ref, cw_ref, cb_ref, wd_ref, fn_ref, p2_ref, p1_ref,
         y_ref, a_out_ref, hid_scr) = refs
    else:
        (x_ref, o_ref, z_ref, wo_ref, gn_ref, wu_ref, cw_ref, cb_ref, wd_ref, fn_ref,
         y_ref, cs_ref, hid_scr, carry_scr) = refs

        @pl.when(pl.program_id(1) == 0)
        def _():
            carry_scr[...] = jnp.zeros(carry_scr.shape, F32)

    mix = (jnp.dot(o_ref[...].astype(BF16), wo_ref[:D_ATT, :], preferred_element_type=F32)
           + jnp.dot(z_ref[...].astype(BF16), wo_ref[D_ATT:, :], preferred_element_type=F32))
    x1 = x_ref[...] + mix
    r = lax.rsqrt(jnp.mean(x1 * x1, axis=-1, keepdims=True) + NORM_EPS)
    hn = (x1 * r * gn_ref[...]).astype(BF16)
    for j in range(N_FF_CHUNKS):
        cols = slice(j * FF_CHUNK, (j + 1) * FF_CHUNK)
        gcols = slice(D_FF + j * FF_CHUNK, D_FF + (j + 1) * FF_CHUNK)
        a = jnp.dot(hn, wu_ref[:, cols], preferred_element_type=F32)
        gate = jnp.dot(hn, wu_ref[:, gcols], preferred_element_type=F32)
        if decode:
            am2, am1 = p2_ref[:, cols], p1_ref[:, cols]
            a_out_ref[:, cols] = a
        else:
            prev = carry_scr[:, cols]
            p2 = jnp.broadcast_to(prev[CARRY_ROWS - 2:CARRY_ROWS - 1], a.shape)
            p1 = jnp.broadcast_to(prev[CARRY_ROWS - 1:], a.shape)
            row = lax.broadcasted_iota(jnp.int32, a.shape, 0)
            am1 = jnp.where(row < 1, p1, pltpu.roll(a, 1, 0))
            am2 = jnp.where(row < 1, p2, jnp.where(row < 2, p1, pltpu.roll(a, 2, 0)))
            carry_scr[:, cols] = a[tm - CARRY_ROWS:]
        cw = cw_ref[:, cols]
        c = cb_ref[:, cols] + (cw[0:1] * am2 + cw[1:2] * am1 + cw[2:3] * a)
        hid_scr[:, cols] = (jax.nn.silu(c) * gate).astype(BF16)
    acc = x1 + jnp.dot(hid_scr[...], wd_ref[...], preferred_element_type=F32)
    r2 = lax.rsqrt(jnp.mean(acc * acc, axis=-1, keepdims=True) + NORM_EPS)
    y_ref[...] = acc * r2 * fn_ref[...]
    if not decode:
        cs_ref[...] = carry_scr[...]


def _ffn(x, o, z, weights, tm, prev=None):
    B, S, D = x.shape
    ns = S // tm
    decode = prev is not None
    wo, gn, wu, cw, cb, wd, fn = weights
    const = lambda shape: pl.BlockSpec(shape, lambda b, s: (0,) * len(shape), pipeline_mode=pl.Buffered(1))
    xs = pl.BlockSpec((None, tm, D), lambda b, s: (b, s, 0))
    in_specs = [xs, pl.BlockSpec((tm, D_ATT), lambda b, s: (b * ns + s, 0)),
                pl.BlockSpec((tm, D_SSM), lambda b, s: (s, b)),
                const((D, D)), const((1, D)), const((D, 2 * D_FF)), const((CONV_W, D_FF)), const((1, D_FF)),
                const((D_FF, D)), const((1, D))]
    args = [x, o, z, wo, gn, wu, cw, cb, wd, fn]
    scratch = [pltpu.VMEM((tm, D_FF), BF16)]
    if decode:
        rows = pl.BlockSpec((tm, D_FF), lambda b, s: (0, 0))
        in_specs += [rows, rows]
        args += list(prev)
        out_specs = [xs, rows]
        out_shape = [jax.ShapeDtypeStruct((B, S, D), F32), jax.ShapeDtypeStruct((tm, D_FF), F32)]
    else:
        out_specs = [xs, pl.BlockSpec((None, CARRY_ROWS, D_FF), lambda b, s: (b, 0, 0))]
        out_shape = [jax.ShapeDtypeStruct((B, S, D), F32), jax.ShapeDtypeStruct((B, CARRY_ROWS, D_FF), F32)]
        scratch.append(pltpu.VMEM((CARRY_ROWS, D_FF), F32))
    y, extra = pl.pallas_call(
        functools.partial(_ffn_kernel, tm=tm, decode=decode),
        grid=(B, ns),
        in_specs=in_specs, out_specs=out_specs, out_shape=out_shape, scratch_shapes=scratch,
        compiler_params=_params("parallel", "arbitrary"),
        name="ffn_decode" if decode else "ffn",
    )(*args)
    return y, extra


def kernel(x_prompt, x_sample, cache_k, cache_v, state_ssm_re, state_ssm_im, state_conv, page_table, norm_mix, w_in, lambda_q1, lambda_k1, lambda_q2, lambda_k2, subln_g, ssm_a_re, ssm_a_im, ssm_log_dt, ssm_b_re, ssm_b_im, ssm_c_re, ssm_c_im, ssm_d, w_glu, w_o, norm_ffn, w_up, conv_w, conv_b, w_down, final_norm):
    assert norm_mix.shape[0] == 1, "single layer"
    Bp, Sp, D = x_prompt.shape
    Bs = x_sample.shape[0]
    G, N, C = N_SSM_GROUPS, SSM_STATE, SSM_GROUP
    past_len = page_table.shape[1] * cache_k.shape[2]

    w_in_b = w_in[0].astype(BF16)
    wo_b = w_o[0].astype(BF16)
    wglu_b = w_glu[0].astype(BF16)
    ffn_w = (wo_b, norm_ffn[0].reshape(1, D), w_up[0].astype(BF16), conv_w[0], conv_b[0].reshape(1, D_FF),
             w_down[0].astype(BF16), final_norm.reshape(1, D))
    lam_vecs = [v.reshape(1, ATT_HEAD_DIM) for v in (lambda_q1[0], lambda_k1[0], lambda_q2[0], lambda_k2[0])]
    g_sub = subln_g[0].reshape(1, HEAD_W)

    lb_re, lb_im, bbt_re, bbt_im = _ssm_prep(ssm_a_re[0], ssm_a_im[0], ssm_log_dt[0], ssm_b_re[0], ssm_b_im[0])
    eye_b = jnp.eye(B_GROUPS, dtype=F32)[None, :, None, :, None]
    place_b = lambda bbt: (bbt.reshape(G // B_GROUPS, B_GROUPS, C, 1, N) * eye_b).reshape(
        G // B_GROUPS, B_GROUPS * C, B_GROUPS * N)
    bb_blk = jnp.concatenate([place_b(bbt_re), place_b(bbt_im)], axis=2).astype(BF16)
    eye_c = jnp.eye(C_GROUPS, dtype=F32)[None, :, None, :, None]
    place_c = lambda c: (jnp.transpose(c, (0, 2, 1)).reshape(G // C_GROUPS, C_GROUPS, N, 1, C) * eye_c).reshape(
        G // C_GROUPS, C_GROUPS * N, C_GROUPS * C)
    c_blk = jnp.stack([place_c(ssm_c_re[0]), -place_c(ssm_c_im[0])], axis=1).astype(BF16)

    tabs_p = _rope_tables(np.arange(Sp))
    q, k, v, u = _in_proj(x_prompt, norm_mix[0], w_in_b, tabs_p, tm=512)
    tabs_s = _rope_tables(np.full((Bs,), past_len))
    qs, ks, vs, us = _in_proj(x_sample.reshape(1, Bs, D), norm_mix[0], w_in_b, tabs_s, tm=Bs)

    o, os_ = _attn(q, k, v, qs, ks, vs, cache_k[0], cache_v[0], page_table, lam_vecs, g_sub, Bp, Sp, tq=256)

    zeros_h = jnp.zeros((Bp, N_STATE), F32)
    z, hre_p, him_p = _ssm(u.reshape(Sp, Bp, D_SSM), bb_blk, lb_re, lb_im, c_blk, ssm_d[0], wglu_b,
                           zeros_h, zeros_h, tc=64)
    y_prompt, cs_p = _ffn(x_prompt, o, z.reshape(Sp, Bp * D_SSM), ffn_w, tm=512)
    conv_prompt = cs_p[:, CARRY_ROWS - (CONV_W - 1):]

    zs, hre_s, him_s = _ssm(us.reshape(1, Bs, D_SSM), bb_blk, lb_re, lb_im, c_blk, ssm_d[0], wglu_b,
                            state_ssm_re[0].reshape(Bs, N_STATE), state_ssm_im[0].reshape(Bs, N_STATE), tc=1)
    sc = state_conv[0]
    y_s, a_s = _ffn(x_sample.reshape(1, Bs, D), os_, zs.reshape(Bs, D_SSM), ffn_w, tm=Bs, prev=(sc[:, 0], sc[:, 1]))
    conv_sample = jnp.stack([sc[:, 1], a_s], axis=1)

    st = lambda h, b: h.reshape(1, b, G, N)
    return (y_prompt, y_s.reshape(Bs, 1, D),
            k.reshape(1, Bp, Sp, N_ATT_HEADS, HEAD_W), v.reshape(1, Bp, Sp, N_ATT_HEADS, HEAD_W),
            st(hre_p, Bp), st(him_p, Bp), conv_prompt[None],
            ks.reshape(1, Bs, 1, N_ATT_HEADS, HEAD_W), vs.reshape(1, Bs, 1, N_ATT_HEADS, HEAD_W),
            st(hre_s, Bs), st(him_s, Bs), conv_sample[None])
```

```python
import functools
import math

import jax
import jax.numpy as jnp
import numpy as np
from jax import lax
from jax.experimental import pallas as pl
from jax.experimental.pallas import tpu as pltpu

F32 = jnp.float32
BF16 = jnp.bfloat16

D_MODEL = 1024
D_ATT = 512
D_SSM = 512
ATT_HEAD_DIM = 64
N_ATT_HEADS = 4
HEAD_W = 2 * ATT_HEAD_DIM
ROT_DIM = 16
ROPE_THETA = 500000.0
SSM_GROUP = 16
N_SSM_GROUPS = 32
SSM_STATE = 64
N_STATE = N_SSM_GROUPS * SSM_STATE
D_FF = 2816
CONV_W = 3
NORM_EPS = 1e-6
SUBLN_EPS = 1e-5
NEG_INF = -1e30
D_PROJ = 3 * D_ATT + D_SSM
LAMBDA_INIT = 0.8 - 0.6 * math.exp(-0.3 * 0)
LOG2_E = math.log2(math.e)

V7X_VMEM_LIMIT = 56 * 1024 * 1024
IN_PROJ_ROWS = 1024
ATTN_Q_ROWS = 256
SSM_STEPS = 128
FFN_ROWS = 512
FF_CHUNK = 256
N_FF_CHUNKS = D_FF // FF_CHUNK
DECODE_GROUPS = 4
CARRY_ROWS = 8


def _params(*sem):
    return pltpu.CompilerParams(dimension_semantics=sem, vmem_limit_bytes=V7X_VMEM_LIMIT)


def _lambda(lq1_ref, lk1_ref, lq2_ref, lk2_ref):
    s1 = jnp.sum(lq1_ref[...] * lk1_ref[...], axis=-1, keepdims=True)
    s2 = jnp.sum(lq2_ref[...] * lk2_ref[...], axis=-1, keepdims=True)
    return jnp.exp(s1) - jnp.exp(s2) + LAMBDA_INIT


def _sub_ln(o, g):
    r = lax.rsqrt(jnp.mean(o * o, axis=-1, keepdims=True) + SUBLN_EPS)
    return (o * r * g) * (1.0 - LAMBDA_INIT)


def _ssm_prep_kernel(ar_ref, ai_ref, ldt_ref, art_ref, ait_ref, brt_ref, bit_ref,
                     lbre_ref, lbim_ref, bbre_ref, bbim_ref):
    dt = jnp.exp(ldt_ref[...])

    def disc(ar, ai):
        mag = jnp.exp(ar * dt)
        return mag * jnp.cos(ai * dt), mag * jnp.sin(ai * dt)

    lb_re, lb_im = disc(ar_ref[...], ai_ref[...])
    lbre_ref[...] = lb_re
    lbim_ref[...] = lb_im
    ar, ai = art_ref[...], ait_ref[...]
    t_re, t_im = disc(ar, ai)
    den = ar * ar + ai * ai
    nr, ni = t_re - 1.0, t_im
    f_re = (nr * ar + ni * ai) / den
    f_im = (ni * ar - nr * ai) / den
    br, bi = brt_ref[...], bit_ref[...]
    bbre_ref[...] = f_re * br - f_im * bi
    bbim_ref[...] = f_re * bi + f_im * br


def _ssm_prep(a_re, a_im, log_dt, b_re, b_im):
    G, N, C = N_SSM_GROUPS, SSM_STATE, SSM_GROUP
    tile = lambda a: jnp.tile(a, (1, C))
    flat = lambda b: jnp.transpose(b, (0, 2, 1)).reshape(G, C * N)
    outs = pl.pallas_call(
        _ssm_prep_kernel,
        out_shape=[jax.ShapeDtypeStruct((G, N), F32)] * 2 + [jax.ShapeDtypeStruct((G, C * N), F32)] * 2,
        name="ssm_prep",
    )(a_re, a_im, log_dt.reshape(G, 1), tile(a_re), tile(a_im), flat(b_re), flat(b_im))
    return outs


def _in_proj_kernel(x_ref, g_ref, w_ref, cos_ref, sa_ref, sb_ref, q_ref, k_ref, v_ref, u_ref):
    tm = x_ref.shape[0]
    x = x_ref[...]
    r = lax.rsqrt(jnp.mean(x * x, axis=-1, keepdims=True) + NORM_EPS)
    hn = (x * r * g_ref[...]).astype(BF16)
    proj = jnp.dot(hn, w_ref[...], preferred_element_type=F32)
    cos, sa, sb = cos_ref[...], sa_ref[...], sb_ref[...]
    half = ROT_DIM // 2

    def rope(xh):
        return xh * cos + pltpu.roll(xh, HEAD_W - half, 1) * sa + pltpu.roll(xh, half, 1) * sb

    for h in range(N_ATT_HEADS):
        lo, hi = h * HEAD_W, (h + 1) * HEAD_W
        q_ref[:, lo:hi] = rope(proj[:, lo:hi]) * (ATT_HEAD_DIM ** -0.5)
        k_ref[pl.ds(h, tm, stride=N_ATT_HEADS), :] = rope(proj[:, D_ATT + lo:D_ATT + hi])
        v_ref[pl.ds(h, tm, stride=N_ATT_HEADS), :] = proj[:, 2 * D_ATT + lo:2 * D_ATT + hi]
    u_ref[...] = proj[:, 3 * D_ATT:]


def _rope_tables(pos):
    half = ROT_DIM // 2
    inv = ROPE_THETA ** (-np.arange(0, ROT_DIM, 2, dtype=np.float64) / ROT_DIM)
    ang = np.asarray(pos, np.float64)[:, None] * inv[None, :]
    cos, sin = np.cos(ang), np.sin(ang)
    S = ang.shape[0]
    pad = np.zeros((S, ATT_HEAD_DIM - ROT_DIM))
    zh = np.zeros((S, half))
    comp = lambda a, b, c: np.concatenate([a, b, c], axis=1)
    two = lambda t: jnp.asarray(np.concatenate([t, t], axis=1), F32)
    return (two(comp(cos, cos, pad + 1.0)), two(comp(-sin, zh, pad)), two(comp(zh, sin, pad)))


def _in_proj(x, g, w_bf16, tables, tm):
    B, S, D = x.shape
    ns = S // tm
    row = lambda b, s: (b * ns + s, 0)
    tab = pl.BlockSpec((tm, HEAD_W), lambda b, s: (s, 0))
    q_spec = pl.BlockSpec((tm, D_ATT), row)
    kv_spec = pl.BlockSpec((tm * N_ATT_HEADS, HEAD_W), row)
    kv_shape = jax.ShapeDtypeStruct((B * S * N_ATT_HEADS, HEAD_W), F32)
    q, k, v, u = pl.pallas_call(
        _in_proj_kernel,
        grid=(B, ns),
        in_specs=[pl.BlockSpec((None, tm, D), lambda b, s: (b, s, 0)),
                  pl.BlockSpec((1, D), lambda b, s: (0, 0)),
                  pl.BlockSpec((D, D_PROJ), lambda b, s: (0, 0)),
                  tab, tab, tab],
        out_specs=[q_spec, kv_spec, kv_spec, pl.BlockSpec((tm, D_SSM), lambda b, s: (s, b))],
        out_shape=[jax.ShapeDtypeStruct((B * S, D_ATT), F32), kv_shape, kv_shape,
                   jax.ShapeDtypeStruct((S, B * D_SSM), F32)],
        compiler_params=_params("parallel", "parallel"),
        name="in_proj",
    )(x, g.reshape(1, D), w_bf16, *tables)
    return q, k, v, u


def _prompt_tile(i, lam, g_col, q_ref, k_ref, v_ref, o_ref, tq):
    nh = N_ATT_HEADS
    feat = lax.broadcasted_iota(jnp.int32, (HEAD_W, tq), 0)
    units = [(h, c) for h in range(nh) for c in range(2)]
    hs = lambda h: slice(h * HEAD_W, (h + 1) * HEAD_W)
    q_t = []
    for h, c in units:
        qh_t = (q_ref[:, hs(h)] * LOG2_E).T
        own = (feat < ATT_HEAD_DIM) if c == 0 else (feat >= ATT_HEAD_DIM)
        q_t.append(jnp.where(own, qh_t, 0.0))

    ones_rows = jnp.ones((8, tq), F32)

    n = range(len(units))
    rows = lambda j, h: pl.ds(pl.multiple_of(j * tq, tq) * nh + h, tq, stride=nh)

    def scores(j, masked):
        s = [jnp.dot(k_ref[rows(j, h), :], q_t[u], preferred_element_type=F32)
             for u, (h, c) in enumerate(units)]
        if masked:
            key = lax.broadcasted_iota(jnp.int32, (tq, tq), 0)
            qry = lax.broadcasted_iota(jnp.int32, (tq, tq), 1)
            s = [jnp.where(key <= qry, su, NEG_INF) for su in s]
        return s

    def softmax(s, carry):
        m_new = [jnp.maximum(carry[u][0], jnp.max(s[u], axis=0, keepdims=True)) for u in n]
        alpha = [jnp.exp2(carry[u][0] - m_new[u]) for u in n]
        return m_new, alpha, [jnp.exp2(s[u] - m_new[u]) for u in n]

    def update(j, carry, m_new, alpha, p):
        pv = [jnp.dot(jnp.concatenate([v_ref[rows(j, h), :].T, ones_rows], axis=0), p[u],
                      preferred_element_type=F32) for u, (h, c) in enumerate(units)]
        return tuple((m_new[u], alpha[u] * carry[u][1] + pv[u][HEAD_W:HEAD_W + 1],
                      alpha[u] * carry[u][2] + pv[u][:HEAD_W]) for u in n)

    init = tuple((jnp.full((1, tq), NEG_INF, F32), jnp.zeros((1, tq), F32), jnp.zeros((HEAD_W, tq), F32))
                 for _ in units)
    carry = lax.fori_loop(0, i, lambda j, c: update(j, c, *softmax(scores(j, False), c)), init)
    yield
    s = scores(i, True)
    yield
    stats = softmax(s, carry)
    yield
    carry = update(i, carry, *stats)
    for h in range(nh):
        (_, l1, acc1), (_, l2, acc2) = carry[2 * h], carry[2 * h + 1]
        o_t = acc1 / l1 - lam * (acc2 / l2)
        r = lax.rsqrt(jnp.mean(o_t * o_t, axis=0, keepdims=True) + SUBLN_EPS)
        o_ref[:, h * HEAD_W:(h + 1) * HEAD_W] = ((o_t * r * g_col) * (1.0 - LAMBDA_INIT)).T


def _decode_step(j, n_steps, lam, g_row, q_ref, kn_ref, vn_ref, k_refs, v_refs, o_ref, m_scr, l_scr, acc_scr):
    H = N_ATT_HEADS

    @pl.when(j == 0)
    def _():
        m_scr[...] = jnp.full(m_scr.shape, NEG_INF, F32)
        l_scr[...] = jnp.zeros(l_scr.shape, F32)
        acc_scr[...] = jnp.zeros(acc_scr.shape, F32)

    q = q_ref[...]
    lane = lax.broadcasted_iota(jnp.int32, q.shape, 1)
    q8 = jnp.concatenate([jnp.where(lane < ATT_HEAD_DIM, q, 0.0), jnp.where(lane >= ATT_HEAD_DIM, q, 0.0)], axis=0)
    rows = k_refs[0].shape[0]
    zero = jnp.zeros_like(q8)
    pad = jnp.zeros((HEAD_W - 4 * H, 2 * HEAD_W), F32)
    q_pair = jnp.concatenate([jnp.concatenate([q8, zero], axis=1), jnp.concatenate([zero, q8], axis=1), pad], axis=0)
    r = lax.broadcasted_iota(jnp.int32, (4 * H, rows), 0)
    col = lax.broadcasted_iota(jnp.int32, (4 * H, rows), 1)
    own = (col % H) == (r % H)
    both = lambda x: jnp.concatenate([x, x], axis=0)
    halves = lambda x, op: op(x[:2 * H], x[2 * H:])

    def scores(group):
        k_pairs = jnp.concatenate([jnp.concatenate([k_refs[a][...], k_refs[a + 1][...]], axis=1) for a in group],
                                  axis=0)
        s_t = lax.dot_general(k_pairs, q_pair, (((1,), (1,)), ((), ())), preferred_element_type=F32)
        return jnp.concatenate([jnp.where(own, s_t[n * rows:(n + 1) * rows].T[:4 * H], NEG_INF)
                                for n in range(len(group))], axis=1)

    def weighted_values(group, p):
        pv = None
        for n, a in enumerate(group):
            blk = slice(n * rows, (n + 1) * rows)
            part = (jnp.dot(p[:2 * H, blk], v_refs[a][...], preferred_element_type=F32)
                    + jnp.dot(p[2 * H:, blk], v_refs[a + 1][...], preferred_element_type=F32))
            pv = part if pv is None else pv + part
        return pv

    pairs = list(range(0, len(k_refs), 2))
    groups = [pairs[g::DECODE_GROUPS] for g in range(DECODE_GROUPS)]
    s = [scores(g) for g in groups]
    yield
    m_g = [halves(jnp.max(sg, axis=-1, keepdims=True), jnp.maximum) for sg in s]
    p = [jnp.exp(sg - both(mg)) for sg, mg in zip(s, m_g)]
    l_g = [halves(jnp.sum(pg, axis=-1, keepdims=True), jnp.add) for pg in p]
    yield
    pv_g = [weighted_values(g, pg) for g, pg in zip(groups, p)]
    m = m_scr[...]
    m_new = functools.reduce(jnp.maximum, m_g, m)
    alpha = jnp.exp(m - m_new)
    l, acc = alpha * l_scr[...], alpha * acc_scr[...]
    for mg, lg, pvg in zip(m_g, l_g, pv_g):
        w = jnp.exp(mg - m_new)
        l, acc = l + w * lg, acc + w * pvg
    m_scr[...] = m_new
    l_scr[...] = l
    acc_scr[...] = acc

    @pl.when(j == n_steps - 1)
    def _():
        kn = jnp.concatenate([kn_ref[...], kn_ref[...]], axis=0)
        vn = jnp.concatenate([vn_ref[...], vn_ref[...]], axis=0)
        s_new = jnp.sum(q8 * kn, axis=-1, keepdims=True)
        m_old = m_scr[...]
        m_fin = jnp.maximum(m_old, s_new)
        a = jnp.exp(m_old - m_fin)
        p_new = jnp.exp(s_new - m_fin)
        l = a * l_scr[...] + p_new
        acc = (a * acc_scr[...] + p_new * vn) / l
        o_ref[...] = _sub_ln(acc[:H] - lam * acc[H:], g_row)


def _attn_kernel(pt_ref, lq1_ref, lk1_ref, lq2_ref, lk2_ref, gcol_ref, grow_ref, q_ref, k_ref, v_ref,
                 qs_ref, kn_ref, vn_ref, ck_hbm, cv_hbm, o_ref, os_ref, m_scr, l_scr, acc_scr, kbuf, vbuf, sem,
                 *, tq, steps_per_seq, pages):
    i = pl.program_id(1)
    step = pl.program_id(0) * pl.num_programs(1) + i
    n_steps = pl.num_programs(0) * pl.num_programs(1)

    def page_copies(of_step, p):
        slot = of_step % 2
        pid = pt_ref[of_step // steps_per_seq, (of_step % steps_per_seq) * pages + p]
        return (pltpu.make_async_copy(ck_hbm.at[pid], kbuf.at[slot, p], sem.at[slot, 0]),
                pltpu.make_async_copy(cv_hbm.at[pid], vbuf.at[slot, p], sem.at[slot, 1]))

    def fetch(of_step, do, unrolled):
        def one(p):
            for c in page_copies(of_step, p):
                do(c)
        if unrolled:
            for p in range(pages):
                one(p)
        else:
            lax.fori_loop(0, pages, lambda p, _: (one(p), 0)[1], 0)

    start, wait = (lambda c: c.start()), (lambda c: c.wait())

    @pl.when(step == 0)
    def _():
        fetch(step, start, unrolled=False)

    following = lax.rem(step + 1, n_steps)
    fetch(following, start, unrolled=True)
    lam = _lambda(lq1_ref, lk1_ref, lq2_ref, lk2_ref)
    prompt = _prompt_tile(i, lam, gcol_ref[...], q_ref, k_ref, v_ref, o_ref, tq)
    next(prompt)
    fetch(step, wait, unrolled=True)
    slot = step % 2
    decode = _decode_step(step % steps_per_seq, steps_per_seq, lam, grow_ref[...], qs_ref, kn_ref, vn_ref,
                          [kbuf.at[slot, p] for p in range(pages)], [vbuf.at[slot, p] for p in range(pages)],
                          os_ref, m_scr, l_scr, acc_scr)
    running = [prompt, decode]
    while running:
        for gen in list(running):
            if next(gen, StopIteration) is StopIteration:
                running.remove(gen)

    @pl.when(step == n_steps - 1)
    def _():
        fetch(following, wait, unrolled=False)


def _attn(q, k, v, qs, ks, vs, cache_k, cache_v, page_table, lam_vecs, g, B, S, tq):
    H = N_ATT_HEADS
    nq = S // tq
    Bs, n_pages = page_table.shape
    n_pool, page = cache_k.shape[0], cache_k.shape[1]
    steps_per_seq = (B * nq) // Bs
    assert steps_per_seq * Bs == B * nq and n_pages % steps_per_seq == 0
    P = n_pages // steps_per_seq
    ck = cache_k.reshape(n_pool, page * H, HEAD_W)
    cv = cache_v.reshape(n_pool, page * H, HEAD_W)
    seq = lambda b, i: (b * nq + i) // steps_per_seq
    vec = pl.BlockSpec((1, ATT_HEAD_DIM), lambda b, i, pt: (0, 0))
    kv = pl.BlockSpec((S * H, HEAD_W), lambda b, i, pt: (b, 0))
    qo = pl.BlockSpec((tq, D_ATT), lambda b, i, pt: (b * nq + i, 0))
    tok = pl.BlockSpec((None, H, HEAD_W), lambda b, i, pt: (seq(b, i), 0, 0))
    hbm = pl.BlockSpec(memory_space=pl.ANY)
    page_buf = pltpu.VMEM((2, P, page * H, HEAD_W), F32)
    grid_spec = pltpu.PrefetchScalarGridSpec(
        num_scalar_prefetch=1,
        grid=(B, nq),
        in_specs=[vec, vec, vec, vec, pl.BlockSpec((HEAD_W, 1), lambda b, i, pt: (0, 0)),
                  pl.BlockSpec((1, HEAD_W), lambda b, i, pt: (0, 0)), qo, kv, kv, tok, tok, tok, hbm, hbm],
        out_specs=[qo, tok],
        scratch_shapes=[pltpu.VMEM((2 * H, 1), F32), pltpu.VMEM((2 * H, 1), F32), pltpu.VMEM((2 * H, HEAD_W), F32),
                        page_buf, page_buf, pltpu.SemaphoreType.DMA((2, 2))],
    )
    tok3 = lambda a: a.reshape(Bs, H, HEAD_W)
    o, o_s = pl.pallas_call(
        functools.partial(_attn_kernel, tq=tq, steps_per_seq=steps_per_seq, pages=P),
        grid_spec=grid_spec,
        out_shape=[jax.ShapeDtypeStruct((B * S, D_ATT), F32), jax.ShapeDtypeStruct((Bs, H, HEAD_W), F32)],
        compiler_params=_params("arbitrary", "arbitrary"),
        name="attn",
    )(page_table, *lam_vecs, g.reshape(HEAD_W, 1), g.reshape(1, HEAD_W), q, k, v, tok3(qs), tok3(ks), tok3(vs), ck, cv)
    return o, o_s.reshape(Bs, D_ATT)


SCAN_COLS = 1024
B_GROUPS = 16
C_GROUPS = 8


def _ssm_kernel(u_ref, bb_ref, lre_ref, lim_ref, c_ref, d_ref, wglu_ref, h0re_ref, h0im_ref,
                z_ref, hre_ref, him_ref, bu_scr, hs_scr, *, tc, bk):
    t0 = pl.program_id(0)

    @pl.when(t0 == 0)
    def _():
        hre_ref[...] = h0re_ref[...]
        him_ref[...] = h0im_ref[...]

    rows = tc * bk
    u = u_ref[...].reshape(rows, D_SSM)
    ub = u.astype(BF16)
    part_w = B_GROUPS * SSM_STATE
    half_w = 2 * part_w
    for half in range(N_SSM_GROUPS // B_GROUPS):
        u_half = ub[:, half * B_GROUPS * SSM_GROUP:(half + 1) * B_GROUPS * SSM_GROUP]
        bu = jnp.dot(u_half, bb_ref[half], preferred_element_type=F32)
        bu_scr[:, :, half * half_w:(half + 1) * half_w] = bu.reshape(tc, bk, half_w)

    for half in range(N_SSM_GROUPS // B_GROUPS):
        for cg in range(part_w // SCAN_COLS):
            st = slice(half * part_w + cg * SCAN_COLS, half * part_w + (cg + 1) * SCAN_COLS)
            re = slice(half * half_w + cg * SCAN_COLS, half * half_w + (cg + 1) * SCAN_COLS)
            im = slice(re.start + part_w, re.stop + part_w)
            lr, li = lre_ref[:, st], lim_ref[:, st]

            def step(t, carry, lr=lr, li=li, re=re, im=im):
                hr, hi = carry
                nhr = lr * hr - li * hi + bu_scr[t, :, re]
                nhi = lr * hi + li * hr + bu_scr[t, :, im]
                hs_scr[t, :, re] = nhr
                hs_scr[t, :, im] = nhi
                return nhr, nhi

            hr, hi = lax.fori_loop(0, tc, step, (hre_ref[:, st], him_ref[:, st]))
            hre_ref[:, st] = hr
            him_ref[:, st] = hi

    slab_w = C_GROUPS * SSM_STATE
    y_slabs = []
    for s in range(N_SSM_GROUPS // C_GROUPS):
        re0 = (s * C_GROUPS // B_GROUPS) * half_w + (s * C_GROUPS % B_GROUPS) * SSM_STATE
        h_re = hs_scr[:, :, re0:re0 + slab_w].reshape(rows, slab_w).astype(BF16)
        h_im = hs_scr[:, :, re0 + part_w:re0 + part_w + slab_w].reshape(rows, slab_w).astype(BF16)
        y_slabs.append(jnp.dot(h_re, c_ref[s, 0], preferred_element_type=F32)
                       + jnp.dot(h_im, c_ref[s, 1], preferred_element_type=F32))
    y = jnp.concatenate(y_slabs, axis=1) + d_ref[...] * u
    y = jax.nn.gelu(y)
    gate = jnp.dot(y.astype(BF16), wglu_ref[...], preferred_element_type=F32)
    z_ref[...] = (y * jax.nn.sigmoid(gate)).reshape(tc, bk, D_SSM)


def _ssm(u3, bb, lb_re, lb_im, c_full, d, w_glu, h0_re, h0_im, tc):
    S, bk, _ = u3.shape
    const = lambda shape: pl.BlockSpec(shape, lambda t: (0,) * len(shape))
    seq = pl.BlockSpec((tc, bk, D_SSM), lambda t: (t, 0, 0))
    rep = lambda a: jnp.broadcast_to(a.reshape(1, N_STATE), (bk, N_STATE))
    z, h_re, h_im = pl.pallas_call(
        functools.partial(_ssm_kernel, tc=tc, bk=bk),
        grid=(S // tc,),
        in_specs=[seq, const(bb.shape), const((bk, N_STATE)), const((bk, N_STATE)),
                  const(c_full.shape), const((1, D_SSM)), const((D_SSM, D_SSM)),
                  const((bk, N_STATE)), const((bk, N_STATE))],
        out_specs=[seq, const((bk, N_STATE)), const((bk, N_STATE))],
        out_shape=[jax.ShapeDtypeStruct((S, bk, D_SSM), F32), jax.ShapeDtypeStruct((bk, N_STATE), F32),
                   jax.ShapeDtypeStruct((bk, N_STATE), F32)],
        scratch_shapes=[pltpu.VMEM((tc, bk, 2 * N_STATE), F32), pltpu.VMEM((tc, bk, 2 * N_STATE), F32)],
        compiler_params=_params("arbitrary"),
        name="ssm",
    )(u3, bb, rep(lb_re), rep(lb_im), c_full, d.reshape(1, D_SSM), w_glu, h0_re, h0_im)
    return z, h_re, h_im


def _ffn_kernel(*refs, tm, decode):
    if decode:
        (x_ref, o_ref, z_ref, wo_ref, gn_ref, wu_ref, cw_ref, cb_ref, wd_ref, fn_ref, p2_ref, p1_ref,
         y_ref, a_out_ref, hid_scr) = refs
    else:
        (x_ref, o_ref, z_ref, wo_ref, gn_ref, wu_ref, cw_ref, cb_ref, wd_ref, fn_ref,
         y_ref, cs_ref, hid_scr, carry_scr) = refs

        @pl.when(pl.program_id(1) == 0)
        def _():
            carry_scr[...] = jnp.zeros(carry_scr.shape, F32)

    mix = (jnp.dot(o_ref[...].astype(BF16), wo_ref[:D_ATT, :], preferred_element_type=F32)
           + jnp.dot(z_ref[...].astype(BF16), wo_ref[D_ATT:, :], preferred_element_type=F32))
    x1 = x_ref[...] + mix
    r = lax.rsqrt(jnp.mean(x1 * x1, axis=-1, keepdims=True) + NORM_EPS)
    hn = (x1 * r * gn_ref[...]).astype(BF16)
    for j in range(N_FF_CHUNKS):
        cols = slice(j * FF_CHUNK, (j + 1) * FF_CHUNK)
        gcols = slice(D_FF + j * FF_CHUNK, D_FF + (j + 1) * FF_CHUNK)
        a = jnp.dot(hn, wu_ref[:, cols], preferred_element_type=F32)
        gate = jnp.dot(hn, wu_ref[:, gcols], preferred_element_type=F32)
        if decode:
            am2, am1 = p2_ref[:, cols], p1_ref[:, cols]
            a_out_ref[:, cols] = a
        else:
            prev = carry_scr[:, cols]
            p2 = jnp.broadcast_to(prev[CARRY_ROWS - 2:CARRY_ROWS - 1], a.shape)
            p1 = jnp.broadcast_to(prev[CARRY_ROWS - 1:], a.shape)
            row = lax.broadcasted_iota(jnp.int32, a.shape, 0)
            am1 = jnp.where(row < 1, p1, pltpu.roll(a, 1, 0))
            am2 = jnp.where(row < 1, p2, jnp.where(row < 2, p1, pltpu.roll(a, 2, 0)))
            carry_scr[:, cols] = a[tm - CARRY_ROWS:]
        cw = cw_ref[:, cols]
        c = cb_ref[:, cols] + (cw[0:1] * am2 + cw[1:2] * am1 + cw[2:3] * a)
        hid_scr[:, cols] = (jax.nn.silu(c) * gate).astype(BF16)
    acc = x1 + jnp.dot(hid_scr[...], wd_ref[...], preferred_element_type=F32)
    r2 = lax.rsqrt(jnp.mean(acc * acc, axis=-1, keepdims=True) + NORM_EPS)
    y_ref[...] = acc * r2 * fn_ref[...]
    if not decode:
        cs_ref[...] = carry_scr[...]


def _ffn(x, o, z, weights, tm, prev=None):
    B, S, D = x.shape
    ns = S // tm
    decode = prev is not None
    wo, gn, wu, cw, cb, wd, fn = weights
    const = lambda shape: pl.BlockSpec(shape, lambda b, s: (0,) * len(shape), pipeline_mode=pl.Buffered(1))
    xs = pl.BlockSpec((None, tm, D), lambda b, s: (b, s, 0))
    in_specs = [xs, pl.BlockSpec((tm, D_ATT), lambda b, s: (b * ns + s, 0)),
                pl.BlockSpec((tm, D_SSM), lambda b, s: (s, b)),
                const((D, D)), const((1, D)), const((D, 2 * D_FF)), const((CONV_W, D_FF)), const((1, D_FF)),
                const((D_FF, D)), const((1, D))]
    args = [x, o, z, wo, gn, wu, cw, cb, wd, fn]
    scratch = [pltpu.VMEM((tm, D_FF), BF16)]
    if decode:
        rows = pl.BlockSpec((tm, D_FF), lambda b, s: (0, 0))
        in_specs += [rows, rows]
        args += list(prev)
        out_specs = [xs, rows]
        out_shape = [jax.ShapeDtypeStruct((B, S, D), F32), jax.ShapeDtypeStruct((tm, D_FF), F32)]
    else:
        out_specs = [xs, pl.BlockSpec((None, CARRY_ROWS, D_FF), lambda b, s: (b, 0, 0))]
        out_shape = [jax.ShapeDtypeStruct((B, S, D), F32), jax.ShapeDtypeStruct((B, CARRY_ROWS, D_FF), F32)]
        scratch.append(pltpu.VMEM((CARRY_ROWS, D_FF), F32))
    y, extra = pl.pallas_call(
        functools.partial(_ffn_kernel, tm=tm, decode=decode),
        grid=(B, ns),
        in_specs=in_specs, out_specs=out_specs, out_shape=out_shape, scratch_shapes=scratch,
        compiler_params=_params("parallel", "arbitrary"),
        name="ffn_decode" if decode else "ffn",
    )(*args)
    return y, extra


def kernel(x_prompt, x_sample, cache_k, cache_v, state_ssm_re, state_ssm_im, state_conv, page_table, norm_mix, w_in, lambda_q1, lambda_k1, lambda_q2, lambda_k2, subln_g, ssm_a_re, ssm_a_im, ssm_log_dt, ssm_b_re, ssm_b_im, ssm_c_re, ssm_c_im, ssm_d, w_glu, w_o, norm_ffn, w_up, conv_w, conv_b, w_down, final_norm):
    assert norm_mix.shape[0] == 1, "single layer"
    Bp, Sp, D = x_prompt.shape
    Bs = x_sample.shape[0]
    G, N, C = N_SSM_GROUPS, SSM_STATE, SSM_GROUP
    past_len = page_table.shape[1] * cache_k.shape[2]

    w_in_b = w_in[0].astype(BF16)
    wo_b = w_o[0].astype(BF16)
    wglu_b = w_glu[0].astype(BF16)
    ffn_w = (wo_b, norm_ffn[0].reshape(1, D), w_up[0].astype(BF16), conv_w[0], conv_b[0].reshape(1, D_FF),
             w_down[0].astype(BF16), final_norm.reshape(1, D))
    lam_vecs = [v.reshape(1, ATT_HEAD_DIM) for v in (lambda_q1[0], lambda_k1[0], lambda_q2[0], lambda_k2[0])]
    g_sub = subln_g[0].reshape(1, HEAD_W)

    lb_re, lb_im, bbt_re, bbt_im = _ssm_prep(ssm_a_re[0], ssm_a_im[0], ssm_log_dt[0], ssm_b_re[0], ssm_b_im[0])
    eye_b = jnp.eye(B_GROUPS, dtype=F32)[None, :, None, :, None]
    place_b = lambda bbt: (bbt.reshape(G // B_GROUPS, B_GROUPS, C, 1, N) * eye_b).reshape(
        G // B_GROUPS, B_GROUPS * C, B_GROUPS * N)
    bb_blk = jnp.concatenate([place_b(bbt_re), place_b(bbt_im)], axis=2).astype(BF16)
    eye_c = jnp.eye(C_GROUPS, dtype=F32)[None, :, None, :, None]
    place_c = lambda c: (jnp.transpose(c, (0, 2, 1)).reshape(G // C_GROUPS, C_GROUPS, N, 1, C) * eye_c).reshape(
        G // C_GROUPS, C_GROUPS * N, C_GROUPS * C)
    c_blk = jnp.stack([place_c(ssm_c_re[0]), -place_c(ssm_c_im[0])], axis=1).astype(BF16)

    tabs_p = _rope_tables(np.arange(Sp))
    q, k, v, u = _in_proj(x_prompt, norm_mix[0], w_in_b, tabs_p, tm=IN_PROJ_ROWS)
    tabs_s = _rope_tables(np.full((Bs,), past_len))
    qs, ks, vs, us = _in_proj(x_sample.reshape(1, Bs, D), norm_mix[0], w_in_b, tabs_s, tm=Bs)

    o, os_ = _attn(q, k, v, qs, ks, vs, cache_k[0], cache_v[0], page_table, lam_vecs, g_sub, Bp, Sp, tq=ATTN_Q_ROWS)

    zeros_h = jnp.zeros((Bp, N_STATE), F32)
    z, hre_p, him_p = _ssm(u.reshape(Sp, Bp, D_SSM), bb_blk, lb_re, lb_im, c_blk, ssm_d[0], wglu_b,
                           zeros_h, zeros_h, tc=SSM_STEPS)
    y_prompt, cs_p = _ffn(x_prompt, o, z.reshape(Sp, Bp * D_SSM), ffn_w, tm=FFN_ROWS)
    conv_prompt = cs_p[:, CARRY_ROWS - (CONV_W - 1):]

    zs, hre_s, him_s = _ssm(us.reshape(1, Bs, D_SSM), bb_blk, lb_re, lb_im, c_blk, ssm_d[0], wglu_b,
                            state_ssm_re[0].reshape(Bs, N_STATE), state_ssm_im[0].reshape(Bs, N_STATE), tc=1)
    sc = state_conv[0]
    y_s, a_s = _ffn(x_sample.reshape(1, Bs, D), os_, zs.reshape(Bs, D_SSM), ffn_w, tm=Bs, prev=(sc[:, 0], sc[:, 1]))
    conv_sample = jnp.stack([sc[:, 1], a_s], axis=1)

    st = lambda h, b: h.reshape(1, b, G, N)
    return (y_prompt, y_s.reshape(Bs, 1, D),
            k.reshape(1, Bp, Sp, N_ATT_HEADS, HEAD_W), v.reshape(1, Bp, Sp, N_ATT_HEADS, HEAD_W),
            st(hre_p, Bp), st(him_p, Bp), conv_prompt[None],
            ks.reshape(1, Bs, 1, N_ATT_HEADS, HEAD_W), vs.reshape(1, Bs, 1, N_ATT_HEADS, HEAD_W),
            st(hre_s, Bs), st(him_s, Bs), conv_sample[None])
```

```python
import functools
import math

import jax
import jax.numpy as jnp
import numpy as np
from jax import lax
from jax.experimental import pallas as pl
from jax.experimental.pallas import tpu as pltpu

F32 = jnp.float32
BF16 = jnp.bfloat16

D_MODEL = 1024
D_ATT = 512
D_SSM = 512
ATT_HEAD_DIM = 64
N_ATT_HEADS = 4
HEAD_W = 2 * ATT_HEAD_DIM
ROT_DIM = 16
ROPE_THETA = 500000.0
SSM_GROUP = 16
N_SSM_GROUPS = 32
SSM_STATE = 64
N_STATE = N_SSM_GROUPS * SSM_STATE
D_FF = 2816
CONV_W = 3
NORM_EPS = 1e-6
SUBLN_EPS = 1e-5
NEG_INF = -1e30
D_PROJ = 3 * D_ATT + D_SSM
LAMBDA_INIT = 0.8 - 0.6 * math.exp(-0.3 * 0)
LOG2_E = math.log2(math.e)

V7X_VMEM_LIMIT = 56 * 1024 * 1024
IN_PROJ_ROWS = 1024
ATTN_Q_ROWS = 256
SSM_STEPS = 128
FFN_ROWS = 512
FF_CHUNK = 256
N_FF_CHUNKS = D_FF // FF_CHUNK
DECODE_GROUPS = 4
CARRY_ROWS = 8


def _params(*sem):
    return pltpu.CompilerParams(dimension_semantics=sem, vmem_limit_bytes=V7X_VMEM_LIMIT)


def _lambda(lq1_ref, lk1_ref, lq2_ref, lk2_ref):
    s1 = jnp.sum(lq1_ref[...] * lk1_ref[...], axis=-1, keepdims=True)
    s2 = jnp.sum(lq2_ref[...] * lk2_ref[...], axis=-1, keepdims=True)
    return jnp.exp(s1) - jnp.exp(s2) + LAMBDA_INIT


def _sub_ln(o, g):
    r = lax.rsqrt(jnp.mean(o * o, axis=-1, keepdims=True) + SUBLN_EPS)
    return (o * r * g) * (1.0 - LAMBDA_INIT)


def _ssm_prep_kernel(ar_ref, ai_ref, ldt_ref, art_ref, ait_ref, brt_ref, bit_ref,
                     lbre_ref, lbim_ref, bbre_ref, bbim_ref):
    dt = jnp.exp(ldt_ref[...])

    def disc(ar, ai):
        mag = jnp.exp(ar * dt)
        return mag * jnp.cos(ai * dt), mag * jnp.sin(ai * dt)

    lb_re, lb_im = disc(ar_ref[...], ai_ref[...])
    lbre_ref[...] = lb_re
    lbim_ref[...] = lb_im
    ar, ai = art_ref[...], ait_ref[...]
    t_re, t_im = disc(ar, ai)
    den = ar * ar + ai * ai
    nr, ni = t_re - 1.0, t_im
    f_re = (nr * ar + ni * ai) / den
    f_im = (ni * ar - nr * ai) / den
    br, bi = brt_ref[...], bit_ref[...]
    bbre_ref[...] = f_re * br - f_im * bi
    bbim_ref[...] = f_re * bi + f_im * br


def _ssm_prep(a_re, a_im, log_dt, b_re, b_im):
    G, N, C = N_SSM_GROUPS, SSM_STATE, SSM_GROUP
    tile = lambda a: jnp.tile(a, (1, C))
    flat = lambda b: jnp.transpose(b, (0, 2, 1)).reshape(G, C * N)
    outs = pl.pallas_call(
        _ssm_prep_kernel,
        out_shape=[jax.ShapeDtypeStruct((G, N), F32)] * 2 + [jax.ShapeDtypeStruct((G, C * N), F32)] * 2,
        name="ssm_prep",
    )(a_re, a_im, log_dt.reshape(G, 1), tile(a_re), tile(a_im), flat(b_re), flat(b_im))
    return outs


def _in_proj_kernel(x_ref, g_ref, w_ref, cos_ref, sa_ref, sb_ref, q_ref, k_ref, v_ref, u_ref):
    tm = x_ref.shape[0]
    x = x_ref[...]
    r = lax.rsqrt(jnp.mean(x * x, axis=-1, keepdims=True) + NORM_EPS)
    hn = (x * r * g_ref[...]).astype(BF16)
    proj = jnp.dot(hn, w_ref[...], preferred_element_type=F32)
    cos, sa, sb = cos_ref[...], sa_ref[...], sb_ref[...]
    half = ROT_DIM // 2

    def rope(xh):
        return xh * cos + pltpu.roll(xh, HEAD_W - half, 1) * sa + pltpu.roll(xh, half, 1) * sb

    for h in range(N_ATT_HEADS):
        lo, hi = h * HEAD_W, (h + 1) * HEAD_W
        q_ref[:, lo:hi] = rope(proj[:, lo:hi]) * (ATT_HEAD_DIM ** -0.5)
        k_ref[pl.ds(h, tm, stride=N_ATT_HEADS), :] = rope(proj[:, D_ATT + lo:D_ATT + hi])
        v_ref[pl.ds(h, tm, stride=N_ATT_HEADS), :] = proj[:, 2 * D_ATT + lo:2 * D_ATT + hi]
    u_ref[...] = proj[:, 3 * D_ATT:]


def _rope_tables(pos):
    half = ROT_DIM // 2
    inv = ROPE_THETA ** (-np.arange(0, ROT_DIM, 2, dtype=np.float64) / ROT_DIM)
    ang = np.asarray(pos, np.float64)[:, None] * inv[None, :]
    cos, sin = np.cos(ang), np.sin(ang)
    S = ang.shape[0]
    pad = np.zeros((S, ATT_HEAD_DIM - ROT_DIM))
    zh = np.zeros((S, half))
    comp = lambda a, b, c: np.concatenate([a, b, c], axis=1)
    two = lambda t: jnp.asarray(np.concatenate([t, t], axis=1), F32)
    return (two(comp(cos, cos, pad + 1.0)), two(comp(-sin, zh, pad)), two(comp(zh, sin, pad)))


def _in_proj(x, g, w_bf16, tables, tm):
    B, S, D = x.shape
    ns = S // tm
    row = lambda b, s: (b * ns + s, 0)
    tab = pl.BlockSpec((tm, HEAD_W), lambda b, s: (s, 0))
    q_spec = pl.BlockSpec((tm, D_ATT), row)
    kv_spec = pl.BlockSpec((tm * N_ATT_HEADS, HEAD_W), row)
    kv_shape = jax.ShapeDtypeStruct((B * S * N_ATT_HEADS, HEAD_W), F32)
    q, k, v, u = pl.pallas_call(
        _in_proj_kernel,
        grid=(B, ns),
        in_specs=[pl.BlockSpec((None, tm, D), lambda b, s: (b, s, 0)),
                  pl.BlockSpec((1, D), lambda b, s: (0, 0)),
                  pl.BlockSpec((D, D_PROJ), lambda b, s: (0, 0)),
                  tab, tab, tab],
        out_specs=[q_spec, kv_spec, kv_spec, pl.BlockSpec((tm, D_SSM), lambda b, s: (s, b))],
        out_shape=[jax.ShapeDtypeStruct((B * S, D_ATT), F32), kv_shape, kv_shape,
                   jax.ShapeDtypeStruct((S, B * D_SSM), F32)],
        compiler_params=_params("parallel", "parallel"),
        name="in_proj",
    )(x, g.reshape(1, D), w_bf16, *tables)
    return q, k, v, u


def _prompt_tile(i, lam, g_col, q_ref, k_ref, v_ref, o_ref, tq):
    nh = N_ATT_HEADS
    feat = lax.broadcasted_iota(jnp.int32, (HEAD_W, tq), 0)
    units = [(h, c) for h in range(nh) for c in range(2)]
    hs = lambda h: slice(h * HEAD_W, (h + 1) * HEAD_W)
    q_t = []
    for h, c in units:
        qh_t = (q_ref[:, hs(h)] * LOG2_E).T
        own = (feat < ATT_HEAD_DIM) if c == 0 else (feat >= ATT_HEAD_DIM)
        q_t.append(jnp.where(own, qh_t, 0.0))

    ones_rows = jnp.ones((8, tq), F32)

    n = range(len(units))
    rows = lambda j, h: pl.ds(pl.multiple_of(j * tq, tq) * nh + h, tq, stride=nh)

    def scores(j, masked):
        s = [jnp.dot(k_ref[rows(j, h), :], q_t[u], preferred_element_type=F32)
             for u, (h, c) in enumerate(units)]
        if masked:
            key = lax.broadcasted_iota(jnp.int32, (tq, tq), 0)
            qry = lax.broadcasted_iota(jnp.int32, (tq, tq), 1)
            s = [jnp.where(key <= qry, su, NEG_INF) for su in s]
        return s

    def softmax(s, carry):
        m_new = [jnp.maximum(carry[u][0], jnp.max(s[u], axis=0, keepdims=True)) for u in n]
        alpha = [jnp.exp2(carry[u][0] - m_new[u]) for u in n]
        return m_new, alpha, [jnp.exp2(s[u] - m_new[u]) for u in n]

    def update(j, carry, m_new, alpha, p):
        pv = [jnp.dot(jnp.concatenate([v_ref[rows(j, h), :].T, ones_rows], axis=0), p[u],
                      preferred_element_type=F32) for u, (h, c) in enumerate(units)]
        return tuple((m_new[u], alpha[u] * carry[u][1] + pv[u][HEAD_W:HEAD_W + 1],
                      alpha[u] * carry[u][2] + pv[u][:HEAD_W]) for u in n)

    init = tuple((jnp.full((1, tq), NEG_INF, F32), jnp.zeros((1, tq), F32), jnp.zeros((HEAD_W, tq), F32))
                 for _ in units)
    carry = lax.fori_loop(0, i, lambda j, c: update(j, c, *softmax(scores(j, False), c)), init)
    yield
    s = scores(i, True)
    yield
    stats = softmax(s, carry)
    yield
    carry = update(i, carry, *stats)
    for h in range(nh):
        (_, l1, acc1), (_, l2, acc2) = carry[2 * h], carry[2 * h + 1]
        o_t = acc1 / l1 - lam * (acc2 / l2)
        r = lax.rsqrt(jnp.mean(o_t * o_t, axis=0, keepdims=True) + SUBLN_EPS)
        o_ref[:, h * HEAD_W:(h + 1) * HEAD_W] = ((o_t * r * g_col) * (1.0 - LAMBDA_INIT)).T


def _decode_step(j, n_steps, lam, g_row, q_ref, kn_ref, vn_ref, k_refs, v_refs, o_ref, m_scr, l_scr, acc_scr):
    H = N_ATT_HEADS

    @pl.when(j == 0)
    def _():
        m_scr[...] = jnp.full(m_scr.shape, NEG_INF, F32)
        l_scr[...] = jnp.zeros(l_scr.shape, F32)
        acc_scr[...] = jnp.zeros(acc_scr.shape, F32)

    q = q_ref[...]
    lane = lax.broadcasted_iota(jnp.int32, q.shape, 1)
    q8 = jnp.concatenate([jnp.where(lane < ATT_HEAD_DIM, q, 0.0), jnp.where(lane >= ATT_HEAD_DIM, q, 0.0)], axis=0)
    rows = k_refs[0].shape[0]
    zero = jnp.zeros_like(q8)
    pad = jnp.zeros((HEAD_W - 4 * H, 2 * HEAD_W), F32)
    q_pair = jnp.concatenate([jnp.concatenate([q8, zero], axis=1), jnp.concatenate([zero, q8], axis=1), pad], axis=0)
    r = lax.broadcasted_iota(jnp.int32, (4 * H, rows), 0)
    col = lax.broadcasted_iota(jnp.int32, (4 * H, rows), 1)
    own = (col % H) == (r % H)
    both = lambda x: jnp.concatenate([x, x], axis=0)
    halves = lambda x, op: op(x[:2 * H], x[2 * H:])

    def scores(group):
        k_pairs = jnp.concatenate([jnp.concatenate([k_refs[a][...], k_refs[a + 1][...]], axis=1) for a in group],
                                  axis=0)
        s_t = lax.dot_general(k_pairs, q_pair, (((1,), (1,)), ((), ())), preferred_element_type=F32)
        return jnp.concatenate([jnp.where(own, s_t[n * rows:(n + 1) * rows].T[:4 * H], NEG_INF)
                                for n in range(len(group))], axis=1)

    def weighted_values(group, p):
        pv = None
        for n, a in enumerate(group):
            blk = slice(n * rows, (n + 1) * rows)
            part = (jnp.dot(p[:2 * H, blk], v_refs[a][...], preferred_element_type=F32)
                    + jnp.dot(p[2 * H:, blk], v_refs[a + 1][...], preferred_element_type=F32))
            pv = part if pv is None else pv + part
        return pv

    pairs = list(range(0, len(k_refs), 2))
    groups = [pairs[g::DECODE_GROUPS] for g in range(DECODE_GROUPS)]
    s = [scores(g) for g in groups]
    yield
    m_g = [halves(jnp.max(sg, axis=-1, keepdims=True), jnp.maximum) for sg in s]
    p = [jnp.exp(sg - both(mg)) for sg, mg in zip(s, m_g)]
    l_g = [halves(jnp.sum(pg, axis=-1, keepdims=True), jnp.add) for pg in p]
    yield
    pv_g = [weighted_values(g, pg) for g, pg in zip(groups, p)]
    m = m_scr[...]
    m_new = functools.reduce(jnp.maximum, m_g, m)
    alpha = jnp.exp(m - m_new)
    l, acc = alpha * l_scr[...], alpha * acc_scr[...]
    for mg, lg, pvg in zip(m_g, l_g, pv_g):
        w = jnp.exp(mg - m_new)
        l, acc = l + w * lg, acc + w * pvg
    m_scr[...] = m_new
    l_scr[...] = l
    acc_scr[...] = acc

    @pl.when(j == n_steps - 1)
    def _():
        kn = jnp.concatenate([kn_ref[...], kn_ref[...]], axis=0)
        vn = jnp.concatenate([vn_ref[...], vn_ref[...]], axis=0)
        s_new = jnp.sum(q8 * kn, axis=-1, keepdims=True)
        m_old = m_scr[...]
        m_fin = jnp.maximum(m_old, s_new)
        a = jnp.exp(m_old - m_fin)
        p_new = jnp.exp(s_new - m_fin)
        l = a * l_scr[...] + p_new
        acc = (a * acc_scr[...] + p_new * vn) / l
        o_ref[...] = _sub_ln(acc[:H] - lam * acc[H:], g_row)


def _attn_kernel(pt_ref, lq1_ref, lk1_ref, lq2_ref, lk2_ref, gcol_ref, grow_ref, q_ref, k_ref, v_ref,
                 qs_ref, kn_ref, vn_ref, ck_hbm, cv_hbm, o_ref, os_ref, m_scr, l_scr, acc_scr, kbuf, vbuf, sem,
                 *, tq, steps_per_seq, pages):
    i = pl.program_id(1)
    step = pl.program_id(0) * pl.num_programs(1) + i
    n_steps = pl.num_programs(0) * pl.num_programs(1)

    def page_copies(of_step, p):
        slot = of_step % 2
        pid = pt_ref[of_step // steps_per_seq, (of_step % steps_per_seq) * pages + p]
        return (pltpu.make_async_copy(ck_hbm.at[pid], kbuf.at[slot, p], sem.at[slot, 0]),
                pltpu.make_async_copy(cv_hbm.at[pid], vbuf.at[slot, p], sem.at[slot, 1]))

    def fetch(of_step, do, unrolled):
        def one(p):
            for c in page_copies(of_step, p):
                do(c)
        if unrolled:
            for p in range(pages):
                one(p)
        else:
            lax.fori_loop(0, pages, lambda p, _: (one(p), 0)[1], 0)

    start, wait = (lambda c: c.start()), (lambda c: c.wait())

    @pl.when(step == 0)
    def _():
        fetch(step, start, unrolled=False)

    following = lax.rem(step + 1, n_steps)
    fetch(following, start, unrolled=True)
    lam = _lambda(lq1_ref, lk1_ref, lq2_ref, lk2_ref)
    prompt = _prompt_tile(i, lam, gcol_ref[...], q_ref, k_ref, v_ref, o_ref, tq)
    next(prompt)
    fetch(step, wait, unrolled=True)
    slot = step % 2
    decode = _decode_step(step % steps_per_seq, steps_per_seq, lam, grow_ref[...], qs_ref, kn_ref, vn_ref,
                          [kbuf.at[slot, p] for p in range(pages)], [vbuf.at[slot, p] for p in range(pages)],
                          os_ref, m_scr, l_scr, acc_scr)
    running = [prompt, decode]
    while running:
        for gen in list(running):
            if next(gen, StopIteration) is StopIteration:
                running.remove(gen)

    @pl.when(step == n_steps - 1)
    def _():
        fetch(following, wait, unrolled=False)


def _attn(q, k, v, qs, ks, vs, cache_k, cache_v, page_table, lam_vecs, g, B, S, tq):
    H = N_ATT_HEADS
    nq = S // tq
    Bs, n_pages = page_table.shape
    n_pool, page = cache_k.shape[0], cache_k.shape[1]
    steps_per_seq = (B * nq) // Bs
    assert steps_per_seq * Bs == B * nq and n_pages % steps_per_seq == 0
    P = n_pages // steps_per_seq
    ck = cache_k.reshape(n_pool, page * H, HEAD_W)
    cv = cache_v.reshape(n_pool, page * H, HEAD_W)
    seq = lambda b, i: (b * nq + i) // steps_per_seq
    vec = pl.BlockSpec((1, ATT_HEAD_DIM), lambda b, i, pt: (0, 0))
    kv = pl.BlockSpec((S * H, HEAD_W), lambda b, i, pt: (b, 0))
    qo = pl.BlockSpec((tq, D_ATT), lambda b, i, pt: (b * nq + i, 0))
    tok = pl.BlockSpec((None, H, HEAD_W), lambda b, i, pt: (seq(b, i), 0, 0))
    hbm = pl.BlockSpec(memory_space=pl.ANY)
    page_buf = pltpu.VMEM((2, P, page * H, HEAD_W), F32)
    grid_spec = pltpu.PrefetchScalarGridSpec(
        num_scalar_prefetch=1,
        grid=(B, nq),
        in_specs=[vec, vec, vec, vec, pl.BlockSpec((HEAD_W, 1), lambda b, i, pt: (0, 0)),
                  pl.BlockSpec((1, HEAD_W), lambda b, i, pt: (0, 0)), qo, kv, kv, tok, tok, tok, hbm, hbm],
        out_specs=[qo, tok],
        scratch_shapes=[pltpu.VMEM((2 * H, 1), F32), pltpu.VMEM((2 * H, 1), F32), pltpu.VMEM((2 * H, HEAD_W), F32),
                        page_buf, page_buf, pltpu.SemaphoreType.DMA((2, 2))],
    )
    tok3 = lambda a: a.reshape(Bs, H, HEAD_W)
    o, o_s = pl.pallas_call(
        functools.partial(_attn_kernel, tq=tq, steps_per_seq=steps_per_seq, pages=P),
        grid_spec=grid_spec,
        out_shape=[jax.ShapeDtypeStruct((B * S, D_ATT), F32), jax.ShapeDtypeStruct((Bs, H, HEAD_W), F32)],
        compiler_params=_params("arbitrary", "arbitrary"),
        name="attn",
    )(page_table, *lam_vecs, g.reshape(HEAD_W, 1), g.reshape(1, HEAD_W), q, k, v, tok3(qs), tok3(ks), tok3(vs), ck, cv)
    return o, o_s.reshape(Bs, D_ATT)


SCAN_COLS = 1024
B_GROUPS = 16
C_GROUPS = 8


def _ssm_kernel(u_ref, bb_ref, lre_ref, lim_ref, c_ref, d_ref, wglu_ref, h0re_ref, h0im_ref,
                z_ref, hre_ref, him_ref, bu_scr, hs_scr, *, tc, bk):
    t0 = pl.program_id(0)

    @pl.when(t0 == 0)
    def _():
        hre_ref[...] = h0re_ref[...]
        him_ref[...] = h0im_ref[...]

    rows = tc * bk
    u = u_ref[...].reshape(rows, D_SSM)
    ub = u.astype(BF16)
    part_w = B_GROUPS * SSM_STATE
    half_w = 2 * part_w
    n_half = N_SSM_GROUPS // B_GROUPS
    slab_w = C_GROUPS * SSM_STATE

    def input_map(half):
        u_half = ub[:, half * B_GROUPS * SSM_GROUP:(half + 1) * B_GROUPS * SSM_GROUP]
        bu = jnp.dot(u_half, bb_ref[half], preferred_element_type=F32)
        bu_scr[:, :, half * half_w:(half + 1) * half_w] = bu.reshape(tc, bk, half_w)

    def scan(half):
        for cg in range(part_w // SCAN_COLS):
            st = slice(half * part_w + cg * SCAN_COLS, half * part_w + (cg + 1) * SCAN_COLS)
            re = slice(half * half_w + cg * SCAN_COLS, half * half_w + (cg + 1) * SCAN_COLS)
            im = slice(re.start + part_w, re.stop + part_w)
            lr, li = lre_ref[:, st], lim_ref[:, st]
            hr, hi = hre_ref[:, st], him_ref[:, st]
            for t in range(tc):
                hr, hi = lr * hr - li * hi + bu_scr[t, :, re], lr * hi + li * hr + bu_scr[t, :, im]
                hs_scr[t, :, re] = hr
                hs_scr[t, :, im] = hi
            hre_ref[:, st] = hr
            him_ref[:, st] = hi

    def output_map(s):
        re0 = (s * C_GROUPS // B_GROUPS) * half_w + (s * C_GROUPS % B_GROUPS) * SSM_STATE
        h_re = hs_scr[:, :, re0:re0 + slab_w].reshape(rows, slab_w).astype(BF16)
        h_im = hs_scr[:, :, re0 + part_w:re0 + part_w + slab_w].reshape(rows, slab_w).astype(BF16)
        return (jnp.dot(h_re, c_ref[s, 0], preferred_element_type=F32)
                + jnp.dot(h_im, c_ref[s, 1], preferred_element_type=F32))

    slabs_per_half = B_GROUPS // C_GROUPS
    input_map(0)
    y_slabs = []
    for half in range(n_half):
        if half + 1 < n_half:
            input_map(half + 1)
        scan(half)
        y_slabs += [output_map(half * slabs_per_half + s) for s in range(slabs_per_half)]
    y = jnp.concatenate(y_slabs, axis=1) + d_ref[...] * u
    y = jax.nn.gelu(y)
    gate = jnp.dot(y.astype(BF16), wglu_ref[...], preferred_element_type=F32)
    z_ref[...] = (y * jax.nn.sigmoid(gate)).reshape(tc, bk, D_SSM)


def _ssm(u3, bb, lb_re, lb_im, c_full, d, w_glu, h0_re, h0_im, tc):
    S, bk, _ = u3.shape
    const = lambda shape: pl.BlockSpec(shape, lambda t: (0,) * len(shape))
    seq = pl.BlockSpec((tc, bk, D_SSM), lambda t: (t, 0, 0))
    rep = lambda a: jnp.broadcast_to(a.reshape(1, N_STATE), (bk, N_STATE))
    z, h_re, h_im = pl.pallas_call(
        functools.partial(_ssm_kernel, tc=tc, bk=bk),
        grid=(S // tc,),
        in_specs=[seq, const(bb.shape), const((bk, N_STATE)), const((bk, N_STATE)),
                  const(c_full.shape), const((1, D_SSM)), const((D_SSM, D_SSM)),
                  const((bk, N_STATE)), const((bk, N_STATE))],
        out_specs=[seq, const((bk, N_STATE)), const((bk, N_STATE))],
        out_shape=[jax.ShapeDtypeStruct((S, bk, D_SSM), F32), jax.ShapeDtypeStruct((bk, N_STATE), F32),
                   jax.ShapeDtypeStruct((bk, N_STATE), F32)],
        scratch_shapes=[pltpu.VMEM((tc, bk, 2 * N_STATE), F32), pltpu.VMEM((tc, bk, 2 * N_STATE), F32)],
        compiler_params=_params("arbitrary"),
        name="ssm",
    )(u3, bb, rep(lb_re), rep(lb_im), c_full, d.reshape(1, D_SSM), w_glu, h0_re, h0_im)
    return z, h_re, h_im


def _ffn_kernel(*refs, tm, decode):
    if decode:
        (x_ref, o_ref, z_ref, wo_ref, gn_ref, wu_ref, cw_ref, cb_ref, wd_ref, fn_ref, p2_ref, p1_ref,
         y_ref, a_out_ref, hid_scr) = refs
    else:
        (x_ref, o_ref, z_ref, wo_ref, gn_ref, wu_ref, cw_ref, cb_ref, wd_ref, fn_ref,
         y_ref, cs_ref, hid_scr, carry_scr) = refs

        @pl.when(pl.program_id(1) == 0)
        def _():
            carry_scr[...] = jnp.zeros(carry_scr.shape, F32)

    mix = (jnp.dot(o_ref[...].astype(BF16), wo_ref[:D_ATT, :], preferred_element_type=F32)
           + jnp.dot(z_ref[...].astype(BF16), wo_ref[D_ATT:, :], preferred_element_type=F32))
    x1 = x_ref[...] + mix
    r = lax.rsqrt(jnp.mean(x1 * x1, axis=-1, keepdims=True) + NORM_EPS)
    hn = (x1 * r * gn_ref[...]).astype(BF16)
    for j in range(N_FF_CHUNKS):
        cols = slice(j * FF_CHUNK, (j + 1) * FF_CHUNK)
        gcols = slice(D_FF + j * FF_CHUNK, D_FF + (j + 1) * FF_CHUNK)
        a = jnp.dot(hn, wu_ref[:, cols], preferred_element_type=F32)
        gate = jnp.dot(hn, wu_ref[:, gcols], preferred_element_type=F32)
        if decode:
            am2, am1 = p2_ref[:, cols], p1_ref[:, cols]
            a_out_ref[:, cols] = a
        else:
            prev = carry_scr[:, cols]
            p2 = jnp.broadcast_to(prev[CARRY_ROWS - 2:CARRY_ROWS - 1], a.shape)
            p1 = jnp.broadcast_to(prev[CARRY_ROWS - 1:], a.shape)
            row = lax.broadcasted_iota(jnp.int32, a.shape, 0)
            am1 = jnp.where(row < 1, p1, pltpu.roll(a, 1, 0))
            am2 = jnp.where(row < 1, p2, jnp.where(row < 2, p1, pltpu.roll(a, 2, 0)))
            carry_scr[:, cols] = a[tm - CARRY_ROWS:]
        cw = cw_ref[:, cols]
        c = cb_ref[:, cols] + (cw[0:1] * am2 + cw[1:2] * am1 + cw[2:3] * a)
        hid_scr[:, cols] = (jax.nn.silu(c) * gate).astype(BF16)
    acc = x1 + jnp.dot(hid_scr[...], wd_ref[...], preferred_element_type=F32)
    r2 = lax.rsqrt(jnp.mean(acc * acc, axis=-1, keepdims=True) + NORM_EPS)
    y_ref[...] = acc * r2 * fn_ref[...]
    if not decode:
        cs_ref[...] = carry_scr[...]


def _ffn(x, o, z, weights, tm, prev=None):
    B, S, D = x.shape
    ns = S // tm
    decode = prev is not None
    wo, gn, wu, cw, cb, wd, fn = weights
    const = lambda shape: pl.BlockSpec(shape, lambda b, s: (0,) * len(shape), pipeline_mode=pl.Buffered(1))
    xs = pl.BlockSpec((None, tm, D), lambda b, s: (b, s, 0))
    in_specs = [xs, pl.BlockSpec((tm, D_ATT), lambda b, s: (b * ns + s, 0)),
                pl.BlockSpec((tm, D_SSM), lambda b, s: (s, b)),
                const((D, D)), const((1, D)), const((D, 2 * D_FF)), const((CONV_W, D_FF)), const((1, D_FF)),
                const((D_FF, D)), const((1, D))]
    args = [x, o, z, wo, gn, wu, cw, cb, wd, fn]
    scratch = [pltpu.VMEM((tm, D_FF), BF16)]
    if decode:
        rows = pl.BlockSpec((tm, D_FF), lambda b, s: (0, 0))
        in_specs += [rows, rows]
        args += list(prev)
        out_specs = [xs, rows]
        out_shape = [jax.ShapeDtypeStruct((B, S, D), F32), jax.ShapeDtypeStruct((tm, D_FF), F32)]
    else:
        out_specs = [xs, pl.BlockSpec((None, CARRY_ROWS, D_FF), lambda b, s: (b, 0, 0))]
        out_shape = [jax.ShapeDtypeStruct((B, S, D), F32), jax.ShapeDtypeStruct((B, CARRY_ROWS, D_FF), F32)]
        scratch.append(pltpu.VMEM((CARRY_ROWS, D_FF), F32))
    y, extra = pl.pallas_call(
        functools.partial(_ffn_kernel, tm=tm, decode=decode),
        grid=(B, ns),
        in_specs=in_specs, out_specs=out_specs, out_shape=out_shape, scratch_shapes=scratch,
        compiler_params=_params("parallel", "arbitrary"),
        name="ffn_decode" if decode else "ffn",
    )(*args)
    return y, extra


def kernel(x_prompt, x_sample, cache_k, cache_v, state_ssm_re, state_ssm_im, state_conv, page_table, norm_mix, w_in, lambda_q1, lambda_k1, lambda_q2, lambda_k2, subln_g, ssm_a_re, ssm_a_im, ssm_log_dt, ssm_b_re, ssm_b_im, ssm_c_re, ssm_c_im, ssm_d, w_glu, w_o, norm_ffn, w_up, conv_w, conv_b, w_down, final_norm):
    assert norm_mix.shape[0] == 1, "single layer"
    Bp, Sp, D = x_prompt.shape
    Bs = x_sample.shape[0]
    G, N, C = N_SSM_GROUPS, SSM_STATE, SSM_GROUP
    past_len = page_table.shape[1] * cache_k.shape[2]

    w_in_b = w_in[0].astype(BF16)
    wo_b = w_o[0].astype(BF16)
    wglu_b = w_glu[0].astype(BF16)
    ffn_w = (wo_b, norm_ffn[0].reshape(1, D), w_up[0].astype(BF16), conv_w[0], conv_b[0].reshape(1, D_FF),
             w_down[0].astype(BF16), final_norm.reshape(1, D))
    lam_vecs = [v.reshape(1, ATT_HEAD_DIM) for v in (lambda_q1[0], lambda_k1[0], lambda_q2[0], lambda_k2[0])]
    g_sub = subln_g[0].reshape(1, HEAD_W)

    lb_re, lb_im, bbt_re, bbt_im = _ssm_prep(ssm_a_re[0], ssm_a_im[0], ssm_log_dt[0], ssm_b_re[0], ssm_b_im[0])
    eye_b = jnp.eye(B_GROUPS, dtype=F32)[None, :, None, :, None]
    place_b = lambda bbt: (bbt.reshape(G // B_GROUPS, B_GROUPS, C, 1, N) * eye_b).reshape(
        G // B_GROUPS, B_GROUPS * C, B_GROUPS * N)
    bb_blk = jnp.concatenate([place_b(bbt_re), place_b(bbt_im)], axis=2).astype(BF16)
    eye_c = jnp.eye(C_GROUPS, dtype=F32)[None, :, None, :, None]
    place_c = lambda c: (jnp.transpose(c, (0, 2, 1)).reshape(G // C_GROUPS, C_GROUPS, N, 1, C) * eye_c).reshape(
        G // C_GROUPS, C_GROUPS * N, C_GROUPS * C)
    c_blk = jnp.stack([place_c(ssm_c_re[0]), -place_c(ssm_c_im[0])], axis=1).astype(BF16)

    tabs_p = _rope_tables(np.arange(Sp))
    q, k, v, u = _in_proj(x_prompt, norm_mix[0], w_in_b, tabs_p, tm=IN_PROJ_ROWS)
    tabs_s = _rope_tables(np.full((Bs,), past_len))
    qs, ks, vs, us = _in_proj(x_sample.reshape(1, Bs, D), norm_mix[0], w_in_b, tabs_s, tm=Bs)

    o, os_ = _attn(q, k, v, qs, ks, vs, cache_k[0], cache_v[0], page_table, lam_vecs, g_sub, Bp, Sp, tq=ATTN_Q_ROWS)

    zeros_h = jnp.zeros((Bp, N_STATE), F32)
    z, hre_p, him_p = _ssm(u.reshape(Sp, Bp, D_SSM), bb_blk, lb_re, lb_im, c_blk, ssm_d[0], wglu_b,
                           zeros_h, zeros_h, tc=SSM_STEPS)
    y_prompt, cs_p = _ffn(x_prompt, o, z.reshape(Sp, Bp * D_SSM), ffn_w, tm=FFN_ROWS)
    conv_prompt = cs_p[:, CARRY_ROWS - (CONV_W - 1):]

    zs, hre_s, him_s = _ssm(us.reshape(1, Bs, D_SSM), bb_blk, lb_re, lb_im, c_blk, ssm_d[0], wglu_b,
                            state_ssm_re[0].reshape(Bs, N_STATE), state_ssm_im[0].reshape(Bs, N_STATE), tc=1)
    sc = state_conv[0]
    y_s, a_s = _ffn(x_sample.reshape(1, Bs, D), os_, zs.reshape(Bs, D_SSM), ffn_w, tm=Bs, prev=(sc[:, 0], sc[:, 1]))
    conv_sample = jnp.stack([sc[:, 1], a_s], axis=1)

    st = lambda h, b: h.reshape(1, b, G, N)
    return (y_prompt, y_s.reshape(Bs, 1, D),
            k.reshape(1, Bp, Sp, N_ATT_HEADS, HEAD_W), v.reshape(1, Bp, Sp, N_ATT_HEADS, HEAD_W),
            st(hre_p, Bp), st(him_p, Bp), conv_prompt[None],
            ks.reshape(1, Bs, 1, N_ATT_HEADS, HEAD_W), vs.reshape(1, Bs, 1, N_ATT_HEADS, HEAD_W),
            st(hre_s, Bs), st(him_s, Bs), conv_sample[None])
```

```python
import functools
import math

import jax
import jax.numpy as jnp
import numpy as np
from jax import lax
from jax.experimental import pallas as pl
from jax.experimental.pallas import tpu as pltpu

F32 = jnp.float32
BF16 = jnp.bfloat16

D_MODEL = 1024
D_ATT = 512
D_SSM = 512
ATT_HEAD_DIM = 64
N_ATT_HEADS = 4
HEAD_W = 2 * ATT_HEAD_DIM
ROT_DIM = 16
ROPE_THETA = 500000.0
SSM_GROUP = 16
N_SSM_GROUPS = 32
SSM_STATE = 64
N_STATE = N_SSM_GROUPS * SSM_STATE
D_FF = 2816
CONV_W = 3
NORM_EPS = 1e-6
SUBLN_EPS = 1e-5
NEG_INF = -1e30
D_PROJ = 3 * D_ATT + D_SSM
LAMBDA_INIT = 0.8 - 0.6 * math.exp(-0.3 * 0)
LOG2_E = math.log2(math.e)

V7X_VMEM_LIMIT = 56 * 1024 * 1024
IN_PROJ_ROWS = 1024
ATTN_Q_ROWS = 256
SSM_STEPS = 128
FFN_ROWS = 512
FF_CHUNK = 256
N_FF_CHUNKS = D_FF // FF_CHUNK
BF16_SUBLANES = 16
DECODE_GROUPS = 4
CARRY_ROWS = 8


def _params(*sem):
    return pltpu.CompilerParams(dimension_semantics=sem, vmem_limit_bytes=V7X_VMEM_LIMIT)


def _lambda(lq1_ref, lk1_ref, lq2_ref, lk2_ref):
    s1 = jnp.sum(lq1_ref[...] * lk1_ref[...], axis=-1, keepdims=True)
    s2 = jnp.sum(lq2_ref[...] * lk2_ref[...], axis=-1, keepdims=True)
    return jnp.exp(s1) - jnp.exp(s2) + LAMBDA_INIT


def _sub_ln(o, g):
    r = lax.rsqrt(jnp.mean(o * o, axis=-1, keepdims=True) + SUBLN_EPS)
    return (o * r * g) * (1.0 - LAMBDA_INIT)


def _ssm_prep_kernel(ar_ref, ai_ref, ldt_ref, arr_ref, air_ref, ldtr_ref, br_ref, bi_ref, cre_ref, cim_ref,
                     lbre_ref, lbim_ref, bb_ref, c_ref):
    G, N, C = N_SSM_GROUPS, SSM_STATE, SSM_GROUP

    def disc(ar, ai, dt):
        mag = jnp.exp(ar * dt)
        return mag * jnp.cos(ai * dt), mag * jnp.sin(ai * dt)

    lb_re, lb_im = disc(ar_ref[...], ai_ref[...], jnp.exp(ldt_ref[...]))
    lbre_ref[...] = lb_re
    lbim_ref[...] = lb_im
    ar, ai = arr_ref[...], air_ref[...]
    t_re, t_im = disc(ar, ai, jnp.exp(ldtr_ref[...]))
    den = ar * ar + ai * ai
    nr, ni = t_re - 1.0, t_im
    f_re = (nr * ar + ni * ai) / den
    f_im = (ni * ar - nr * ai) / den
    br, bi = br_ref[...], bi_ref[...]
    bb = ((f_re * br - f_im * bi).astype(BF16), (f_re * bi + f_im * br).astype(BF16))

    def iota(shape, axis):
        return lax.broadcasted_iota(jnp.int32, shape, axis)

    half_rows, part_w = B_GROUPS * C, B_GROUPS * N
    spread = (iota((N, part_w), 1) % N == iota((N, part_w), 0)).astype(BF16)
    own = iota((half_rows, part_w), 0) // C == iota((half_rows, part_w), 1) // N
    for half in range(G // B_GROUPS):
        for part in range(2):
            tiled = jnp.dot(bb[part][half * half_rows:(half + 1) * half_rows], spread, preferred_element_type=F32)
            bb_ref[half, :, part * part_w:(part + 1) * part_w] = jnp.where(own, tiled, 0.0).astype(BF16)

    slab_rows, slab_w = C_GROUPS * N, C_GROUPS * C
    spread_c = (iota((C, slab_w), 1) % C == iota((C, slab_w), 0)).astype(BF16)
    own_c = (iota((G * N, slab_w), 0) // N) % C_GROUPS == iota((G * N, slab_w), 1) // C
    for part, (c_rows_ref, sign) in enumerate(((cre_ref, 1.0), (cim_ref, -1.0))):
        tiled = jnp.dot(c_rows_ref[...].astype(BF16), spread_c, preferred_element_type=F32)
        blk = jnp.where(own_c, sign * tiled, 0.0).astype(BF16)
        for s in range(G // C_GROUPS):
            c_ref[s, part] = blk[s * slab_rows:(s + 1) * slab_rows]


def _ssm_prep(a_re, a_im, log_dt, b_re, b_im, c_re, c_im):
    G, N, C = N_SSM_GROUPS, SSM_STATE, SSM_GROUP
    per_channel = lambda a: jnp.repeat(a, C, axis=0)
    b_rows = lambda b: jnp.transpose(b, (0, 2, 1)).reshape(G * C, N)
    c_rows = lambda c: jnp.transpose(c, (0, 2, 1)).reshape(G * N, C)
    ldt = log_dt.reshape(G, 1)
    return pl.pallas_call(
        _ssm_prep_kernel,
        out_shape=[jax.ShapeDtypeStruct((G, N), F32)] * 2
                  + [jax.ShapeDtypeStruct((G // B_GROUPS, B_GROUPS * C, 2 * B_GROUPS * N), BF16),
                     jax.ShapeDtypeStruct((G // C_GROUPS, 2, C_GROUPS * N, C_GROUPS * C), BF16)],
        name="ssm_prep",
    )(a_re, a_im, ldt, per_channel(a_re), per_channel(a_im), per_channel(ldt), b_rows(b_re), b_rows(b_im),
      c_rows(c_re), c_rows(c_im))


def _in_proj_kernel(x_ref, g_ref, w_ref, cos_ref, sa_ref, sb_ref, *rest):
    n_cast = (len(rest) - 4) // 2
    q_ref, k_ref, v_ref, u_ref = rest[n_cast:n_cast + 4]
    for src_ref, dst_ref in zip(rest[:n_cast], rest[n_cast + 4:]):
        dst_ref[...] = src_ref[...].astype(BF16)
    tm = x_ref.shape[0]
    x = x_ref[...]
    r = lax.rsqrt(jnp.mean(x * x, axis=-1, keepdims=True) + NORM_EPS)
    hn = (x * r * g_ref[...]).astype(BF16)
    proj = jnp.dot(hn, w_ref[...].astype(BF16), preferred_element_type=F32)
    cos, sa, sb = cos_ref[...], sa_ref[...], sb_ref[...]
    half = ROT_DIM // 2

    def rope(xh):
        return xh * cos + pltpu.roll(xh, HEAD_W - half, 1) * sa + pltpu.roll(xh, half, 1) * sb

    for h in range(N_ATT_HEADS):
        lo, hi = h * HEAD_W, (h + 1) * HEAD_W
        q_ref[:, lo:hi] = rope(proj[:, lo:hi]) * (ATT_HEAD_DIM ** -0.5)
        k_ref[pl.ds(h, tm, stride=N_ATT_HEADS), :] = rope(proj[:, D_ATT + lo:D_ATT + hi])
        v_ref[pl.ds(h, tm, stride=N_ATT_HEADS), :] = proj[:, 2 * D_ATT + lo:2 * D_ATT + hi]
    u_ref[...] = proj[:, 3 * D_ATT:]


def _rope_tables(pos):
    half = ROT_DIM // 2
    inv = ROPE_THETA ** (-np.arange(0, ROT_DIM, 2, dtype=np.float64) / ROT_DIM)
    ang = np.asarray(pos, np.float64)[:, None] * inv[None, :]
    cos, sin = np.cos(ang), np.sin(ang)
    S = ang.shape[0]
    pad = np.zeros((S, ATT_HEAD_DIM - ROT_DIM))
    zh = np.zeros((S, half))
    comp = lambda a, b, c: np.concatenate([a, b, c], axis=1)
    two = lambda t: jnp.asarray(np.concatenate([t, t], axis=1), F32)
    return (two(comp(cos, cos, pad + 1.0)), two(comp(-sin, zh, pad)), two(comp(zh, sin, pad)))


def _in_proj(x, g, w, tables, tm, cast=()):
    B, S, D = x.shape
    ns = S // tm
    n_steps = B * ns
    cast_specs = [pl.BlockSpec((c.shape[0] // n_steps, c.shape[1]), lambda b, s: (b * ns + s, 0)) for c in cast]
    assert all(c.shape[0] % (n_steps * BF16_SUBLANES) == 0 for c in cast)
    row = lambda b, s: (b * ns + s, 0)
    tab = pl.BlockSpec((tm, HEAD_W), lambda b, s: (s, 0))
    q_spec = pl.BlockSpec((tm, D_ATT), row)
    kv_spec = pl.BlockSpec((tm * N_ATT_HEADS, HEAD_W), row)
    kv_shape = jax.ShapeDtypeStruct((B * S * N_ATT_HEADS, HEAD_W), F32)
    q, k, v, u, *converted = pl.pallas_call(
        _in_proj_kernel,
        grid=(B, ns),
        in_specs=[pl.BlockSpec((None, tm, D), lambda b, s: (b, s, 0)),
                  pl.BlockSpec((1, D), lambda b, s: (0, 0)),
                  pl.BlockSpec((D, D_PROJ), lambda b, s: (0, 0)),
                  tab, tab, tab] + cast_specs,
        out_specs=[q_spec, kv_spec, kv_spec, pl.BlockSpec((tm, D_SSM), lambda b, s: (s, b))] + cast_specs,
        out_shape=[jax.ShapeDtypeStruct((B * S, D_ATT), F32), kv_shape, kv_shape,
                   jax.ShapeDtypeStruct((S, B * D_SSM), F32)] + [jax.ShapeDtypeStruct(c.shape, BF16) for c in cast],
        compiler_params=_params("parallel", "parallel"),
        name="in_proj",
    )(x, g.reshape(1, D), w, *tables, *cast)
    return q, k, v, u, converted


def _prompt_tile(i, lam, g_col, q_ref, k_ref, v_ref, o_ref, tq):
    nh = N_ATT_HEADS
    feat = lax.broadcasted_iota(jnp.int32, (HEAD_W, tq), 0)
    units = [(h, c) for h in range(nh) for c in range(2)]
    hs = lambda h: slice(h * HEAD_W, (h + 1) * HEAD_W)
    q_t = []
    for h, c in units:
        qh_t = (q_ref[:, hs(h)] * LOG2_E).T
        own = (feat < ATT_HEAD_DIM) if c == 0 else (feat >= ATT_HEAD_DIM)
        q_t.append(jnp.where(own, qh_t, 0.0))

    ones_rows = jnp.ones((8, tq), F32)

    n = range(len(units))
    rows = lambda j, h: pl.ds(pl.multiple_of(j * tq, tq) * nh + h, tq, stride=nh)

    def scores(j, masked):
        s = [jnp.dot(k_ref[rows(j, h), :], q_t[u], preferred_element_type=F32)
             for u, (h, c) in enumerate(units)]
        if masked:
            key = lax.broadcasted_iota(jnp.int32, (tq, tq), 0)
            qry = lax.broadcasted_iota(jnp.int32, (tq, tq), 1)
            s = [jnp.where(key <= qry, su, NEG_INF) for su in s]
        return s

    def softmax(s, carry):
        m_new = [jnp.maximum(carry[u][0], jnp.max(s[u], axis=0, keepdims=True)) for u in n]
        alpha = [jnp.exp2(carry[u][0] - m_new[u]) for u in n]
        return m_new, alpha, [jnp.exp2(s[u] - m_new[u]) for u in n]

    def update(j, carry, m_new, alpha, p):
        pv = [jnp.dot(jnp.concatenate([v_ref[rows(j, h), :].T, ones_rows], axis=0), p[u],
                      preferred_element_type=F32) for u, (h, c) in enumerate(units)]
        return tuple((m_new[u], alpha[u] * carry[u][1] + pv[u][HEAD_W:HEAD_W + 1],
                      alpha[u] * carry[u][2] + pv[u][:HEAD_W]) for u in n)

    init = tuple((jnp.full((1, tq), NEG_INF, F32), jnp.zeros((1, tq), F32), jnp.zeros((HEAD_W, tq), F32))
                 for _ in units)
    carry = lax.fori_loop(0, i, lambda j, c: update(j, c, *softmax(scores(j, False), c)), init)
    yield
    s = scores(i, True)
    yield
    stats = softmax(s, carry)
    yield
    carry = update(i, carry, *stats)
    for h in range(nh):
        (_, l1, acc1), (_, l2, acc2) = carry[2 * h], carry[2 * h + 1]
        o_t = acc1 / l1 - lam * (acc2 / l2)
        r = lax.rsqrt(jnp.mean(o_t * o_t, axis=0, keepdims=True) + SUBLN_EPS)
        o_ref[:, h * HEAD_W:(h + 1) * HEAD_W] = ((o_t * r * g_col) * (1.0 - LAMBDA_INIT)).T


def _decode_step(j, n_steps, lam, g_row, q_ref, kn_ref, vn_ref, k_refs, v_refs, o_ref, m_scr, l_scr, acc_scr):
    H = N_ATT_HEADS

    @pl.when(j == 0)
    def _():
        m_scr[...] = jnp.full(m_scr.shape, NEG_INF, F32)
        l_scr[...] = jnp.zeros(l_scr.shape, F32)
        acc_scr[...] = jnp.zeros(acc_scr.shape, F32)

    q = q_ref[...]
    lane = lax.broadcasted_iota(jnp.int32, q.shape, 1)
    q8 = jnp.concatenate([jnp.where(lane < ATT_HEAD_DIM, q, 0.0), jnp.where(lane >= ATT_HEAD_DIM, q, 0.0)], axis=0)
    rows = k_refs[0].shape[0]
    zero = jnp.zeros_like(q8)
    pad = jnp.zeros((HEAD_W - 4 * H, 2 * HEAD_W), F32)
    q_pair = jnp.concatenate([jnp.concatenate([q8, zero], axis=1), jnp.concatenate([zero, q8], axis=1), pad], axis=0)
    r = lax.broadcasted_iota(jnp.int32, (4 * H, rows), 0)
    col = lax.broadcasted_iota(jnp.int32, (4 * H, rows), 1)
    own = (col % H) == (r % H)
    both = lambda x: jnp.concatenate([x, x], axis=0)
    halves = lambda x, op: op(x[:2 * H], x[2 * H:])

    def scores(group):
        k_pairs = jnp.concatenate([jnp.concatenate([k_refs[a][...], k_refs[a + 1][...]], axis=1) for a in group],
                                  axis=0)
        s_t = lax.dot_general(k_pairs, q_pair, (((1,), (1,)), ((), ())), preferred_element_type=F32)
        return jnp.concatenate([jnp.where(own, s_t[n * rows:(n + 1) * rows].T[:4 * H], NEG_INF)
                                for n in range(len(group))], axis=1)

    def weighted_values(group, p):
        pv = None
        for n, a in enumerate(group):
            blk = slice(n * rows, (n + 1) * rows)
            part = (jnp.dot(p[:2 * H, blk], v_refs[a][...], preferred_element_type=F32)
                    + jnp.dot(p[2 * H:, blk], v_refs[a + 1][...], preferred_element_type=F32))
            pv = part if pv is None else pv + part
        return pv

    pairs = list(range(0, len(k_refs), 2))
    groups = [pairs[g::DECODE_GROUPS] for g in range(DECODE_GROUPS)]
    s = [scores(g) for g in groups]
    yield
    m_g = [halves(jnp.max(sg, axis=-1, keepdims=True), jnp.maximum) for sg in s]
    p = [jnp.exp(sg - both(mg)) for sg, mg in zip(s, m_g)]
    l_g = [halves(jnp.sum(pg, axis=-1, keepdims=True), jnp.add) for pg in p]
    yield
    pv_g = [weighted_values(g, pg) for g, pg in zip(groups, p)]
    m = m_scr[...]
    m_new = functools.reduce(jnp.maximum, m_g, m)
    alpha = jnp.exp(m - m_new)
    l, acc = alpha * l_scr[...], alpha * acc_scr[...]
    for mg, lg, pvg in zip(m_g, l_g, pv_g):
        w = jnp.exp(mg - m_new)
        l, acc = l + w * lg, acc + w * pvg
    m_scr[...] = m_new
    l_scr[...] = l
    acc_scr[...] = acc

    @pl.when(j == n_steps - 1)
    def _():
        kn = jnp.concatenate([kn_ref[...], kn_ref[...]], axis=0)
        vn = jnp.concatenate([vn_ref[...], vn_ref[...]], axis=0)
        s_new = jnp.sum(q8 * kn, axis=-1, keepdims=True)
        m_old = m_scr[...]
        m_fin = jnp.maximum(m_old, s_new)
        a = jnp.exp(m_old - m_fin)
        p_new = jnp.exp(s_new - m_fin)
        l = a * l_scr[...] + p_new
        acc = (a * acc_scr[...] + p_new * vn) / l
        o_ref[...] = _sub_ln(acc[:H] - lam * acc[H:], g_row)


def _attn_kernel(pt_ref, lq1_ref, lk1_ref, lq2_ref, lk2_ref, gcol_ref, grow_ref, q_ref, k_ref, v_ref,
                 qs_ref, kn_ref, vn_ref, ck_hbm, cv_hbm, o_ref, os_ref, m_scr, l_scr, acc_scr, kbuf, vbuf, sem,
                 *, tq, steps_per_seq, pages):
    i = pl.program_id(1)
    step = pl.program_id(0) * pl.num_programs(1) + i
    n_steps = pl.num_programs(0) * pl.num_programs(1)

    def page_copies(of_step, p):
        slot = of_step % 2
        pid = pt_ref[of_step // steps_per_seq, (of_step % steps_per_seq) * pages + p]
        return (pltpu.make_async_copy(ck_hbm.at[pid], kbuf.at[slot, p], sem.at[slot, 0]),
                pltpu.make_async_copy(cv_hbm.at[pid], vbuf.at[slot, p], sem.at[slot, 1]))

    def fetch(of_step, do, unrolled):
        def one(p):
            for c in page_copies(of_step, p):
                do(c)
        if unrolled:
            for p in range(pages):
                one(p)
        else:
            lax.fori_loop(0, pages, lambda p, _: (one(p), 0)[1], 0)

    start, wait = (lambda c: c.start()), (lambda c: c.wait())

    @pl.when(step == 0)
    def _():
        fetch(step, start, unrolled=False)

    following = lax.rem(step + 1, n_steps)
    fetch(following, start, unrolled=True)
    lam = _lambda(lq1_ref, lk1_ref, lq2_ref, lk2_ref)
    prompt = _prompt_tile(i, lam, gcol_ref[...], q_ref, k_ref, v_ref, o_ref, tq)
    next(prompt)
    fetch(step, wait, unrolled=True)
    slot = step % 2
    decode = _decode_step(step % steps_per_seq, steps_per_seq, lam, grow_ref[...], qs_ref, kn_ref, vn_ref,
                          [kbuf.at[slot, p] for p in range(pages)], [vbuf.at[slot, p] for p in range(pages)],
                          os_ref, m_scr, l_scr, acc_scr)
    running = [prompt, decode]
    while running:
        for gen in list(running):
            if next(gen, StopIteration) is StopIteration:
                running.remove(gen)

    @pl.when(step == n_steps - 1)
    def _():
        fetch(following, wait, unrolled=False)


def _attn(q, k, v, qs, ks, vs, cache_k, cache_v, page_table, lam_vecs, g, B, S, tq):
    H = N_ATT_HEADS
    nq = S // tq
    Bs, n_pages = page_table.shape
    n_pool, page = cache_k.shape[0], cache_k.shape[1]
    steps_per_seq = (B * nq) // Bs
    assert steps_per_seq * Bs == B * nq and n_pages % steps_per_seq == 0
    P = n_pages // steps_per_seq
    ck = cache_k.reshape(n_pool, page * H, HEAD_W)
    cv = cache_v.reshape(n_pool, page * H, HEAD_W)
    seq = lambda b, i: (b * nq + i) // steps_per_seq
    vec = pl.BlockSpec((1, ATT_HEAD_DIM), lambda b, i, pt: (0, 0))
    kv = pl.BlockSpec((S * H, HEAD_W), lambda b, i, pt: (b, 0))
    qo = pl.BlockSpec((tq, D_ATT), lambda b, i, pt: (b * nq + i, 0))
    tok = pl.BlockSpec((None, H, HEAD_W), lambda b, i, pt: (seq(b, i), 0, 0))
    hbm = pl.BlockSpec(memory_space=pl.ANY)
    page_buf = pltpu.VMEM((2, P, page * H, HEAD_W), F32)
    grid_spec = pltpu.PrefetchScalarGridSpec(
        num_scalar_prefetch=1,
        grid=(B, nq),
        in_specs=[vec, vec, vec, vec, pl.BlockSpec((HEAD_W, 1), lambda b, i, pt: (0, 0)),
                  pl.BlockSpec((1, HEAD_W), lambda b, i, pt: (0, 0)), qo, kv, kv, tok, tok, tok, hbm, hbm],
        out_specs=[qo, tok],
        scratch_shapes=[pltpu.VMEM((2 * H, 1), F32), pltpu.VMEM((2 * H, 1), F32), pltpu.VMEM((2 * H, HEAD_W), F32),
                        page_buf, page_buf, pltpu.SemaphoreType.DMA((2, 2))],
    )
    tok3 = lambda a: a.reshape(Bs, H, HEAD_W)
    o, o_s = pl.pallas_call(
        functools.partial(_attn_kernel, tq=tq, steps_per_seq=steps_per_seq, pages=P),
        grid_spec=grid_spec,
        out_shape=[jax.ShapeDtypeStruct((B * S, D_ATT), F32), jax.ShapeDtypeStruct((Bs, H, HEAD_W), F32)],
        compiler_params=_params("arbitrary", "arbitrary"),
        name="attn",
    )(page_table, *lam_vecs, g.reshape(HEAD_W, 1), g.reshape(1, HEAD_W), q, k, v, tok3(qs), tok3(ks), tok3(vs), ck, cv)
    return o, o_s.reshape(Bs, D_ATT)


SCAN_COLS = 1024
B_GROUPS = 16
C_GROUPS = 8


def _ssm_kernel(u_ref, bb_ref, lre_ref, lim_ref, c_ref, d_ref, wglu_ref, *refs, tc, bk, fresh):
    h0_refs, (z_ref, hre_ref, him_ref, bu_scr, hs_scr) = refs[:-5], refs[-5:]
    t0 = pl.program_id(0)

    @pl.when(t0 == 0)
    def _():
        for h_ref, h0_ref in zip((hre_ref, him_ref), h0_refs or (None, None)):
            h_ref[...] = jnp.zeros(h_ref.shape, F32) if fresh else h0_ref[...]

    rows = tc * bk
    u = u_ref[...].reshape(rows, D_SSM)
    ub = u.astype(BF16)
    part_w = B_GROUPS * SSM_STATE
    half_w = 2 * part_w
    n_half = N_SSM_GROUPS // B_GROUPS
    slab_w = C_GROUPS * SSM_STATE

    def input_map(half):
        u_half = ub[:, half * B_GROUPS * SSM_GROUP:(half + 1) * B_GROUPS * SSM_GROUP]
        bu = jnp.dot(u_half, bb_ref[half], preferred_element_type=F32)
        bu_scr[:, :, half * half_w:(half + 1) * half_w] = bu.reshape(tc, bk, half_w)

    def scan(half):
        for cg in range(part_w // SCAN_COLS):
            st = slice(half * part_w + cg * SCAN_COLS, half * part_w + (cg + 1) * SCAN_COLS)
            re = slice(half * half_w + cg * SCAN_COLS, half * half_w + (cg + 1) * SCAN_COLS)
            im = slice(re.start + part_w, re.stop + part_w)
            lr = jnp.broadcast_to(lre_ref[:, st], (bk, SCAN_COLS))
            li = jnp.broadcast_to(lim_ref[:, st], (bk, SCAN_COLS))
            hr, hi = hre_ref[:, st], him_ref[:, st]
            for t in range(tc):
                hr, hi = lr * hr - li * hi + bu_scr[t, :, re], lr * hi + li * hr + bu_scr[t, :, im]
                hs_scr[t, :, re] = hr
                hs_scr[t, :, im] = hi
            hre_ref[:, st] = hr
            him_ref[:, st] = hi

    def output_map(s):
        re0 = (s * C_GROUPS // B_GROUPS) * half_w + (s * C_GROUPS % B_GROUPS) * SSM_STATE
        h_re = hs_scr[:, :, re0:re0 + slab_w].reshape(rows, slab_w).astype(BF16)
        h_im = hs_scr[:, :, re0 + part_w:re0 + part_w + slab_w].reshape(rows, slab_w).astype(BF16)
        return (jnp.dot(h_re, c_ref[s, 0], preferred_element_type=F32)
                + jnp.dot(h_im, c_ref[s, 1], preferred_element_type=F32))

    slabs_per_half = B_GROUPS // C_GROUPS
    input_map(0)
    y_slabs = []
    for half in range(n_half):
        if half + 1 < n_half:
            input_map(half + 1)
        scan(half)
        y_slabs += [output_map(half * slabs_per_half + s) for s in range(slabs_per_half)]
    y = jnp.concatenate(y_slabs, axis=1) + d_ref[...] * u
    y = jax.nn.gelu(y)
    gate = jnp.dot(y.astype(BF16), wglu_ref[...], preferred_element_type=F32)
    z_ref[...] = (y * jax.nn.sigmoid(gate)).reshape(tc, bk, D_SSM)


def _ssm(u3, bb, lb_re, lb_im, c_full, d, w_glu, h0, tc):
    S, bk, _ = u3.shape
    const = lambda shape: pl.BlockSpec(shape, lambda t: (0,) * len(shape))
    seq = pl.BlockSpec((tc, bk, D_SSM), lambda t: (t, 0, 0))
    h0 = tuple(h0 or ())
    z, h_re, h_im = pl.pallas_call(
        functools.partial(_ssm_kernel, tc=tc, bk=bk, fresh=not h0),
        grid=(S // tc,),
        in_specs=[seq, const(bb.shape), const((1, N_STATE)), const((1, N_STATE)),
                  const(c_full.shape), const((1, D_SSM)), const((D_SSM, D_SSM))] + [const((bk, N_STATE))] * len(h0),
        out_specs=[seq, const((bk, N_STATE)), const((bk, N_STATE))],
        out_shape=[jax.ShapeDtypeStruct((S, bk, D_SSM), F32), jax.ShapeDtypeStruct((bk, N_STATE), F32),
                   jax.ShapeDtypeStruct((bk, N_STATE), F32)],
        scratch_shapes=[pltpu.VMEM((tc, bk, 2 * N_STATE), F32), pltpu.VMEM((tc, bk, 2 * N_STATE), F32)],
        compiler_params=_params("arbitrary"),
        name="ssm",
    )(u3, bb, lb_re.reshape(1, N_STATE), lb_im.reshape(1, N_STATE), c_full, d.reshape(1, D_SSM), w_glu, *h0)
    return z, h_re, h_im


def _ffn_kernel(*refs, tm, decode):
    if decode:
        (x_ref, o_ref, z_ref, wo_ref, gn_ref, wu_ref, cw_ref, cb_ref, wd_ref, fn_ref, p2_ref, p1_ref,
         y_ref, a_out_ref, hid_scr) = refs
    else:
        (x_ref, o_ref, z_ref, wo_ref, gn_ref, wu_ref, cw_ref, cb_ref, wd_ref, fn_ref,
         y_ref, cs_ref, hid_scr, carry_scr) = refs

        @pl.when(pl.program_id(1) == 0)
        def _():
            carry_scr[...] = jnp.zeros(carry_scr.shape, F32)

    mix = (jnp.dot(o_ref[...].astype(BF16), wo_ref[:D_ATT, :], preferred_element_type=F32)
           + jnp.dot(z_ref[...].astype(BF16), wo_ref[D_ATT:, :], preferred_element_type=F32))
    x1 = x_ref[...] + mix
    r = lax.rsqrt(jnp.mean(x1 * x1, axis=-1, keepdims=True) + NORM_EPS)
    hn = (x1 * r * gn_ref[...]).astype(BF16)
    for j in range(N_FF_CHUNKS):
        cols = slice(j * FF_CHUNK, (j + 1) * FF_CHUNK)
        gcols = slice(D_FF + j * FF_CHUNK, D_FF + (j + 1) * FF_CHUNK)
        a = jnp.dot(hn, wu_ref[:, cols], preferred_element_type=F32)
        gate = jnp.dot(hn, wu_ref[:, gcols], preferred_element_type=F32)
        if decode:
            am2, am1 = p2_ref[:, cols], p1_ref[:, cols]
            a_out_ref[:, cols] = a
        else:
            prev = carry_scr[:, cols]
            p2 = jnp.broadcast_to(prev[CARRY_ROWS - 2:CARRY_ROWS - 1], a.shape)
            p1 = jnp.broadcast_to(prev[CARRY_ROWS - 1:], a.shape)
            row = lax.broadcasted_iota(jnp.int32, a.shape, 0)
            am1 = jnp.where(row < 1, p1, pltpu.roll(a, 1, 0))
            am2 = jnp.where(row < 1, p2, jnp.where(row < 2, p1, pltpu.roll(a, 2, 0)))
            carry_scr[:, cols] = a[tm - CARRY_ROWS:]
        cw = cw_ref[:, cols]
        c = cb_ref[:, cols] + (cw[0:1] * am2 + cw[1:2] * am1 + cw[2:3] * a)
        hid_scr[:, cols] = (jax.nn.silu(c) * gate).astype(BF16)
    acc = x1 + jnp.dot(hid_scr[...], wd_ref[...], preferred_element_type=F32)
    r2 = lax.rsqrt(jnp.mean(acc * acc, axis=-1, keepdims=True) + NORM_EPS)
    y_ref[...] = acc * r2 * fn_ref[...]
    if not decode:
        cs_ref[...] = carry_scr[CARRY_ROWS - (CONV_W - 1):, :]


def _ffn(x, o, z, weights, tm, prev=None):
    B, S, D = x.shape
    ns = S // tm
    decode = prev is not None
    wo, gn, wu, cw, cb, wd, fn = weights
    const = lambda shape: pl.BlockSpec(shape, lambda b, s: (0,) * len(shape), pipeline_mode=pl.Buffered(1))
    xs = pl.BlockSpec((None, tm, D), lambda b, s: (b, s, 0))
    in_specs = [xs, pl.BlockSpec((tm, D_ATT), lambda b, s: (b * ns + s, 0)),
                pl.BlockSpec((tm, D_SSM), lambda b, s: (s, b)),
                const((D, D)), const((1, D)), const((D, 2 * D_FF)), const((CONV_W, D_FF)), const((1, D_FF)),
                const((D_FF, D)), const((1, D))]
    args = [x, o, z, wo, gn, wu, cw, cb, wd, fn]
    scratch = [pltpu.VMEM((tm, D_FF), BF16)]
    if decode:
        rows = pl.BlockSpec((tm, D_FF), lambda b, s: (0, 0))
        in_specs += [rows, rows]
        args += list(prev)
        out_specs = [xs, rows]
        out_shape = [jax.ShapeDtypeStruct((B, S, D), F32), jax.ShapeDtypeStruct((tm, D_FF), F32)]
    else:
        out_specs = [xs, pl.BlockSpec((None, CONV_W - 1, D_FF), lambda b, s: (b, 0, 0))]
        out_shape = [jax.ShapeDtypeStruct((B, S, D), F32), jax.ShapeDtypeStruct((B, CONV_W - 1, D_FF), F32)]
        scratch.append(pltpu.VMEM((CARRY_ROWS, D_FF), F32))
    y, extra = pl.pallas_call(
        functools.partial(_ffn_kernel, tm=tm, decode=decode),
        grid=(B, ns),
        in_specs=in_specs, out_specs=out_specs, out_shape=out_shape, scratch_shapes=scratch,
        compiler_params=_params("parallel", "arbitrary"),
        name="ffn_decode" if decode else "ffn",
    )(*args)
    return y, extra


def kernel(x_prompt, x_sample, cache_k, cache_v, state_ssm_re, state_ssm_im, state_conv, page_table, norm_mix, w_in, lambda_q1, lambda_k1, lambda_q2, lambda_k2, subln_g, ssm_a_re, ssm_a_im, ssm_log_dt, ssm_b_re, ssm_b_im, ssm_c_re, ssm_c_im, ssm_d, w_glu, w_o, norm_ffn, w_up, conv_w, conv_b, w_down, final_norm):
    assert norm_mix.shape[0] == 1, "single layer"
    Bp, Sp, D = x_prompt.shape
    Bs = x_sample.shape[0]
    G, N, C = N_SSM_GROUPS, SSM_STATE, SSM_GROUP
    past_len = page_table.shape[1] * cache_k.shape[2]

    lam_vecs = [v.reshape(1, ATT_HEAD_DIM) for v in (lambda_q1[0], lambda_k1[0], lambda_q2[0], lambda_k2[0])]
    g_sub = subln_g[0].reshape(1, HEAD_W)

    lb_re, lb_im, bb_blk, c_blk = _ssm_prep(ssm_a_re[0], ssm_a_im[0], ssm_log_dt[0], ssm_b_re[0], ssm_b_im[0],
                                            ssm_c_re[0], ssm_c_im[0])

    tabs_p = _rope_tables(np.arange(Sp))
    q, k, v, u, (wo_b, wglu_b, wup_b, wdown_b) = _in_proj(
        x_prompt, norm_mix[0], w_in[0], tabs_p, tm=IN_PROJ_ROWS, cast=(w_o[0], w_glu[0], w_up[0], w_down[0]))
    ffn_w = (wo_b, norm_ffn[0].reshape(1, D), wup_b, conv_w[0], conv_b[0].reshape(1, D_FF), wdown_b,
             final_norm.reshape(1, D))
    tabs_s = _rope_tables(np.full((Bs,), past_len))
    qs, ks, vs, us, _ = _in_proj(x_sample.reshape(1, Bs, D), norm_mix[0], w_in[0], tabs_s, tm=Bs)

    o, os_ = _attn(q, k, v, qs, ks, vs, cache_k[0], cache_v[0], page_table, lam_vecs, g_sub, Bp, Sp, tq=ATTN_Q_ROWS)

    z, hre_p, him_p = _ssm(u.reshape(Sp, Bp, D_SSM), bb_blk, lb_re, lb_im, c_blk, ssm_d[0], wglu_b, None,
                           tc=SSM_STEPS)
    y_prompt, cs_p = _ffn(x_prompt, o, z.reshape(Sp, Bp * D_SSM), ffn_w, tm=FFN_ROWS)
    conv_prompt = cs_p

    zs, hre_s, him_s = _ssm(us.reshape(1, Bs, D_SSM), bb_blk, lb_re, lb_im, c_blk, ssm_d[0], wglu_b,
                            (state_ssm_re[0].reshape(Bs, N_STATE), state_ssm_im[0].reshape(Bs, N_STATE)), tc=1)
    sc = state_conv[0]
    y_s, a_s = _ffn(x_sample.reshape(1, Bs, D), os_, zs.reshape(Bs, D_SSM), ffn_w, tm=Bs, prev=(sc[:, 0], sc[:, 1]))
    conv_sample = jnp.stack([sc[:, 1], a_s], axis=1)

    st = lambda h, b: h.reshape(1, b, G, N)
    return (y_prompt, y_s.reshape(Bs, 1, D),
            k.reshape(1, Bp, Sp, N_ATT_HEADS, HEAD_W), v.reshape(1, Bp, Sp, N_ATT_HEADS, HEAD_W),
            st(hre_p, Bp), st(him_p, Bp), conv_prompt[None],
            ks.reshape(1, Bs, 1, N_ATT_HEADS, HEAD_W), vs.reshape(1, Bs, 1, N_ATT_HEADS, HEAD_W),
            st(hre_s, Bs), st(him_s, Bs), conv_sample[None])
```

```python
import functools
import math

import jax
import jax.numpy as jnp
import numpy as np
from jax import lax
from jax.experimental import pallas as pl
from jax.experimental.pallas import tpu as pltpu

F32 = jnp.float32
BF16 = jnp.bfloat16

D_MODEL = 1024
D_ATT = 512
D_SSM = 512
ATT_HEAD_DIM = 64
N_ATT_HEADS = 4
HEAD_W = 2 * ATT_HEAD_DIM
ROT_DIM = 16
ROPE_THETA = 500000.0
SSM_GROUP = 16
N_SSM_GROUPS = 32
SSM_STATE = 64
N_STATE = N_SSM_GROUPS * SSM_STATE
D_FF = 2816
CONV_W = 3
NORM_EPS = 1e-6
SUBLN_EPS = 1e-5
NEG_INF = -1e30
D_PROJ = 3 * D_ATT + D_SSM
LAMBDA_INIT = 0.8 - 0.6 * math.exp(-0.3 * 0)
LOG2_E = math.log2(math.e)

V7X_VMEM_LIMIT = 56 * 1024 * 1024
IN_PROJ_ROWS = 1024
ATTN_Q_ROWS = 256
SSM_STEPS = 128
FFN_ROWS = 512
FF_CHUNK = 256
N_FF_CHUNKS = D_FF // FF_CHUNK
BF16_SUBLANES = 16
DECODE_GROUPS = 4
CARRY_ROWS = 8


def _params(*sem):
    return pltpu.CompilerParams(dimension_semantics=sem, vmem_limit_bytes=V7X_VMEM_LIMIT)


def _convert_specs(weights, n_steps, step_of):
    assert all(w.shape[0] % (n_steps * BF16_SUBLANES) == 0 for w in weights)
    return [pl.BlockSpec((w.shape[0] // n_steps, w.shape[1]), lambda *idx: (step_of(*idx), 0)) for w in weights]


def _convert_rows(src_refs, dst_refs):
    for src_ref, dst_ref in zip(src_refs, dst_refs):
        dst_ref[...] = src_ref[...].astype(BF16)


def _lambda(lq1_ref, lk1_ref, lq2_ref, lk2_ref):
    s1 = jnp.sum(lq1_ref[...] * lk1_ref[...], axis=-1, keepdims=True)
    s2 = jnp.sum(lq2_ref[...] * lk2_ref[...], axis=-1, keepdims=True)
    return jnp.exp(s1) - jnp.exp(s2) + LAMBDA_INIT


def _sub_ln(o, g):
    r = lax.rsqrt(jnp.mean(o * o, axis=-1, keepdims=True) + SUBLN_EPS)
    return (o * r * g) * (1.0 - LAMBDA_INIT)


def _ssm_prep_kernel(ar_ref, ai_ref, ldt_ref, arr_ref, air_ref, ldtr_ref, br_ref, bi_ref, cre_ref, cim_ref,
                     lbre_ref, lbim_ref, bb_ref, c_ref):
    G, N, C = N_SSM_GROUPS, SSM_STATE, SSM_GROUP

    def disc(ar, ai, dt):
        mag = jnp.exp(ar * dt)
        return mag * jnp.cos(ai * dt), mag * jnp.sin(ai * dt)

    lb_re, lb_im = disc(ar_ref[...], ai_ref[...], jnp.exp(ldt_ref[...]))
    lbre_ref[...] = lb_re
    lbim_ref[...] = lb_im
    ar, ai = arr_ref[...], air_ref[...]
    t_re, t_im = disc(ar, ai, jnp.exp(ldtr_ref[...]))
    den = ar * ar + ai * ai
    nr, ni = t_re - 1.0, t_im
    f_re = (nr * ar + ni * ai) / den
    f_im = (ni * ar - nr * ai) / den
    br, bi = br_ref[...], bi_ref[...]
    bb = ((f_re * br - f_im * bi).astype(BF16), (f_re * bi + f_im * br).astype(BF16))

    def iota(shape, axis):
        return lax.broadcasted_iota(jnp.int32, shape, axis)

    half_rows, part_w = B_GROUPS * C, B_GROUPS * N
    spread = (iota((N, part_w), 1) % N == iota((N, part_w), 0)).astype(BF16)
    own = iota((half_rows, part_w), 0) // C == iota((half_rows, part_w), 1) // N
    for half in range(G // B_GROUPS):
        for part in range(2):
            tiled = jnp.dot(bb[part][half * half_rows:(half + 1) * half_rows], spread, preferred_element_type=F32)
            bb_ref[half, :, part * part_w:(part + 1) * part_w] = jnp.where(own, tiled, 0.0).astype(BF16)

    slab_rows, slab_w = C_GROUPS * N, C_GROUPS * C
    spread_c = (iota((C, slab_w), 1) % C == iota((C, slab_w), 0)).astype(BF16)
    own_c = (iota((G * N, slab_w), 0) // N) % C_GROUPS == iota((G * N, slab_w), 1) // C
    for part, (c_rows_ref, sign) in enumerate(((cre_ref, 1.0), (cim_ref, -1.0))):
        tiled = jnp.dot(c_rows_ref[...].astype(BF16), spread_c, preferred_element_type=F32)
        blk = jnp.where(own_c, sign * tiled, 0.0).astype(BF16)
        for s in range(G // C_GROUPS):
            c_ref[s, part] = blk[s * slab_rows:(s + 1) * slab_rows]


def _ssm_prep(a_re, a_im, log_dt, b_re, b_im, c_re, c_im):
    G, N, C = N_SSM_GROUPS, SSM_STATE, SSM_GROUP
    per_channel = lambda a: jnp.repeat(a, C, axis=0)
    b_rows = lambda b: jnp.transpose(b, (0, 2, 1)).reshape(G * C, N)
    c_rows = lambda c: jnp.transpose(c, (0, 2, 1)).reshape(G * N, C)
    ldt = log_dt.reshape(G, 1)
    return pl.pallas_call(
        _ssm_prep_kernel,
        out_shape=[jax.ShapeDtypeStruct((G, N), F32)] * 2
                  + [jax.ShapeDtypeStruct((G // B_GROUPS, B_GROUPS * C, 2 * B_GROUPS * N), BF16),
                     jax.ShapeDtypeStruct((G // C_GROUPS, 2, C_GROUPS * N, C_GROUPS * C), BF16)],
        name="ssm_prep",
    )(a_re, a_im, ldt, per_channel(a_re), per_channel(a_im), per_channel(ldt), b_rows(b_re), b_rows(b_im),
      c_rows(c_re), c_rows(c_im))


def _in_proj_kernel(x_ref, g_ref, w_ref, cos_ref, sa_ref, sb_ref, *rest):
    n_cast = (len(rest) - 4) // 2
    q_ref, k_ref, v_ref, u_ref = rest[n_cast:n_cast + 4]
    _convert_rows(rest[:n_cast], rest[n_cast + 4:])
    tm = x_ref.shape[0]
    x = x_ref[...]
    r = lax.rsqrt(jnp.mean(x * x, axis=-1, keepdims=True) + NORM_EPS)
    hn = (x * r * g_ref[...]).astype(BF16)
    proj = jnp.dot(hn, w_ref[...].astype(BF16), preferred_element_type=F32)
    cos, sa, sb = cos_ref[...], sa_ref[...], sb_ref[...]
    half = ROT_DIM // 2

    def rope(xh):
        return xh * cos + pltpu.roll(xh, HEAD_W - half, 1) * sa + pltpu.roll(xh, half, 1) * sb

    for h in range(N_ATT_HEADS):
        lo, hi = h * HEAD_W, (h + 1) * HEAD_W
        q_ref[:, lo:hi] = rope(proj[:, lo:hi]) * (ATT_HEAD_DIM ** -0.5)
        k_ref[pl.ds(h, tm, stride=N_ATT_HEADS), :] = rope(proj[:, D_ATT + lo:D_ATT + hi])
        v_ref[pl.ds(h, tm, stride=N_ATT_HEADS), :] = proj[:, 2 * D_ATT + lo:2 * D_ATT + hi]
    u_ref[...] = proj[:, 3 * D_ATT:]


def _rope_tables(pos):
    half = ROT_DIM // 2
    inv = ROPE_THETA ** (-np.arange(0, ROT_DIM, 2, dtype=np.float64) / ROT_DIM)
    ang = np.asarray(pos, np.float64)[:, None] * inv[None, :]
    cos, sin = np.cos(ang), np.sin(ang)
    S = ang.shape[0]
    pad = np.zeros((S, ATT_HEAD_DIM - ROT_DIM))
    zh = np.zeros((S, half))
    comp = lambda a, b, c: np.concatenate([a, b, c], axis=1)
    two = lambda t: jnp.asarray(np.concatenate([t, t], axis=1), F32)
    return (two(comp(cos, cos, pad + 1.0)), two(comp(-sin, zh, pad)), two(comp(zh, sin, pad)))


def _in_proj(x, g, w, tables, tm, cast=()):
    B, S, D = x.shape
    ns = S // tm
    cast_specs = _convert_specs(cast, B * ns, lambda b, s: b * ns + s)
    row = lambda b, s: (b * ns + s, 0)
    tab = pl.BlockSpec((tm, HEAD_W), lambda b, s: (s, 0))
    q_spec = pl.BlockSpec((tm, D_ATT), row)
    kv_spec = pl.BlockSpec((tm * N_ATT_HEADS, HEAD_W), row)
    kv_shape = jax.ShapeDtypeStruct((B * S * N_ATT_HEADS, HEAD_W), F32)
    q, k, v, u, *converted = pl.pallas_call(
        _in_proj_kernel,
        grid=(B, ns),
        in_specs=[pl.BlockSpec((None, tm, D), lambda b, s: (b, s, 0)),
                  pl.BlockSpec((1, D), lambda b, s: (0, 0)),
                  pl.BlockSpec((D, D_PROJ), lambda b, s: (0, 0)),
                  tab, tab, tab] + cast_specs,
        out_specs=[q_spec, kv_spec, kv_spec, pl.BlockSpec((tm, D_SSM), lambda b, s: (s, b))] + cast_specs,
        out_shape=[jax.ShapeDtypeStruct((B * S, D_ATT), F32), kv_shape, kv_shape,
                   jax.ShapeDtypeStruct((S, B * D_SSM), F32)] + [jax.ShapeDtypeStruct(c.shape, BF16) for c in cast],
        compiler_params=_params("parallel", "parallel"),
        name="in_proj",
    )(x, g.reshape(1, D), w, *tables, *cast)
    return q, k, v, u, converted


def _prompt_tile(i, lam, g_col, q_ref, k_ref, v_ref, o_ref, tq):
    nh = N_ATT_HEADS
    feat = lax.broadcasted_iota(jnp.int32, (HEAD_W, tq), 0)
    units = [(h, c) for h in range(nh) for c in range(2)]
    hs = lambda h: slice(h * HEAD_W, (h + 1) * HEAD_W)
    q_t = []
    for h, c in units:
        qh_t = (q_ref[:, hs(h)] * LOG2_E).T
        own = (feat < ATT_HEAD_DIM) if c == 0 else (feat >= ATT_HEAD_DIM)
        q_t.append(jnp.where(own, qh_t, 0.0))

    ones_rows = jnp.ones((8, tq), F32)

    n = range(len(units))
    rows = lambda j, h: pl.ds(pl.multiple_of(j * tq, tq) * nh + h, tq, stride=nh)

    def scores(j, masked):
        s = [jnp.dot(k_ref[rows(j, h), :], q_t[u], preferred_element_type=F32)
             for u, (h, c) in enumerate(units)]
        if masked:
            key = lax.broadcasted_iota(jnp.int32, (tq, tq), 0)
            qry = lax.broadcasted_iota(jnp.int32, (tq, tq), 1)
            s = [jnp.where(key <= qry, su, NEG_INF) for su in s]
        return s

    def softmax(s, carry):
        m_new = [jnp.maximum(carry[u][0], jnp.max(s[u], axis=0, keepdims=True)) for u in n]
        alpha = [jnp.exp2(carry[u][0] - m_new[u]) for u in n]
        return m_new, alpha, [jnp.exp2(s[u] - m_new[u]) for u in n]

    def update(j, carry, m_new, alpha, p):
        pv = [jnp.dot(jnp.concatenate([v_ref[rows(j, h), :].T, ones_rows], axis=0), p[u],
                      preferred_element_type=F32) for u, (h, c) in enumerate(units)]
        return tuple((m_new[u], alpha[u] * carry[u][1] + pv[u][HEAD_W:HEAD_W + 1],
                      alpha[u] * carry[u][2] + pv[u][:HEAD_W]) for u in n)

    init = tuple((jnp.full((1, tq), NEG_INF, F32), jnp.zeros((1, tq), F32), jnp.zeros((HEAD_W, tq), F32))
                 for _ in units)
    carry = lax.fori_loop(0, i, lambda j, c: update(j, c, *softmax(scores(j, False), c)), init)
    yield
    s = scores(i, True)
    yield
    stats = softmax(s, carry)
    yield
    carry = update(i, carry, *stats)
    for h in range(nh):
        (_, l1, acc1), (_, l2, acc2) = carry[2 * h], carry[2 * h + 1]
        o_t = acc1 / l1 - lam * (acc2 / l2)
        r = lax.rsqrt(jnp.mean(o_t * o_t, axis=0, keepdims=True) + SUBLN_EPS)
        o_ref[:, h * HEAD_W:(h + 1) * HEAD_W] = ((o_t * r * g_col) * (1.0 - LAMBDA_INIT)).T


def _decode_step(j, n_steps, lam, g_row, q_ref, kn_ref, vn_ref, k_refs, v_refs, o_ref, m_scr, l_scr, acc_scr):
    H = N_ATT_HEADS

    @pl.when(j == 0)
    def _():
        m_scr[...] = jnp.full(m_scr.shape, NEG_INF, F32)
        l_scr[...] = jnp.zeros(l_scr.shape, F32)
        acc_scr[...] = jnp.zeros(acc_scr.shape, F32)

    q = q_ref[...]
    lane = lax.broadcasted_iota(jnp.int32, q.shape, 1)
    q8 = jnp.concatenate([jnp.where(lane < ATT_HEAD_DIM, q, 0.0), jnp.where(lane >= ATT_HEAD_DIM, q, 0.0)], axis=0)
    rows = k_refs[0].shape[0]
    zero = jnp.zeros_like(q8)
    pad = jnp.zeros((HEAD_W - 4 * H, 2 * HEAD_W), F32)
    q_pair = jnp.concatenate([jnp.concatenate([q8, zero], axis=1), jnp.concatenate([zero, q8], axis=1), pad], axis=0)
    r = lax.broadcasted_iota(jnp.int32, (4 * H, rows), 0)
    col = lax.broadcasted_iota(jnp.int32, (4 * H, rows), 1)
    own = (col % H) == (r % H)
    both = lambda x: jnp.concatenate([x, x], axis=0)
    halves = lambda x, op: op(x[:2 * H], x[2 * H:])

    def scores(group):
        k_pairs = jnp.concatenate([jnp.concatenate([k_refs[a][...], k_refs[a + 1][...]], axis=1) for a in group],
                                  axis=0)
        s_t = lax.dot_general(k_pairs, q_pair, (((1,), (1,)), ((), ())), preferred_element_type=F32)
        return jnp.concatenate([jnp.where(own, s_t[n * rows:(n + 1) * rows].T[:4 * H], NEG_INF)
                                for n in range(len(group))], axis=1)

    def weighted_values(group, p):
        pv = None
        for n, a in enumerate(group):
            blk = slice(n * rows, (n + 1) * rows)
            part = (jnp.dot(p[:2 * H, blk], v_refs[a][...], preferred_element_type=F32)
                    + jnp.dot(p[2 * H:, blk], v_refs[a + 1][...], preferred_element_type=F32))
            pv = part if pv is None else pv + part
        return pv

    pairs = list(range(0, len(k_refs), 2))
    groups = [pairs[g::DECODE_GROUPS] for g in range(DECODE_GROUPS)]
    s = [scores(g) for g in groups]
    yield
    m_g = [halves(jnp.max(sg, axis=-1, keepdims=True), jnp.maximum) for sg in s]
    p = [jnp.exp(sg - both(mg)) for sg, mg in zip(s, m_g)]
    l_g = [halves(jnp.sum(pg, axis=-1, keepdims=True), jnp.add) for pg in p]
    yield
    pv_g = [weighted_values(g, pg) for g, pg in zip(groups, p)]
    m = m_scr[...]
    m_new = functools.reduce(jnp.maximum, m_g, m)
    alpha = jnp.exp(m - m_new)
    l, acc = alpha * l_scr[...], alpha * acc_scr[...]
    for mg, lg, pvg in zip(m_g, l_g, pv_g):
        w = jnp.exp(mg - m_new)
        l, acc = l + w * lg, acc + w * pvg
    m_scr[...] = m_new
    l_scr[...] = l
    acc_scr[...] = acc

    @pl.when(j == n_steps - 1)
    def _():
        kn = jnp.concatenate([kn_ref[...], kn_ref[...]], axis=0)
        vn = jnp.concatenate([vn_ref[...], vn_ref[...]], axis=0)
        s_new = jnp.sum(q8 * kn, axis=-1, keepdims=True)
        m_old = m_scr[...]
        m_fin = jnp.maximum(m_old, s_new)
        a = jnp.exp(m_old - m_fin)
        p_new = jnp.exp(s_new - m_fin)
        l = a * l_scr[...] + p_new
        acc = (a * acc_scr[...] + p_new * vn) / l
        o_ref[...] = _sub_ln(acc[:H] - lam * acc[H:], g_row)


def _attn_kernel(pt_ref, lq1_ref, lk1_ref, lq2_ref, lk2_ref, gcol_ref, grow_ref, q_ref, k_ref, v_ref,
                 qs_ref, kn_ref, vn_ref, ck_hbm, cv_hbm, o_ref, os_ref, m_scr, l_scr, acc_scr, kbuf, vbuf, sem,
                 *, tq, steps_per_seq, pages):
    i = pl.program_id(1)
    step = pl.program_id(0) * pl.num_programs(1) + i
    n_steps = pl.num_programs(0) * pl.num_programs(1)

    def page_copies(of_step, p):
        slot = of_step % 2
        pid = pt_ref[of_step // steps_per_seq, (of_step % steps_per_seq) * pages + p]
        return (pltpu.make_async_copy(ck_hbm.at[pid], kbuf.at[slot, p], sem.at[slot, 0]),
                pltpu.make_async_copy(cv_hbm.at[pid], vbuf.at[slot, p], sem.at[slot, 1]))

    def fetch(of_step, do, unrolled):
        def one(p):
            for c in page_copies(of_step, p):
                do(c)
        if unrolled:
            for p in range(pages):
                one(p)
        else:
            lax.fori_loop(0, pages, lambda p, _: (one(p), 0)[1], 0)

    start, wait = (lambda c: c.start()), (lambda c: c.wait())

    @pl.when(step == 0)
    def _():
        fetch(step, start, unrolled=False)

    following = lax.rem(step + 1, n_steps)
    fetch(following, start, unrolled=True)
    lam = _lambda(lq1_ref, lk1_ref, lq2_ref, lk2_ref)
    prompt = _prompt_tile(i, lam, gcol_ref[...], q_ref, k_ref, v_ref, o_ref, tq)
    next(prompt)
    fetch(step, wait, unrolled=True)
    slot = step % 2
    decode = _decode_step(step % steps_per_seq, steps_per_seq, lam, grow_ref[...], qs_ref, kn_ref, vn_ref,
                          [kbuf.at[slot, p] for p in range(pages)], [vbuf.at[slot, p] for p in range(pages)],
                          os_ref, m_scr, l_scr, acc_scr)
    running = [prompt, decode]
    while running:
        for gen in list(running):
            if next(gen, StopIteration) is StopIteration:
                running.remove(gen)

    @pl.when(step == n_steps - 1)
    def _():
        fetch(following, wait, unrolled=False)


def _attn(q, k, v, qs, ks, vs, cache_k, cache_v, page_table, lam_vecs, g, B, S, tq):
    H = N_ATT_HEADS
    nq = S // tq
    Bs, n_pages = page_table.shape
    n_pool, page = cache_k.shape[0], cache_k.shape[1]
    steps_per_seq = (B * nq) // Bs
    assert steps_per_seq * Bs == B * nq and n_pages % steps_per_seq == 0
    P = n_pages // steps_per_seq
    ck = cache_k.reshape(n_pool, page * H, HEAD_W)
    cv = cache_v.reshape(n_pool, page * H, HEAD_W)
    seq = lambda b, i: (b * nq + i) // steps_per_seq
    vec = pl.BlockSpec((1, ATT_HEAD_DIM), lambda b, i, pt: (0, 0))
    kv = pl.BlockSpec((S * H, HEAD_W), lambda b, i, pt: (b, 0))
    qo = pl.BlockSpec((tq, D_ATT), lambda b, i, pt: (b * nq + i, 0))
    tok = pl.BlockSpec((None, H, HEAD_W), lambda b, i, pt: (seq(b, i), 0, 0))
    hbm = pl.BlockSpec(memory_space=pl.ANY)
    page_buf = pltpu.VMEM((2, P, page * H, HEAD_W), F32)
    grid_spec = pltpu.PrefetchScalarGridSpec(
        num_scalar_prefetch=1,
        grid=(B, nq),
        in_specs=[vec, vec, vec, vec, pl.BlockSpec((HEAD_W, 1), lambda b, i, pt: (0, 0)),
                  pl.BlockSpec((1, HEAD_W), lambda b, i, pt: (0, 0)), qo, kv, kv, tok, tok, tok, hbm, hbm],
        out_specs=[qo, tok],
        scratch_shapes=[pltpu.VMEM((2 * H, 1), F32), pltpu.VMEM((2 * H, 1), F32), pltpu.VMEM((2 * H, HEAD_W), F32),
                        page_buf, page_buf, pltpu.SemaphoreType.DMA((2, 2))],
    )
    tok3 = lambda a: a.reshape(Bs, H, HEAD_W)
    o, o_s = pl.pallas_call(
        functools.partial(_attn_kernel, tq=tq, steps_per_seq=steps_per_seq, pages=P),
        grid_spec=grid_spec,
        out_shape=[jax.ShapeDtypeStruct((B * S, D_ATT), F32), jax.ShapeDtypeStruct((Bs, H, HEAD_W), F32)],
        compiler_params=_params("arbitrary", "arbitrary"),
        name="attn",
    )(page_table, *lam_vecs, g.reshape(HEAD_W, 1), g.reshape(1, HEAD_W), q, k, v, tok3(qs), tok3(ks), tok3(vs), ck, cv)
    return o, o_s.reshape(Bs, D_ATT)


SCAN_COLS = 1024
B_GROUPS = 16
C_GROUPS = 8


def _ssm_kernel(u_ref, bb_ref, lre_ref, lim_ref, c_ref, d_ref, wglu_ref, *refs, tc, bk, fresh):
    n_h0 = 0 if fresh else 2
    n_cast = (len(refs) - n_h0 - 5) // 2
    h0_refs, cast_refs = refs[:n_h0], refs[n_h0:n_h0 + n_cast]
    z_ref, hre_ref, him_ref = refs[n_h0 + n_cast:n_h0 + n_cast + 3]
    bu_scr, hs_scr = refs[-2:]
    _convert_rows(cast_refs, refs[n_h0 + n_cast + 3:-2])
    t0 = pl.program_id(0)

    @pl.when(t0 == 0)
    def _():
        for h_ref, h0_ref in zip((hre_ref, him_ref), h0_refs or (None, None)):
            h_ref[...] = jnp.zeros(h_ref.shape, F32) if fresh else h0_ref[...]

    rows = tc * bk
    u = u_ref[...].reshape(rows, D_SSM)
    ub = u.astype(BF16)
    part_w = B_GROUPS * SSM_STATE
    half_w = 2 * part_w
    n_half = N_SSM_GROUPS // B_GROUPS
    slab_w = C_GROUPS * SSM_STATE

    def input_map(half):
        u_half = ub[:, half * B_GROUPS * SSM_GROUP:(half + 1) * B_GROUPS * SSM_GROUP]
        bu = jnp.dot(u_half, bb_ref[half], preferred_element_type=F32)
        bu_scr[:, :, half * half_w:(half + 1) * half_w] = bu.reshape(tc, bk, half_w)

    def scan(half):
        for cg in range(part_w // SCAN_COLS):
            st = slice(half * part_w + cg * SCAN_COLS, half * part_w + (cg + 1) * SCAN_COLS)
            re = slice(half * half_w + cg * SCAN_COLS, half * half_w + (cg + 1) * SCAN_COLS)
            im = slice(re.start + part_w, re.stop + part_w)
            lr = jnp.broadcast_to(lre_ref[:, st], (bk, SCAN_COLS))
            li = jnp.broadcast_to(lim_ref[:, st], (bk, SCAN_COLS))
            hr, hi = hre_ref[:, st], him_ref[:, st]
            for t in range(tc):
                hr, hi = lr * hr - li * hi + bu_scr[t, :, re], lr * hi + li * hr + bu_scr[t, :, im]
                hs_scr[t, :, re] = hr
                hs_scr[t, :, im] = hi
            hre_ref[:, st] = hr
            him_ref[:, st] = hi

    def output_map(s):
        re0 = (s * C_GROUPS // B_GROUPS) * half_w + (s * C_GROUPS % B_GROUPS) * SSM_STATE
        h_re = hs_scr[:, :, re0:re0 + slab_w].reshape(rows, slab_w).astype(BF16)
        h_im = hs_scr[:, :, re0 + part_w:re0 + part_w + slab_w].reshape(rows, slab_w).astype(BF16)
        return (jnp.dot(h_re, c_ref[s, 0], preferred_element_type=F32)
                + jnp.dot(h_im, c_ref[s, 1], preferred_element_type=F32))

    slabs_per_half = B_GROUPS // C_GROUPS
    input_map(0)
    y_slabs = []
    for half in range(n_half):
        if half + 1 < n_half:
            input_map(half + 1)
        scan(half)
        y_slabs += [output_map(half * slabs_per_half + s) for s in range(slabs_per_half)]
    y = jnp.concatenate(y_slabs, axis=1) + d_ref[...] * u
    y = jax.nn.gelu(y)
    gate = jnp.dot(y.astype(BF16), wglu_ref[...], preferred_element_type=F32)
    z_ref[...] = (y * jax.nn.sigmoid(gate)).reshape(tc, bk, D_SSM)


def _ssm(u3, bb, lb_re, lb_im, c_full, d, w_glu, h0, tc, cast=()):
    S, bk, _ = u3.shape
    const = lambda shape: pl.BlockSpec(shape, lambda t: (0,) * len(shape))
    seq = pl.BlockSpec((tc, bk, D_SSM), lambda t: (t, 0, 0))
    h0 = tuple(h0 or ())
    cast_specs = _convert_specs(cast, S // tc, lambda t: t)
    z, h_re, h_im, *converted = pl.pallas_call(
        functools.partial(_ssm_kernel, tc=tc, bk=bk, fresh=not h0),
        grid=(S // tc,),
        in_specs=[seq, const(bb.shape), const((1, N_STATE)), const((1, N_STATE)),
                  const(c_full.shape), const((1, D_SSM)), const((D_SSM, D_SSM))] + [const((bk, N_STATE))] * len(h0)
                 + cast_specs,
        out_specs=[seq, const((bk, N_STATE)), const((bk, N_STATE))] + cast_specs,
        out_shape=[jax.ShapeDtypeStruct((S, bk, D_SSM), F32), jax.ShapeDtypeStruct((bk, N_STATE), F32),
                   jax.ShapeDtypeStruct((bk, N_STATE), F32)] + [jax.ShapeDtypeStruct(c.shape, BF16) for c in cast],
        scratch_shapes=[pltpu.VMEM((tc, bk, 2 * N_STATE), F32), pltpu.VMEM((tc, bk, 2 * N_STATE), F32)],
        compiler_params=_params("arbitrary"),
        name="ssm",
    )(u3, bb, lb_re.reshape(1, N_STATE), lb_im.reshape(1, N_STATE), c_full, d.reshape(1, D_SSM), w_glu, *h0, *cast)
    return z, h_re, h_im, converted


def _ffn_kernel(*refs, tm, decode):
    if decode:
        (x_ref, o_ref, z_ref, wo_ref, gn_ref, wu_ref, cw_ref, cb_ref, wd_ref, fn_ref, p2_ref, p1_ref,
         y_ref, a_out_ref, hid_scr) = refs
    else:
        (x_ref, o_ref, z_ref, wo_ref, gn_ref, wu_ref, cw_ref, cb_ref, wd_ref, fn_ref,
         y_ref, cs_ref, hid_scr, carry_scr) = refs

        @pl.when(pl.program_id(1) == 0)
        def _():
            carry_scr[...] = jnp.zeros(carry_scr.shape, F32)

    mix = (jnp.dot(o_ref[...].astype(BF16), wo_ref[:D_ATT, :], preferred_element_type=F32)
           + jnp.dot(z_ref[...].astype(BF16), wo_ref[D_ATT:, :], preferred_element_type=F32))
    x1 = x_ref[...] + mix
    r = lax.rsqrt(jnp.mean(x1 * x1, axis=-1, keepdims=True) + NORM_EPS)
    hn = (x1 * r * gn_ref[...]).astype(BF16)
    for j in range(N_FF_CHUNKS):
        cols = slice(j * FF_CHUNK, (j + 1) * FF_CHUNK)
        gcols = slice(D_FF + j * FF_CHUNK, D_FF + (j + 1) * FF_CHUNK)
        a = jnp.dot(hn, wu_ref[:, cols], preferred_element_type=F32)
        gate = jnp.dot(hn, wu_ref[:, gcols], preferred_element_type=F32)
        if decode:
            am2, am1 = p2_ref[:, cols], p1_ref[:, cols]
            a_out_ref[:, cols] = a
        else:
            prev = carry_scr[:, cols]
            p2 = jnp.broadcast_to(prev[CARRY_ROWS - 2:CARRY_ROWS - 1], a.shape)
            p1 = jnp.broadcast_to(prev[CARRY_ROWS - 1:], a.shape)
            row = lax.broadcasted_iota(jnp.int32, a.shape, 0)
            am1 = jnp.where(row < 1, p1, pltpu.roll(a, 1, 0))
            am2 = jnp.where(row < 1, p2, jnp.where(row < 2, p1, pltpu.roll(a, 2, 0)))
            carry_scr[:, cols] = a[tm - CARRY_ROWS:]
        cw = cw_ref[:, cols]
        c = cb_ref[:, cols] + (cw[0:1] * am2 + cw[1:2] * am1 + cw[2:3] * a)
        hid_scr[:, cols] = (jax.nn.silu(c) * gate).astype(BF16)
    acc = x1 + jnp.dot(hid_scr[...], wd_ref[...], preferred_element_type=F32)
    r2 = lax.rsqrt(jnp.mean(acc * acc, axis=-1, keepdims=True) + NORM_EPS)
    y_ref[...] = acc * r2 * fn_ref[...]
    if not decode:
        cs_ref[...] = carry_scr[CARRY_ROWS - (CONV_W - 1):, :]


def _ffn(x, o, z, weights, tm, prev=None):
    B, S, D = x.shape
    ns = S // tm
    decode = prev is not None
    wo, gn, wu, cw, cb, wd, fn = weights
    const = lambda shape: pl.BlockSpec(shape, lambda b, s: (0,) * len(shape), pipeline_mode=pl.Buffered(1))
    xs = pl.BlockSpec((None, tm, D), lambda b, s: (b, s, 0))
    in_specs = [xs, pl.BlockSpec((tm, D_ATT), lambda b, s: (b * ns + s, 0)),
                pl.BlockSpec((tm, D_SSM), lambda b, s: (s, b)),
                const((D, D)), const((1, D)), const((D, 2 * D_FF)), const((CONV_W, D_FF)), const((1, D_FF)),
                const((D_FF, D)), const((1, D))]
    args = [x, o, z, wo, gn, wu, cw, cb, wd, fn]
    scratch = [pltpu.VMEM((tm, D_FF), BF16)]
    if decode:
        rows = pl.BlockSpec((tm, D_FF), lambda b, s: (0, 0))
        in_specs += [rows, rows]
        args += list(prev)
        out_specs = [xs, rows]
        out_shape = [jax.ShapeDtypeStruct((B, S, D), F32), jax.ShapeDtypeStruct((tm, D_FF), F32)]
    else:
        out_specs = [xs, pl.BlockSpec((None, CONV_W - 1, D_FF), lambda b, s: (b, 0, 0))]
        out_shape = [jax.ShapeDtypeStruct((B, S, D), F32), jax.ShapeDtypeStruct((B, CONV_W - 1, D_FF), F32)]
        scratch.append(pltpu.VMEM((CARRY_ROWS, D_FF), F32))
    y, extra = pl.pallas_call(
        functools.partial(_ffn_kernel, tm=tm, decode=decode),
        grid=(B, ns),
        in_specs=in_specs, out_specs=out_specs, out_shape=out_shape, scratch_shapes=scratch,
        compiler_params=_params("parallel", "arbitrary"),
        name="ffn_decode" if decode else "ffn",
    )(*args)
    return y, extra


def kernel(x_prompt, x_sample, cache_k, cache_v, state_ssm_re, state_ssm_im, state_conv, page_table, norm_mix, w_in, lambda_q1, lambda_k1, lambda_q2, lambda_k2, subln_g, ssm_a_re, ssm_a_im, ssm_log_dt, ssm_b_re, ssm_b_im, ssm_c_re, ssm_c_im, ssm_d, w_glu, w_o, norm_ffn, w_up, conv_w, conv_b, w_down, final_norm):
    assert norm_mix.shape[0] == 1, "single layer"
    Bp, Sp, D = x_prompt.shape
    Bs = x_sample.shape[0]
    G, N, C = N_SSM_GROUPS, SSM_STATE, SSM_GROUP
    past_len = page_table.shape[1] * cache_k.shape[2]

    lam_vecs = [v.reshape(1, ATT_HEAD_DIM) for v in (lambda_q1[0], lambda_k1[0], lambda_q2[0], lambda_k2[0])]
    g_sub = subln_g[0].reshape(1, HEAD_W)

    lb_re, lb_im, bb_blk, c_blk = _ssm_prep(ssm_a_re[0], ssm_a_im[0], ssm_log_dt[0], ssm_b_re[0], ssm_b_im[0],
                                            ssm_c_re[0], ssm_c_im[0])

    tabs_p = _rope_tables(np.arange(Sp))
    q, k, v, u, (w_in_b, wglu_b) = _in_proj(x_prompt, norm_mix[0], w_in[0], tabs_p, tm=IN_PROJ_ROWS,
                                            cast=(w_in[0], w_glu[0]))
    tabs_s = _rope_tables(np.full((Bs,), past_len))
    qs, ks, vs, us, _ = _in_proj(x_sample.reshape(1, Bs, D), norm_mix[0], w_in_b, tabs_s, tm=Bs)

    o, os_ = _attn(q, k, v, qs, ks, vs, cache_k[0], cache_v[0], page_table, lam_vecs, g_sub, Bp, Sp, tq=ATTN_Q_ROWS)

    z, hre_p, him_p, (wo_b, wup_b, wdown_b) = _ssm(
        u.reshape(Sp, Bp, D_SSM), bb_blk, lb_re, lb_im, c_blk, ssm_d[0], wglu_b, None, tc=SSM_STEPS,
        cast=(w_o[0], w_up[0], w_down[0]))
    ffn_w = (wo_b, norm_ffn[0].reshape(1, D), wup_b, conv_w[0], conv_b[0].reshape(1, D_FF), wdown_b,
             final_norm.reshape(1, D))
    y_prompt, cs_p = _ffn(x_prompt, o, z.reshape(Sp, Bp * D_SSM), ffn_w, tm=FFN_ROWS)
    conv_prompt = cs_p

    zs, hre_s, him_s, _ = _ssm(us.reshape(1, Bs, D_SSM), bb_blk, lb_re, lb_im, c_blk, ssm_d[0], wglu_b,
                            (state_ssm_re[0].reshape(Bs, N_STATE), state_ssm_im[0].reshape(Bs, N_STATE)), tc=1)
    sc = state_conv[0]
    y_s, a_s = _ffn(x_sample.reshape(1, Bs, D), os_, zs.reshape(Bs, D_SSM), ffn_w, tm=Bs, prev=(sc[:, 0], sc[:, 1]))
    conv_sample = jnp.stack([sc[:, 1], a_s], axis=1)

    st = lambda h, b: h.reshape(1, b, G, N)
    return (y_prompt, y_s.reshape(Bs, 1, D),
            k.reshape(1, Bp, Sp, N_ATT_HEADS, HEAD_W), v.reshape(1, Bp, Sp, N_ATT_HEADS, HEAD_W),
            st(hre_p, Bp), st(him_p, Bp), conv_prompt[None],
            ks.reshape(1, Bs, 1, N_ATT_HEADS, HEAD_W), vs.reshape(1, Bs, 1, N_ATT_HEADS, HEAD_W),
            st(hre_s, Bs), st(him_s, Bs), conv_sample[None])
```

```python
import functools
import math

import jax
import jax.numpy as jnp
import numpy as np
from jax import lax
from jax.experimental import pallas as pl
from jax.experimental.pallas import tpu as pltpu

F32 = jnp.float32
BF16 = jnp.bfloat16

D_MODEL = 1024
D_ATT = 512
D_SSM = 512
ATT_HEAD_DIM = 64
N_ATT_HEADS = 4
HEAD_W = 2 * ATT_HEAD_DIM
ROT_DIM = 16
ROPE_THETA = 500000.0
SSM_GROUP = 16
N_SSM_GROUPS = 32
SSM_STATE = 64
N_STATE = N_SSM_GROUPS * SSM_STATE
D_FF = 2816
CONV_W = 3
NORM_EPS = 1e-6
SUBLN_EPS = 1e-5
NEG_INF = -1e30
D_PROJ = 3 * D_ATT + D_SSM
LAMBDA_INIT = 0.8 - 0.6 * math.exp(-0.3 * 0)
LOG2_E = math.log2(math.e)

V7X_VMEM_LIMIT = 58 * 1024 * 1024
IN_PROJ_ROWS = 1024
ATTN_Q_ROWS = 256
SSM_STEPS = 128
FFN_ROWS = 512
FF_CHUNK = 256
N_FF_CHUNKS = D_FF // FF_CHUNK
BF16_SUBLANES = 16
DECODE_GROUPS = 2
CARRY_ROWS = 8


def _params(*sem):
    return pltpu.CompilerParams(dimension_semantics=sem, vmem_limit_bytes=V7X_VMEM_LIMIT)


def _convert_specs(weights, n_steps, step_of):
    assert all(w.shape[0] % (n_steps * BF16_SUBLANES) == 0 for w in weights)
    return [pl.BlockSpec((w.shape[0] // n_steps, w.shape[1]), lambda *idx: (step_of(*idx), 0)) for w in weights]


def _convert_rows(src_refs, dst_refs):
    for src_ref, dst_ref in zip(src_refs, dst_refs):
        dst_ref[...] = src_ref[...].astype(BF16)


def _lambda(lq1_ref, lk1_ref, lq2_ref, lk2_ref):
    s1 = jnp.sum(lq1_ref[...] * lk1_ref[...], axis=-1, keepdims=True)
    s2 = jnp.sum(lq2_ref[...] * lk2_ref[...], axis=-1, keepdims=True)
    return jnp.exp(s1) - jnp.exp(s2) + LAMBDA_INIT


def _sub_ln(o, g):
    r = lax.rsqrt(jnp.mean(o * o, axis=-1, keepdims=True) + SUBLN_EPS)
    return (o * r * g) * (1.0 - LAMBDA_INIT)


def _ssm_prep_kernel(ar_ref, ai_ref, ldt_ref, arr_ref, air_ref, ldtr_ref, br_ref, bi_ref, cre_ref, cim_ref,
                     lbre_ref, lbim_ref, bb_ref, c_ref):
    G, N, C = N_SSM_GROUPS, SSM_STATE, SSM_GROUP

    def disc(ar, ai, dt):
        mag = jnp.exp(ar * dt)
        return mag * jnp.cos(ai * dt), mag * jnp.sin(ai * dt)

    lb_re, lb_im = disc(ar_ref[...], ai_ref[...], jnp.exp(ldt_ref[...]))
    lbre_ref[...] = lb_re
    lbim_ref[...] = lb_im
    ar, ai = arr_ref[...], air_ref[...]
    t_re, t_im = disc(ar, ai, jnp.exp(ldtr_ref[...]))
    den = ar * ar + ai * ai
    nr, ni = t_re - 1.0, t_im
    f_re = (nr * ar + ni * ai) / den
    f_im = (ni * ar - nr * ai) / den
    br, bi = br_ref[...], bi_ref[...]
    bb = ((f_re * br - f_im * bi).astype(BF16), (f_re * bi + f_im * br).astype(BF16))

    def iota(shape, axis):
        return lax.broadcasted_iota(jnp.int32, shape, axis)

    half_rows, part_w = B_GROUPS * C, B_GROUPS * N
    spread = (iota((N, part_w), 1) % N == iota((N, part_w), 0)).astype(BF16)
    own = iota((half_rows, part_w), 0) // C == iota((half_rows, part_w), 1) // N
    for half in range(G // B_GROUPS):
        for part in range(2):
            tiled = jnp.dot(bb[part][half * half_rows:(half + 1) * half_rows], spread, preferred_element_type=F32)
            bb_ref[half, :, part * part_w:(part + 1) * part_w] = jnp.where(own, tiled, 0.0).astype(BF16)

    slab_rows, slab_w = C_GROUPS * N, C_GROUPS * C
    spread_c = (iota((C, slab_w), 1) % C == iota((C, slab_w), 0)).astype(BF16)
    own_c = (iota((G * N, slab_w), 0) // N) % C_GROUPS == iota((G * N, slab_w), 1) // C
    for part, (c_rows_ref, sign) in enumerate(((cre_ref, 1.0), (cim_ref, -1.0))):
        tiled = jnp.dot(c_rows_ref[...].astype(BF16), spread_c, preferred_element_type=F32)
        blk = jnp.where(own_c, sign * tiled, 0.0).astype(BF16)
        for s in range(G // C_GROUPS):
            c_ref[s, part] = blk[s * slab_rows:(s + 1) * slab_rows]


def _ssm_prep(a_re, a_im, log_dt, b_re, b_im, c_re, c_im):
    G, N, C = N_SSM_GROUPS, SSM_STATE, SSM_GROUP
    per_channel = lambda a: jnp.repeat(a, C, axis=0)
    b_rows = lambda b: jnp.transpose(b, (0, 2, 1)).reshape(G * C, N)
    c_rows = lambda c: jnp.transpose(c, (0, 2, 1)).reshape(G * N, C)
    ldt = log_dt.reshape(G, 1)
    return pl.pallas_call(
        _ssm_prep_kernel,
        out_shape=[jax.ShapeDtypeStruct((G, N), F32)] * 2
                  + [jax.ShapeDtypeStruct((G // B_GROUPS, B_GROUPS * C, 2 * B_GROUPS * N), BF16),
                     jax.ShapeDtypeStruct((G // C_GROUPS, 2, C_GROUPS * N, C_GROUPS * C), BF16)],
        name="ssm_prep",
    )(a_re, a_im, ldt, per_channel(a_re), per_channel(a_im), per_channel(ldt), b_rows(b_re), b_rows(b_im),
      c_rows(c_re), c_rows(c_im))


def _in_proj_kernel(x_ref, g_ref, w_ref, cos_ref, sa_ref, sb_ref, *rest):
    n_cast = (len(rest) - 4) // 2
    q_ref, k_ref, v_ref, u_ref = rest[n_cast:n_cast + 4]
    _convert_rows(rest[:n_cast], rest[n_cast + 4:])
    tm = x_ref.shape[0]
    x = x_ref[...]
    r = lax.rsqrt(jnp.mean(x * x, axis=-1, keepdims=True) + NORM_EPS)
    hn = (x * r * g_ref[...]).astype(BF16)
    proj = jnp.dot(hn, w_ref[...].astype(BF16), preferred_element_type=F32)
    cos, sa, sb = cos_ref[...], sa_ref[...], sb_ref[...]
    half = ROT_DIM // 2

    def rope(xh):
        return xh * cos + pltpu.roll(xh, HEAD_W - half, 1) * sa + pltpu.roll(xh, half, 1) * sb

    for h in range(N_ATT_HEADS):
        lo, hi = h * HEAD_W, (h + 1) * HEAD_W
        q_ref[:, lo:hi] = rope(proj[:, lo:hi]) * (ATT_HEAD_DIM ** -0.5)
        k_ref[pl.ds(h, tm, stride=N_ATT_HEADS), :] = rope(proj[:, D_ATT + lo:D_ATT + hi])
        v_ref[pl.ds(h, tm, stride=N_ATT_HEADS), :] = proj[:, 2 * D_ATT + lo:2 * D_ATT + hi]
    u_ref[...] = proj[:, 3 * D_ATT:]


def _rope_tables(pos):
    half = ROT_DIM // 2
    inv = ROPE_THETA ** (-np.arange(0, ROT_DIM, 2, dtype=np.float64) / ROT_DIM)
    ang = np.asarray(pos, np.float64)[:, None] * inv[None, :]
    cos, sin = np.cos(ang), np.sin(ang)
    S = ang.shape[0]
    pad = np.zeros((S, ATT_HEAD_DIM - ROT_DIM))
    zh = np.zeros((S, half))
    comp = lambda a, b, c: np.concatenate([a, b, c], axis=1)
    two = lambda t: jnp.asarray(np.concatenate([t, t], axis=1), F32)
    return (two(comp(cos, cos, pad + 1.0)), two(comp(-sin, zh, pad)), two(comp(zh, sin, pad)))


def _in_proj(x, g, w, tables, tm, cast=()):
    B, S, D = x.shape
    ns = S // tm
    cast_specs = _convert_specs(cast, B * ns, lambda b, s: b * ns + s)
    row = lambda b, s: (b * ns + s, 0)
    tab = pl.BlockSpec((tm, HEAD_W), lambda b, s: (s, 0))
    q_spec = pl.BlockSpec((tm, D_ATT), row)
    kv_spec = pl.BlockSpec((tm * N_ATT_HEADS, HEAD_W), row)
    kv_shape = jax.ShapeDtypeStruct((B * S * N_ATT_HEADS, HEAD_W), F32)
    q, k, v, u, *converted = pl.pallas_call(
        _in_proj_kernel,
        grid=(B, ns),
        in_specs=[pl.BlockSpec((None, tm, D), lambda b, s: (b, s, 0)),
                  pl.BlockSpec((1, D), lambda b, s: (0, 0)),
                  pl.BlockSpec((D, D_PROJ), lambda b, s: (0, 0)),
                  tab, tab, tab] + cast_specs,
        out_specs=[q_spec, kv_spec, kv_spec, pl.BlockSpec((tm, D_SSM), lambda b, s: (s, b))] + cast_specs,
        out_shape=[jax.ShapeDtypeStruct((B * S, D_ATT), F32), kv_shape, kv_shape,
                   jax.ShapeDtypeStruct((S, B * D_SSM), F32)] + [jax.ShapeDtypeStruct(c.shape, BF16) for c in cast],
        compiler_params=_params("parallel", "parallel"),
        name="in_proj",
    )(x, g.reshape(1, D), w, *tables, *cast)
    return q, k, v, u, converted


def _prompt_tile(i, lam, g_col, q_ref, k_ref, v_ref, o_ref, ml_scr, acc_scr, tq):
    nh = N_ATT_HEADS
    feat = lax.broadcasted_iota(jnp.int32, (HEAD_W, tq), 0)
    units = [(h, c) for h in range(nh) for c in range(2)]
    hs = lambda h: slice(h * HEAD_W, (h + 1) * HEAD_W)
    q_t = []
    for h, c in units:
        qh_t = (q_ref[:, hs(h)] * LOG2_E).T
        own = (feat < ATT_HEAD_DIM) if c == 0 else (feat >= ATT_HEAD_DIM)
        q_t.append(jnp.where(own, qh_t, 0.0))

    n = range(len(units))
    rows = lambda start, size, h: pl.ds(pl.multiple_of(start, tq) * nh + h, size, stride=nh)

    def scores(start, size, masked):
        s = [jnp.dot(k_ref[rows(start, size, h), :], q_t[u], preferred_element_type=F32)
             for u, (h, c) in enumerate(units)]
        if masked:
            key = lax.broadcasted_iota(jnp.int32, (size, tq), 0)
            qry = lax.broadcasted_iota(jnp.int32, (size, tq), 1)
            s = [jnp.where(key <= qry, su, NEG_INF) for su in s]
        return s

    def softmax(s, carry):
        m_new = [jnp.maximum(carry[u][0], jnp.max(s[u], axis=0, keepdims=True)) for u in n]
        alpha = [jnp.exp2(carry[u][0] - m_new[u]) for u in n]
        return m_new, alpha, [jnp.exp2(s[u] - m_new[u]) for u in n]

    def update(start, size, carry, m_new, alpha, p):
        ones_rows = jnp.ones((8, size), F32)
        pv = [jnp.dot(jnp.concatenate([v_ref[rows(start, size, h), :].T, ones_rows], axis=0), p[u],
                      preferred_element_type=F32) for u, (h, c) in enumerate(units)]
        return tuple((m_new[u], alpha[u] * carry[u][1] + pv[u][HEAD_W:HEAD_W + 1],
                      alpha[u] * carry[u][2] + pv[u][:HEAD_W]) for u in n)

    nu = len(units)
    load = lambda: tuple((ml_scr[u:u + 1, :], ml_scr[nu + u:nu + u + 1, :], acc_scr[u]) for u in n)

    def store(carry):
        for u, (m, l, acc) in enumerate(carry):
            ml_scr[u:u + 1, :] = m
            ml_scr[nu + u:nu + u + 1, :] = l
            acc_scr[u] = acc

    def block(start, size):
        carry = load()
        store(update(start, size, carry, *softmax(scores(start, size, False), carry)))

    ml_scr[:nu, :] = jnp.full((nu, tq), NEG_INF, F32)
    ml_scr[nu:, :] = jnp.zeros((nu, tq), F32)
    acc_scr[...] = jnp.zeros(acc_scr.shape, F32)
    lax.fori_loop(0, i // 2, lambda j, _: (block(2 * j * tq, 2 * tq), 0)[1], 0)

    @pl.when(i % 2 == 1)
    def _():
        block((i - 1) * tq, tq)

    yield
    carry = load()
    s = scores(i * tq, tq, True)
    yield
    stats = softmax(s, carry)
    yield
    carry = update(i * tq, tq, carry, *stats)
    for h in range(nh):
        (_, l1, acc1), (_, l2, acc2) = carry[2 * h], carry[2 * h + 1]
        o_t = acc1 / l1 - lam * (acc2 / l2)
        r = lax.rsqrt(jnp.mean(o_t * o_t, axis=0, keepdims=True) + SUBLN_EPS)
        o_ref[:, h * HEAD_W:(h + 1) * HEAD_W] = ((o_t * r * g_col) * (1.0 - LAMBDA_INIT)).T


def _decode_step(j, n_steps, lam, g_row, q_ref, kn_ref, vn_ref, k_refs, v_refs, o_ref, m_scr, l_scr, acc_scr):
    H = N_ATT_HEADS

    @pl.when(j == 0)
    def _():
        m_scr[...] = jnp.full(m_scr.shape, NEG_INF, F32)
        l_scr[...] = jnp.zeros(l_scr.shape, F32)
        acc_scr[...] = jnp.zeros(acc_scr.shape, F32)

    q = q_ref[...]
    lane = lax.broadcasted_iota(jnp.int32, q.shape, 1)
    q8 = jnp.concatenate([jnp.where(lane < ATT_HEAD_DIM, q, 0.0), jnp.where(lane >= ATT_HEAD_DIM, q, 0.0)], axis=0)
    rows = k_refs[0].shape[0]
    zero = jnp.zeros_like(q8)
    pad = jnp.zeros((HEAD_W - 4 * H, 2 * HEAD_W), F32)
    q_pair = jnp.concatenate([jnp.concatenate([q8, zero], axis=1), jnp.concatenate([zero, q8], axis=1), pad], axis=0)
    r = lax.broadcasted_iota(jnp.int32, (4 * H, rows), 0)
    col = lax.broadcasted_iota(jnp.int32, (4 * H, rows), 1)
    own = (col % H) == (r % H)
    both = lambda x: jnp.concatenate([x, x], axis=0)
    halves = lambda x, op: op(x[:2 * H], x[2 * H:])

    def scores(group):
        k_pairs = jnp.concatenate([jnp.concatenate([k_refs[a][...], k_refs[a + 1][...]], axis=1) for a in group],
                                  axis=0)
        s_t = lax.dot_general(k_pairs, q_pair, (((1,), (1,)), ((), ())), preferred_element_type=F32)
        return jnp.concatenate([jnp.where(own, s_t[n * rows:(n + 1) * rows].T[:4 * H], NEG_INF)
                                for n in range(len(group))], axis=1)

    def weighted_values(group, p):
        pv = None
        for n, a in enumerate(group):
            blk = slice(n * rows, (n + 1) * rows)
            part = (jnp.dot(p[:2 * H, blk], v_refs[a][...], preferred_element_type=F32)
                    + jnp.dot(p[2 * H:, blk], v_refs[a + 1][...], preferred_element_type=F32))
            pv = part if pv is None else pv + part
        return pv

    pairs = list(range(0, len(k_refs), 2))
    groups = [pairs[g::DECODE_GROUPS] for g in range(DECODE_GROUPS)]
    s = [scores(g) for g in groups]
    yield
    m_g = [halves(jnp.max(sg, axis=-1, keepdims=True), jnp.maximum) for sg in s]
    p = [jnp.exp(sg - both(mg)) for sg, mg in zip(s, m_g)]
    l_g = [halves(jnp.sum(pg, axis=-1, keepdims=True), jnp.add) for pg in p]
    yield
    pv_g = [weighted_values(g, pg) for g, pg in zip(groups, p)]
    m = m_scr[...]
    m_new = functools.reduce(jnp.maximum, m_g, m)
    alpha = jnp.exp(m - m_new)
    l, acc = alpha * l_scr[...], alpha * acc_scr[...]
    for mg, lg, pvg in zip(m_g, l_g, pv_g):
        w = jnp.exp(mg - m_new)
        l, acc = l + w * lg, acc + w * pvg
    m_scr[...] = m_new
    l_scr[...] = l
    acc_scr[...] = acc

    @pl.when(j == n_steps - 1)
    def _():
        kn = jnp.concatenate([kn_ref[...], kn_ref[...]], axis=0)
        vn = jnp.concatenate([vn_ref[...], vn_ref[...]], axis=0)
        s_new = jnp.sum(q8 * kn, axis=-1, keepdims=True)
        m_old = m_scr[...]
        m_fin = jnp.maximum(m_old, s_new)
        a = jnp.exp(m_old - m_fin)
        p_new = jnp.exp(s_new - m_fin)
        l = a * l_scr[...] + p_new
        acc = (a * acc_scr[...] + p_new * vn) / l
        o_ref[...] = _sub_ln(acc[:H] - lam * acc[H:], g_row)


def _attn_kernel(pt_ref, lq1_ref, lk1_ref, lq2_ref, lk2_ref, gcol_ref, grow_ref, q_ref, k_ref, v_ref,
                 qs_ref, kn_ref, vn_ref, ck_hbm, cv_hbm, o_ref, os_ref, m_scr, l_scr, acc_scr, kbuf, vbuf, sem,
                 pml_scr, pacc_scr,
                 *, tq, steps_per_seq, pages):
    i = pl.program_id(1)
    step = pl.program_id(0) * pl.num_programs(1) + i
    n_steps = pl.num_programs(0) * pl.num_programs(1)

    def page_copies(of_step, p):
        slot = of_step % 2
        pid = pt_ref[of_step // steps_per_seq, (of_step % steps_per_seq) * pages + p]
        return (pltpu.make_async_copy(ck_hbm.at[pid], kbuf.at[slot, p], sem.at[slot, 0]),
                pltpu.make_async_copy(cv_hbm.at[pid], vbuf.at[slot, p], sem.at[slot, 1]))

    def fetch(of_step, do, unrolled):
        def one(p):
            for c in page_copies(of_step, p):
                do(c)
        if unrolled:
            for p in range(pages):
                one(p)
        else:
            lax.fori_loop(0, pages, lambda p, _: (one(p), 0)[1], 0)

    start, wait = (lambda c: c.start()), (lambda c: c.wait())

    @pl.when(step == 0)
    def _():
        fetch(step, start, unrolled=False)

    following = lax.rem(step + 1, n_steps)
    fetch(following, start, unrolled=True)
    lam = _lambda(lq1_ref, lk1_ref, lq2_ref, lk2_ref)
    prompt = _prompt_tile(i, lam, gcol_ref[...], q_ref, k_ref, v_ref, o_ref, pml_scr, pacc_scr, tq)
    next(prompt)
    fetch(step, wait, unrolled=True)
    slot = step % 2
    decode = _decode_step(step % steps_per_seq, steps_per_seq, lam, grow_ref[...], qs_ref, kn_ref, vn_ref,
                          [kbuf.at[slot, p] for p in range(pages)], [vbuf.at[slot, p] for p in range(pages)],
                          os_ref, m_scr, l_scr, acc_scr)
    running = [prompt, decode]
    while running:
        for gen in list(running):
            if next(gen, StopIteration) is StopIteration:
                running.remove(gen)

    @pl.when(step == n_steps - 1)
    def _():
        fetch(following, wait, unrolled=False)


def _attn(q, k, v, qs, ks, vs, cache_k, cache_v, page_table, lam_vecs, g, B, S, tq):
    H = N_ATT_HEADS
    nq = S // tq
    Bs, n_pages = page_table.shape
    n_pool, page = cache_k.shape[0], cache_k.shape[1]
    steps_per_seq = (B * nq) // Bs
    assert steps_per_seq * Bs == B * nq and n_pages % steps_per_seq == 0
    P = n_pages // steps_per_seq
    ck = cache_k.reshape(n_pool, page * H, HEAD_W)
    cv = cache_v.reshape(n_pool, page * H, HEAD_W)
    seq = lambda b, i: (b * nq + i) // steps_per_seq
    vec = pl.BlockSpec((1, ATT_HEAD_DIM), lambda b, i, pt: (0, 0))
    kv = pl.BlockSpec((S * H, HEAD_W), lambda b, i, pt: (b, 0))
    qo = pl.BlockSpec((tq, D_ATT), lambda b, i, pt: (b * nq + i, 0))
    tok = pl.BlockSpec((None, H, HEAD_W), lambda b, i, pt: (seq(b, i), 0, 0))
    hbm = pl.BlockSpec(memory_space=pl.ANY)
    page_buf = pltpu.VMEM((2, P, page * H, HEAD_W), F32)
    grid_spec = pltpu.PrefetchScalarGridSpec(
        num_scalar_prefetch=1,
        grid=(B, nq),
        in_specs=[vec, vec, vec, vec, pl.BlockSpec((HEAD_W, 1), lambda b, i, pt: (0, 0)),
                  pl.BlockSpec((1, HEAD_W), lambda b, i, pt: (0, 0)), qo, kv, kv, tok, tok, tok, hbm, hbm],
        out_specs=[qo, tok],
        scratch_shapes=[pltpu.VMEM((2 * H, 1), F32), pltpu.VMEM((2 * H, 1), F32), pltpu.VMEM((2 * H, HEAD_W), F32),
                        page_buf, page_buf, pltpu.SemaphoreType.DMA((2, 2)),
                        pltpu.VMEM((4 * H, tq), F32), pltpu.VMEM((2 * H, HEAD_W, tq), F32)],
    )
    tok3 = lambda a: a.reshape(Bs, H, HEAD_W)
    o, o_s = pl.pallas_call(
        functools.partial(_attn_kernel, tq=tq, steps_per_seq=steps_per_seq, pages=P),
        grid_spec=grid_spec,
        out_shape=[jax.ShapeDtypeStruct((B * S, D_ATT), F32), jax.ShapeDtypeStruct((Bs, H, HEAD_W), F32)],
        compiler_params=_params("arbitrary", "arbitrary"),
        name="attn",
    )(page_table, *lam_vecs, g.reshape(HEAD_W, 1), g.reshape(1, HEAD_W), q, k, v, tok3(qs), tok3(ks), tok3(vs), ck, cv)
    return o, o_s.reshape(Bs, D_ATT)


SCAN_COLS = 1024
B_GROUPS = 16
C_GROUPS = 8


def _ssm_kernel(u_ref, bb_ref, lre_ref, lim_ref, c_ref, d_ref, wglu_ref, *refs, tc, bk, fresh):
    n_h0 = 0 if fresh else 2
    n_cast = (len(refs) - n_h0 - 5) // 2
    h0_refs, cast_refs = refs[:n_h0], refs[n_h0:n_h0 + n_cast]
    z_ref, hre_ref, him_ref = refs[n_h0 + n_cast:n_h0 + n_cast + 3]
    bu_scr, hs_scr = refs[-2:]
    _convert_rows(cast_refs, refs[n_h0 + n_cast + 3:-2])
    t0 = pl.program_id(0)

    @pl.when(t0 == 0)
    def _():
        for h_ref, h0_ref in zip((hre_ref, him_ref), h0_refs or (None, None)):
            h_ref[...] = jnp.zeros(h_ref.shape, F32) if fresh else h0_ref[...]

    rows = tc * bk
    u = u_ref[...].reshape(rows, D_SSM)
    ub = u.astype(BF16)
    part_w = B_GROUPS * SSM_STATE
    half_w = 2 * part_w
    n_half = N_SSM_GROUPS // B_GROUPS
    slab_w = C_GROUPS * SSM_STATE

    def input_map(half):
        u_half = ub[:, half * B_GROUPS * SSM_GROUP:(half + 1) * B_GROUPS * SSM_GROUP]
        bu = jnp.dot(u_half, bb_ref[half], preferred_element_type=F32)
        bu_scr[:, :, half * half_w:(half + 1) * half_w] = bu.reshape(tc, bk, half_w)

    def scan(half):
        for cg in range(part_w // SCAN_COLS):
            st = slice(half * part_w + cg * SCAN_COLS, half * part_w + (cg + 1) * SCAN_COLS)
            re = slice(half * half_w + cg * SCAN_COLS, half * half_w + (cg + 1) * SCAN_COLS)
            im = slice(re.start + part_w, re.stop + part_w)
            lr = jnp.broadcast_to(lre_ref[:, st], (bk, SCAN_COLS))
            li = jnp.broadcast_to(lim_ref[:, st], (bk, SCAN_COLS))
            hr, hi = hre_ref[:, st], him_ref[:, st]
            for t in range(tc):
                hr, hi = lr * hr - li * hi + bu_scr[t, :, re], lr * hi + li * hr + bu_scr[t, :, im]
                hs_scr[t, :, re] = hr
                hs_scr[t, :, im] = hi
            hre_ref[:, st] = hr
            him_ref[:, st] = hi

    def output_map(s):
        re0 = (s * C_GROUPS // B_GROUPS) * half_w + (s * C_GROUPS % B_GROUPS) * SSM_STATE
        h_re = hs_scr[:, :, re0:re0 + slab_w].reshape(rows, slab_w).astype(BF16)
        h_im = hs_scr[:, :, re0 + part_w:re0 + part_w + slab_w].reshape(rows, slab_w).astype(BF16)
        return (jnp.dot(h_re, c_ref[s, 0], preferred_element_type=F32)
                + jnp.dot(h_im, c_ref[s, 1], preferred_element_type=F32))

    slabs_per_half = B_GROUPS // C_GROUPS
    input_map(0)
    y_slabs = []
    for half in range(n_half):
        if half + 1 < n_half:
            input_map(half + 1)
        scan(half)
        y_slabs += [output_map(half * slabs_per_half + s) for s in range(slabs_per_half)]
    y = jnp.concatenate(y_slabs, axis=1) + d_ref[...] * u
    y = jax.nn.gelu(y)
    gate = jnp.dot(y.astype(BF16), wglu_ref[...], preferred_element_type=F32)
    z_ref[...] = (y * jax.nn.sigmoid(gate)).reshape(tc, bk, D_SSM)


def _ssm(u3, bb, lb_re, lb_im, c_full, d, w_glu, h0, tc, cast=()):
    S, bk, _ = u3.shape
    const = lambda shape: pl.BlockSpec(shape, lambda t: (0,) * len(shape))
    seq = pl.BlockSpec((tc, bk, D_SSM), lambda t: (t, 0, 0))
    h0 = tuple(h0 or ())
    cast_specs = _convert_specs(cast, S // tc, lambda t: t)
    z, h_re, h_im, *converted = pl.pallas_call(
        functools.partial(_ssm_kernel, tc=tc, bk=bk, fresh=not h0),
        grid=(S // tc,),
        in_specs=[seq, const(bb.shape), const((1, N_STATE)), const((1, N_STATE)),
                  const(c_full.shape), const((1, D_SSM)), const((D_SSM, D_SSM))] + [const((bk, N_STATE))] * len(h0)
                 + cast_specs,
        out_specs=[seq, const((bk, N_STATE)), const((bk, N_STATE))] + cast_specs,
        out_shape=[jax.ShapeDtypeStruct((S, bk, D_SSM), F32), jax.ShapeDtypeStruct((bk, N_STATE), F32),
                   jax.ShapeDtypeStruct((bk, N_STATE), F32)] + [jax.ShapeDtypeStruct(c.shape, BF16) for c in cast],
        scratch_shapes=[pltpu.VMEM((tc, bk, 2 * N_STATE), F32), pltpu.VMEM((tc, bk, 2 * N_STATE), F32)],
        compiler_params=_params("arbitrary"),
        name="ssm",
    )(u3, bb, lb_re.reshape(1, N_STATE), lb_im.reshape(1, N_STATE), c_full, d.reshape(1, D_SSM), w_glu, *h0, *cast)
    return z, h_re, h_im, converted


def _ffn_kernel(*refs, tm, decode):
    if decode:
        (x_ref, o_ref, z_ref, wo_ref, gn_ref, wu_ref, cw_ref, cb_ref, wd_ref, fn_ref, p2_ref, p1_ref,
         y_ref, a_out_ref, hid_scr) = refs
    else:
        (x_ref, o_ref, z_ref, wo_ref, gn_ref, wu_ref, cw_ref, cb_ref, wd_ref, fn_ref,
         y_ref, cs_ref, hid_scr, carry_scr) = refs

        @pl.when(pl.program_id(1) == 0)
        def _():
            carry_scr[...] = jnp.zeros(carry_scr.shape, F32)

    mix = (jnp.dot(o_ref[...].astype(BF16), wo_ref[:D_ATT, :], preferred_element_type=F32)
           + jnp.dot(z_ref[...].astype(BF16), wo_ref[D_ATT:, :], preferred_element_type=F32))
    x1 = x_ref[...] + mix
    r = lax.rsqrt(jnp.mean(x1 * x1, axis=-1, keepdims=True) + NORM_EPS)
    hn = (x1 * r * gn_ref[...]).astype(BF16)
    for j in range(N_FF_CHUNKS):
        cols = slice(j * FF_CHUNK, (j + 1) * FF_CHUNK)
        gcols = slice(D_FF + j * FF_CHUNK, D_FF + (j + 1) * FF_CHUNK)
        a = jnp.dot(hn, wu_ref[:, cols], preferred_element_type=F32)
        gate = jnp.dot(hn, wu_ref[:, gcols], preferred_element_type=F32)
        if decode:
            am2, am1 = p2_ref[:, cols], p1_ref[:, cols]
            a_out_ref[:, cols] = a
        else:
            prev = carry_scr[:, cols]
            p2 = jnp.broadcast_to(prev[CARRY_ROWS - 2:CARRY_ROWS - 1], a.shape)
            p1 = jnp.broadcast_to(prev[CARRY_ROWS - 1:], a.shape)
            row = lax.broadcasted_iota(jnp.int32, a.shape, 0)
            am1 = jnp.where(row < 1, p1, pltpu.roll(a, 1, 0))
            am2 = jnp.where(row < 1, p2, jnp.where(row < 2, p1, pltpu.roll(a, 2, 0)))
            carry_scr[:, cols] = a[tm - CARRY_ROWS:]
        cw = cw_ref[:, cols]
        c = cb_ref[:, cols] + (cw[0:1] * am2 + cw[1:2] * am1 + cw[2:3] * a)
        hid_scr[:, cols] = (jax.nn.silu(c) * gate).astype(BF16)
    acc = x1 + jnp.dot(hid_scr[...], wd_ref[...], preferred_element_type=F32)
    r2 = lax.rsqrt(jnp.mean(acc * acc, axis=-1, keepdims=True) + NORM_EPS)
    y_ref[...] = acc * r2 * fn_ref[...]
    if not decode:
        cs_ref[...] = carry_scr[CARRY_ROWS - (CONV_W - 1):, :]


def _ffn(x, o, z, weights, tm, prev=None):
    B, S, D = x.shape
    ns = S // tm
    decode = prev is not None
    wo, gn, wu, cw, cb, wd, fn = weights
    const = lambda shape: pl.BlockSpec(shape, lambda b, s: (0,) * len(shape), pipeline_mode=pl.Buffered(1))
    xs = pl.BlockSpec((None, tm, D), lambda b, s: (b, s, 0))
    in_specs = [xs, pl.BlockSpec((tm, D_ATT), lambda b, s: (b * ns + s, 0)),
                pl.BlockSpec((tm, D_SSM), lambda b, s: (s, b)),
                const((D, D)), const((1, D)), const((D, 2 * D_FF)), const((CONV_W, D_FF)), const((1, D_FF)),
                const((D_FF, D)), const((1, D))]
    args = [x, o, z, wo, gn, wu, cw, cb, wd, fn]
    scratch = [pltpu.VMEM((tm, D_FF), BF16)]
    if decode:
        rows = pl.BlockSpec((tm, D_FF), lambda b, s: (0, 0))
        in_specs += [rows, rows]
        args += list(prev)
        out_specs = [xs, rows]
        out_shape = [jax.ShapeDtypeStruct((B, S, D), F32), jax.ShapeDtypeStruct((tm, D_FF), F32)]
    else:
        out_specs = [xs, pl.BlockSpec((None, CONV_W - 1, D_FF), lambda b, s: (b, 0, 0))]
        out_shape = [jax.ShapeDtypeStruct((B, S, D), F32), jax.ShapeDtypeStruct((B, CONV_W - 1, D_FF), F32)]
        scratch.append(pltpu.VMEM((CARRY_ROWS, D_FF), F32))
    y, extra = pl.pallas_call(
        functools.partial(_ffn_kernel, tm=tm, decode=decode),
        grid=(B, ns),
        in_specs=in_specs, out_specs=out_specs, out_shape=out_shape, scratch_shapes=scratch,
        compiler_params=_params("parallel", "arbitrary"),
        name="ffn_decode" if decode else "ffn",
    )(*args)
    return y, extra


def kernel(x_prompt, x_sample, cache_k, cache_v, state_ssm_re, state_ssm_im, state_conv, page_table, norm_mix, w_in, lambda_q1, lambda_k1, lambda_q2, lambda_k2, subln_g, ssm_a_re, ssm_a_im, ssm_log_dt, ssm_b_re, ssm_b_im, ssm_c_re, ssm_c_im, ssm_d, w_glu, w_o, norm_ffn, w_up, conv_w, conv_b, w_down, final_norm):
    assert norm_mix.shape[0] == 1, "single layer"
    Bp, Sp, D = x_prompt.shape
    Bs = x_sample.shape[0]
    G, N, C = N_SSM_GROUPS, SSM_STATE, SSM_GROUP
    past_len = page_table.shape[1] * cache_k.shape[2]

    lam_vecs = [v.reshape(1, ATT_HEAD_DIM) for v in (lambda_q1[0], lambda_k1[0], lambda_q2[0], lambda_k2[0])]
    g_sub = subln_g[0].reshape(1, HEAD_W)

    lb_re, lb_im, bb_blk, c_blk = _ssm_prep(ssm_a_re[0], ssm_a_im[0], ssm_log_dt[0], ssm_b_re[0], ssm_b_im[0],
                                            ssm_c_re[0], ssm_c_im[0])

    tabs_p = _rope_tables(np.arange(Sp))
    q, k, v, u, (w_in_b, wglu_b) = _in_proj(x_prompt, norm_mix[0], w_in[0], tabs_p, tm=IN_PROJ_ROWS,
                                            cast=(w_in[0], w_glu[0]))
    tabs_s = _rope_tables(np.full((Bs,), past_len))
    qs, ks, vs, us, _ = _in_proj(x_sample.reshape(1, Bs, D), norm_mix[0], w_in_b, tabs_s, tm=Bs)

    o, os_ = _attn(q, k, v, qs, ks, vs, cache_k[0], cache_v[0], page_table, lam_vecs, g_sub, Bp, Sp, tq=ATTN_Q_ROWS)

    z, hre_p, him_p, (wo_b, wup_b, wdown_b) = _ssm(
        u.reshape(Sp, Bp, D_SSM), bb_blk, lb_re, lb_im, c_blk, ssm_d[0], wglu_b, None, tc=SSM_STEPS,
        cast=(w_o[0], w_up[0], w_down[0]))
    ffn_w = (wo_b, norm_ffn[0].reshape(1, D), wup_b, conv_w[0], conv_b[0].reshape(1, D_FF), wdown_b,
             final_norm.reshape(1, D))
    y_prompt, cs_p = _ffn(x_prompt, o, z.reshape(Sp, Bp * D_SSM), ffn_w, tm=FFN_ROWS)
    conv_prompt = cs_p

    zs, hre_s, him_s, _ = _ssm(us.reshape(1, Bs, D_SSM), bb_blk, lb_re, lb_im, c_blk, ssm_d[0], wglu_b,
                            (state_ssm_re[0].reshape(Bs, N_STATE), state_ssm_im[0].reshape(Bs, N_STATE)), tc=1)
    sc = state_conv[0]
    y_s, a_s = _ffn(x_sample.reshape(1, Bs, D), os_, zs.reshape(Bs, D_SSM), ffn_w, tm=Bs, prev=(sc[:, 0], sc[:, 1]))
    conv_sample = jnp.stack([sc[:, 1], a_s], axis=1)

    st = lambda h, b: h.reshape(1, b, G, N)
    return (y_prompt, y_s.reshape(Bs, 1, D),
            k.reshape(1, Bp, Sp, N_ATT_HEADS, HEAD_W), v.reshape(1, Bp, Sp, N_ATT_HEADS, HEAD_W),
            st(hre_p, Bp), st(him_p, Bp), conv_prompt[None],
            ks.reshape(1, Bs, 1, N_ATT_HEADS, HEAD_W), vs.reshape(1, Bs, 1, N_ATT_HEADS, HEAD_W),
            st(hre_s, Bs), st(him_s, Bs), conv_sample[None])
```

```python
import functools
import math

import jax
import jax.numpy as jnp
import numpy as np
from jax import lax
from jax.experimental import pallas as pl
from jax.experimental.pallas import tpu as pltpu

F32 = jnp.float32
BF16 = jnp.bfloat16

D_MODEL = 1024
D_ATT = 512
D_SSM = 512
ATT_HEAD_DIM = 64
N_ATT_HEADS = 4
HEAD_W = 2 * ATT_HEAD_DIM
ROT_DIM = 16
ROPE_THETA = 500000.0
SSM_GROUP = 16
N_SSM_GROUPS = 32
SSM_STATE = 64
N_STATE = N_SSM_GROUPS * SSM_STATE
D_FF = 2816
CONV_W = 3
NORM_EPS = 1e-6
SUBLN_EPS = 1e-5
NEG_INF = -1e30
D_PROJ = 3 * D_ATT + D_SSM
LAMBDA_INIT = 0.8 - 0.6 * math.exp(-0.3 * 0)
LOG2_E = math.log2(math.e)

V7X_VMEM_LIMIT = 58 * 1024 * 1024
IN_PROJ_ROWS = 1024
ATTN_Q_ROWS = 256
SSM_STEPS = 128
FFN_ROWS = 512
FF_CHUNK = 256
N_FF_CHUNKS = D_FF // FF_CHUNK
BF16_SUBLANES = 16
DECODE_GROUPS = 2
CARRY_ROWS = 8


def _params(*sem):
    return pltpu.CompilerParams(dimension_semantics=sem, vmem_limit_bytes=V7X_VMEM_LIMIT)


def _convert_specs(weights, n_steps, step_of):
    assert all(w.shape[0] % (n_steps * BF16_SUBLANES) == 0 for w in weights)
    return [pl.BlockSpec((w.shape[0] // n_steps, w.shape[1]), lambda *idx: (step_of(*idx), 0)) for w in weights]


def _convert_rows(src_refs, dst_refs):
    for src_ref, dst_ref in zip(src_refs, dst_refs):
        dst_ref[...] = src_ref[...].astype(BF16)


def _lambda(lq1_ref, lk1_ref, lq2_ref, lk2_ref):
    s1 = jnp.sum(lq1_ref[...] * lk1_ref[...], axis=-1, keepdims=True)
    s2 = jnp.sum(lq2_ref[...] * lk2_ref[...], axis=-1, keepdims=True)
    return jnp.exp(s1) - jnp.exp(s2) + LAMBDA_INIT


def _sub_ln(o, g):
    r = lax.rsqrt(jnp.mean(o * o, axis=-1, keepdims=True) + SUBLN_EPS)
    return (o * r * g) * (1.0 - LAMBDA_INIT)


def _ssm_prep_kernel(ar_ref, ai_ref, ldt_ref, arr_ref, air_ref, ldtr_ref, br_ref, bi_ref, cre_ref, cim_ref,
                     lbre_ref, lbim_ref, bb_ref, c_ref):
    G, N, C = N_SSM_GROUPS, SSM_STATE, SSM_GROUP

    def disc(ar, ai, dt):
        mag = jnp.exp(ar * dt)
        return mag * jnp.cos(ai * dt), mag * jnp.sin(ai * dt)

    lb_re, lb_im = disc(ar_ref[...], ai_ref[...], jnp.exp(ldt_ref[...]))
    lbre_ref[...] = lb_re
    lbim_ref[...] = lb_im
    ar, ai = arr_ref[...], air_ref[...]
    t_re, t_im = disc(ar, ai, jnp.exp(ldtr_ref[...]))
    den = ar * ar + ai * ai
    nr, ni = t_re - 1.0, t_im
    f_re = (nr * ar + ni * ai) / den
    f_im = (ni * ar - nr * ai) / den
    br, bi = br_ref[...], bi_ref[...]
    bb = ((f_re * br - f_im * bi).astype(BF16), (f_re * bi + f_im * br).astype(BF16))

    def iota(shape, axis):
        return lax.broadcasted_iota(jnp.int32, shape, axis)

    half_rows, part_w = B_GROUPS * C, B_GROUPS * N
    spread = (iota((N, part_w), 1) % N == iota((N, part_w), 0)).astype(BF16)
    own = iota((half_rows, part_w), 0) // C == iota((half_rows, part_w), 1) // N
    for half in range(G // B_GROUPS):
        for part in range(2):
            tiled = jnp.dot(bb[part][half * half_rows:(half + 1) * half_rows], spread, preferred_element_type=F32)
            bb_ref[half, :, part * part_w:(part + 1) * part_w] = jnp.where(own, tiled, 0.0).astype(BF16)

    slab_rows, slab_w = C_GROUPS * N, C_GROUPS * C
    spread_c = (iota((C, slab_w), 1) % C == iota((C, slab_w), 0)).astype(BF16)
    own_c = (iota((G * N, slab_w), 0) // N) % C_GROUPS == iota((G * N, slab_w), 1) // C
    for part, (c_rows_ref, sign) in enumerate(((cre_ref, 1.0), (cim_ref, -1.0))):
        tiled = jnp.dot(c_rows_ref[...].astype(BF16), spread_c, preferred_element_type=F32)
        blk = jnp.where(own_c, sign * tiled, 0.0).astype(BF16)
        for s in range(G // C_GROUPS):
            c_ref[s, part] = blk[s * slab_rows:(s + 1) * slab_rows]


def _ssm_prep(a_re, a_im, log_dt, b_re, b_im, c_re, c_im):
    G, N, C = N_SSM_GROUPS, SSM_STATE, SSM_GROUP
    per_channel = lambda a: jnp.repeat(a, C, axis=0)
    b_rows = lambda b: jnp.transpose(b, (0, 2, 1)).reshape(G * C, N)
    c_rows = lambda c: jnp.transpose(c, (0, 2, 1)).reshape(G * N, C)
    ldt = log_dt.reshape(G, 1)
    return pl.pallas_call(
        _ssm_prep_kernel,
        out_shape=[jax.ShapeDtypeStruct((G, N), F32)] * 2
                  + [jax.ShapeDtypeStruct((G // B_GROUPS, B_GROUPS * C, 2 * B_GROUPS * N), BF16),
                     jax.ShapeDtypeStruct((G // C_GROUPS, 2, C_GROUPS * N, C_GROUPS * C), BF16)],
        name="ssm_prep",
    )(a_re, a_im, ldt, per_channel(a_re), per_channel(a_im), per_channel(ldt), b_rows(b_re), b_rows(b_im),
      c_rows(c_re), c_rows(c_im))


def _in_proj_kernel(x_ref, g_ref, w_ref, cos_ref, sa_ref, sb_ref, *rest):
    n_cast = (len(rest) - 4) // 2
    q_ref, k_ref, v_ref, u_ref = rest[n_cast:n_cast + 4]
    _convert_rows(rest[:n_cast], rest[n_cast + 4:])
    tm = x_ref.shape[0]
    x = x_ref[...]
    r = lax.rsqrt(jnp.mean(x * x, axis=-1, keepdims=True) + NORM_EPS)
    hn = (x * r * g_ref[...]).astype(BF16)
    proj = jnp.dot(hn, w_ref[...].astype(BF16), preferred_element_type=F32)
    cos, sa, sb = cos_ref[...], sa_ref[...], sb_ref[...]
    half = ROT_DIM // 2

    def rope(xh):
        return xh * cos + pltpu.roll(xh, HEAD_W - half, 1) * sa + pltpu.roll(xh, half, 1) * sb

    for h in range(N_ATT_HEADS):
        lo, hi = h * HEAD_W, (h + 1) * HEAD_W
        q_ref[:, lo:hi] = rope(proj[:, lo:hi]) * (ATT_HEAD_DIM ** -0.5)
        k_ref[pl.ds(h, tm, stride=N_ATT_HEADS), :] = rope(proj[:, D_ATT + lo:D_ATT + hi])
        v_ref[pl.ds(h, tm, stride=N_ATT_HEADS), :] = proj[:, 2 * D_ATT + lo:2 * D_ATT + hi]
    u_ref[...] = proj[:, 3 * D_ATT:]


def _rope_tables(pos):
    half = ROT_DIM // 2
    inv = ROPE_THETA ** (-np.arange(0, ROT_DIM, 2, dtype=np.float64) / ROT_DIM)
    ang = np.asarray(pos, np.float64)[:, None] * inv[None, :]
    cos, sin = np.cos(ang), np.sin(ang)
    S = ang.shape[0]
    pad = np.zeros((S, ATT_HEAD_DIM - ROT_DIM))
    zh = np.zeros((S, half))
    comp = lambda a, b, c: np.concatenate([a, b, c], axis=1)
    two = lambda t: jnp.asarray(np.concatenate([t, t], axis=1), F32)
    return (two(comp(cos, cos, pad + 1.0)), two(comp(-sin, zh, pad)), two(comp(zh, sin, pad)))


def _in_proj(x, g, w, tables, tm, cast=()):
    B, S, D = x.shape
    ns = S // tm
    cast_specs = _convert_specs(cast, B * ns, lambda b, s: b * ns + s)
    row = lambda b, s: (b * ns + s, 0)
    tab = pl.BlockSpec((tm, HEAD_W), lambda b, s: (s, 0))
    q_spec = pl.BlockSpec((tm, D_ATT), row)
    kv_spec = pl.BlockSpec((tm * N_ATT_HEADS, HEAD_W), row)
    kv_shape = jax.ShapeDtypeStruct((B * S * N_ATT_HEADS, HEAD_W), F32)
    q, k, v, u, *converted = pl.pallas_call(
        _in_proj_kernel,
        grid=(B, ns),
        in_specs=[pl.BlockSpec((None, tm, D), lambda b, s: (b, s, 0)),
                  pl.BlockSpec((1, D), lambda b, s: (0, 0)),
                  pl.BlockSpec((D, D_PROJ), lambda b, s: (0, 0)),
                  tab, tab, tab] + cast_specs,
        out_specs=[q_spec, kv_spec, kv_spec, pl.BlockSpec((tm, D_SSM), lambda b, s: (s, b))] + cast_specs,
        out_shape=[jax.ShapeDtypeStruct((B * S, D_ATT), F32), kv_shape, kv_shape,
                   jax.ShapeDtypeStruct((S, B * D_SSM), F32)] + [jax.ShapeDtypeStruct(c.shape, BF16) for c in cast],
        compiler_params=_params("parallel", "parallel"),
        name="in_proj",
    )(x, g.reshape(1, D), w, *tables, *cast)
    return q, k, v, u, converted


def _prompt_tile(i, lam, g_col, q_ref, k_ref, v_ref, o_ref, ml_scr, acc_scr, tq):
    nh = N_ATT_HEADS
    feat = lax.broadcasted_iota(jnp.int32, (HEAD_W, tq), 0)
    units = [(h, c) for h in range(nh) for c in range(2)]
    hs = lambda h: slice(h * HEAD_W, (h + 1) * HEAD_W)
    q_t = []
    for h, c in units:
        qh_t = (q_ref[:, hs(h)] * LOG2_E).T
        own = (feat < ATT_HEAD_DIM) if c == 0 else (feat >= ATT_HEAD_DIM)
        q_t.append(jnp.where(own, qh_t, 0.0))

    n = range(len(units))
    rows = lambda start, size, h: pl.ds(pl.multiple_of(start, tq) * nh + h, size, stride=nh)

    def scores(start, size, masked):
        s = [jnp.dot(k_ref[rows(start, size, h), :], q_t[u], preferred_element_type=F32)
             for u, (h, c) in enumerate(units)]
        if masked:
            key = lax.broadcasted_iota(jnp.int32, (size, tq), 0)
            qry = lax.broadcasted_iota(jnp.int32, (size, tq), 1)
            s = [jnp.where(key <= qry, su, NEG_INF) for su in s]
        return s

    def softmax(s, carry):
        m_new = [jnp.maximum(carry[u][0], jnp.max(s[u], axis=0, keepdims=True)) for u in n]
        alpha = [jnp.exp2(carry[u][0] - m_new[u]) for u in n]
        return m_new, alpha, [jnp.exp2(s[u] - m_new[u]) for u in n]

    def update(start, size, carry, m_new, alpha, p):
        ones_rows = jnp.ones((8, size), F32)
        pv = [jnp.dot(jnp.concatenate([v_ref[rows(start, size, h), :].T, ones_rows], axis=0), p[u],
                      preferred_element_type=F32) for u, (h, c) in enumerate(units)]
        return tuple((m_new[u], alpha[u] * carry[u][1] + pv[u][HEAD_W:HEAD_W + 1],
                      alpha[u] * carry[u][2] + pv[u][:HEAD_W]) for u in n)

    nu = len(units)
    load = lambda: tuple((ml_scr[u:u + 1, :], ml_scr[nu + u:nu + u + 1, :], acc_scr[u]) for u in n)

    def store(carry):
        for u, (m, l, acc) in enumerate(carry):
            ml_scr[u:u + 1, :] = m
            ml_scr[nu + u:nu + u + 1, :] = l
            acc_scr[u] = acc

    def block(start, size):
        carry = load()
        store(update(start, size, carry, *softmax(scores(start, size, False), carry)))

    ml_scr[:nu, :] = jnp.full((nu, tq), NEG_INF, F32)
    ml_scr[nu:, :] = jnp.zeros((nu, tq), F32)
    acc_scr[...] = jnp.zeros(acc_scr.shape, F32)
    lax.fori_loop(0, i // 2, lambda j, _: (block(2 * j * tq, 2 * tq), 0)[1], 0)

    @pl.when(i % 2 == 1)
    def _():
        block((i - 1) * tq, tq)

    yield
    carry = load()
    s = scores(i * tq, tq, True)
    yield
    stats = softmax(s, carry)
    yield
    carry = update(i * tq, tq, carry, *stats)
    for h in range(nh):
        (_, l1, acc1), (_, l2, acc2) = carry[2 * h], carry[2 * h + 1]
        o_t = acc1 / l1 - lam * (acc2 / l2)
        r = lax.rsqrt(jnp.mean(o_t * o_t, axis=0, keepdims=True) + SUBLN_EPS)
        o_ref[:, h * HEAD_W:(h + 1) * HEAD_W] = ((o_t * r * g_col) * (1.0 - LAMBDA_INIT)).T


def _decode_step(j, n_steps, lam, g_row, q_ref, kn_ref, vn_ref, k_refs, v_refs, o_ref, m_scr, l_scr, acc_scr):
    H = N_ATT_HEADS

    @pl.when(j == 0)
    def _():
        m_scr[...] = jnp.full(m_scr.shape, NEG_INF, F32)
        l_scr[...] = jnp.zeros(l_scr.shape, F32)
        acc_scr[...] = jnp.zeros(acc_scr.shape, F32)

    q = q_ref[...]
    lane = lax.broadcasted_iota(jnp.int32, q.shape, 1)
    q8 = jnp.concatenate([jnp.where(lane < ATT_HEAD_DIM, q, 0.0), jnp.where(lane >= ATT_HEAD_DIM, q, 0.0)], axis=0)
    rows = k_refs[0].shape[0]
    zero = jnp.zeros_like(q8)
    pad = jnp.zeros((HEAD_W - 4 * H, 2 * HEAD_W), F32)
    q_pair = jnp.concatenate([jnp.concatenate([q8, zero], axis=1), jnp.concatenate([zero, q8], axis=1), pad], axis=0)
    r = lax.broadcasted_iota(jnp.int32, (4 * H, rows), 0)
    col = lax.broadcasted_iota(jnp.int32, (4 * H, rows), 1)
    own = (col % H) == (r % H)
    both = lambda x: jnp.concatenate([x, x], axis=0)
    halves = lambda x, op: op(x[:2 * H], x[2 * H:])

    def scores(group):
        k_pairs = jnp.concatenate([jnp.concatenate([k_refs[a][...], k_refs[a + 1][...]], axis=1) for a in group],
                                  axis=0)
        s_t = lax.dot_general(k_pairs, q_pair, (((1,), (1,)), ((), ())), preferred_element_type=F32)
        return jnp.concatenate([jnp.where(own, s_t[n * rows:(n + 1) * rows].T[:4 * H], NEG_INF)
                                for n in range(len(group))], axis=1)

    def weighted_values(group, p):
        pv = None
        for n, a in enumerate(group):
            blk = slice(n * rows, (n + 1) * rows)
            part = (jnp.dot(p[:2 * H, blk], v_refs[a][...], preferred_element_type=F32)
                    + jnp.dot(p[2 * H:, blk], v_refs[a + 1][...], preferred_element_type=F32))
            pv = part if pv is None else pv + part
        return pv

    pairs = list(range(0, len(k_refs), 2))
    groups = [pairs[g::DECODE_GROUPS] for g in range(DECODE_GROUPS)]
    s = [scores(g) for g in groups]
    yield
    m_g = [halves(jnp.max(sg, axis=-1, keepdims=True), jnp.maximum) for sg in s]
    p = [jnp.exp(sg - both(mg)) for sg, mg in zip(s, m_g)]
    l_g = [halves(jnp.sum(pg, axis=-1, keepdims=True), jnp.add) for pg in p]
    yield
    pv_g = [weighted_values(g, pg) for g, pg in zip(groups, p)]
    m = m_scr[...]
    m_new = functools.reduce(jnp.maximum, m_g, m)
    alpha = jnp.exp(m - m_new)
    l, acc = alpha * l_scr[...], alpha * acc_scr[...]
    for mg, lg, pvg in zip(m_g, l_g, pv_g):
        w = jnp.exp(mg - m_new)
        l, acc = l + w * lg, acc + w * pvg
    m_scr[...] = m_new
    l_scr[...] = l
    acc_scr[...] = acc

    @pl.when(j == n_steps - 1)
    def _():
        kn = jnp.concatenate([kn_ref[...], kn_ref[...]], axis=0)
        vn = jnp.concatenate([vn_ref[...], vn_ref[...]], axis=0)
        s_new = jnp.sum(q8 * kn, axis=-1, keepdims=True)
        m_old = m_scr[...]
        m_fin = jnp.maximum(m_old, s_new)
        a = jnp.exp(m_old - m_fin)
        p_new = jnp.exp(s_new - m_fin)
        l = a * l_scr[...] + p_new
        acc = (a * acc_scr[...] + p_new * vn) / l
        o_ref[...] = _sub_ln(acc[:H] - lam * acc[H:], g_row)


def _tile_order(p, n):
    return jnp.where(p % 2 == 0, n // 2 - 1 - p // 2, n // 2 + p // 2)


def _attn_kernel(pt_ref, lq1_ref, lk1_ref, lq2_ref, lk2_ref, gcol_ref, grow_ref, q_ref, k_ref, v_ref,
                 qs_ref, kn_ref, vn_ref, ck_hbm, cv_hbm, o_ref, os_ref, m_scr, l_scr, acc_scr, kbuf, vbuf, sem,
                 pml_scr, pacc_scr,
                 *, tq, steps_per_seq, pages):
    i = _tile_order(pl.program_id(1), pl.num_programs(1))
    step = pl.program_id(0) * pl.num_programs(1) + pl.program_id(1)
    n_steps = pl.num_programs(0) * pl.num_programs(1)

    def page_copies(of_step, p):
        slot = of_step % 2
        pid = pt_ref[of_step // steps_per_seq, (of_step % steps_per_seq) * pages + p]
        return (pltpu.make_async_copy(ck_hbm.at[pid], kbuf.at[slot, p], sem.at[slot, 0]),
                pltpu.make_async_copy(cv_hbm.at[pid], vbuf.at[slot, p], sem.at[slot, 1]))

    def fetch(of_step, do, unrolled):
        def one(p):
            for c in page_copies(of_step, p):
                do(c)
        if unrolled:
            for p in range(pages):
                one(p)
        else:
            lax.fori_loop(0, pages, lambda p, _: (one(p), 0)[1], 0)

    start, wait = (lambda c: c.start()), (lambda c: c.wait())

    @pl.when(step == 0)
    def _():
        fetch(step, start, unrolled=False)

    following = lax.rem(step + 1, n_steps)
    fetch(following, start, unrolled=True)
    lam = _lambda(lq1_ref, lk1_ref, lq2_ref, lk2_ref)
    prompt = _prompt_tile(i, lam, gcol_ref[...], q_ref, k_ref, v_ref, o_ref, pml_scr, pacc_scr, tq)
    next(prompt)
    fetch(step, wait, unrolled=True)
    slot = step % 2
    decode = _decode_step(step % steps_per_seq, steps_per_seq, lam, grow_ref[...], qs_ref, kn_ref, vn_ref,
                          [kbuf.at[slot, p] for p in range(pages)], [vbuf.at[slot, p] for p in range(pages)],
                          os_ref, m_scr, l_scr, acc_scr)
    running = [prompt, decode]
    while running:
        for gen in list(running):
            if next(gen, StopIteration) is StopIteration:
                running.remove(gen)

    @pl.when(step == n_steps - 1)
    def _():
        fetch(following, wait, unrolled=False)


def _attn(q, k, v, qs, ks, vs, cache_k, cache_v, page_table, lam_vecs, g, B, S, tq):
    H = N_ATT_HEADS
    nq = S // tq
    Bs, n_pages = page_table.shape
    n_pool, page = cache_k.shape[0], cache_k.shape[1]
    steps_per_seq = (B * nq) // Bs
    assert steps_per_seq * Bs == B * nq and n_pages % steps_per_seq == 0
    P = n_pages // steps_per_seq
    ck = cache_k.reshape(n_pool, page * H, HEAD_W)
    cv = cache_v.reshape(n_pool, page * H, HEAD_W)
    seq = lambda b, i: (b * nq + i) // steps_per_seq
    vec = pl.BlockSpec((1, ATT_HEAD_DIM), lambda b, i, pt: (0, 0))
    kv = pl.BlockSpec((S * H, HEAD_W), lambda b, i, pt: (b, 0))
    qo = pl.BlockSpec((tq, D_ATT), lambda b, i, pt: (b * nq + _tile_order(i, nq), 0))
    tok = pl.BlockSpec((None, H, HEAD_W), lambda b, i, pt: (seq(b, i), 0, 0))
    hbm = pl.BlockSpec(memory_space=pl.ANY)
    page_buf = pltpu.VMEM((2, P, page * H, HEAD_W), F32)
    grid_spec = pltpu.PrefetchScalarGridSpec(
        num_scalar_prefetch=1,
        grid=(B, nq),
        in_specs=[vec, vec, vec, vec, pl.BlockSpec((HEAD_W, 1), lambda b, i, pt: (0, 0)),
                  pl.BlockSpec((1, HEAD_W), lambda b, i, pt: (0, 0)), qo, kv, kv, tok, tok, tok, hbm, hbm],
        out_specs=[qo, tok],
        scratch_shapes=[pltpu.VMEM((2 * H, 1), F32), pltpu.VMEM((2 * H, 1), F32), pltpu.VMEM((2 * H, HEAD_W), F32),
                        page_buf, page_buf, pltpu.SemaphoreType.DMA((2, 2)),
                        pltpu.VMEM((4 * H, tq), F32), pltpu.VMEM((2 * H, HEAD_W, tq), F32)],
    )
    tok3 = lambda a: a.reshape(Bs, H, HEAD_W)
    o, o_s = pl.pallas_call(
        functools.partial(_attn_kernel, tq=tq, steps_per_seq=steps_per_seq, pages=P),
        grid_spec=grid_spec,
        out_shape=[jax.ShapeDtypeStruct((B * S, D_ATT), F32), jax.ShapeDtypeStruct((Bs, H, HEAD_W), F32)],
        compiler_params=_params("arbitrary", "arbitrary"),
        name="attn",
    )(page_table, *lam_vecs, g.reshape(HEAD_W, 1), g.reshape(1, HEAD_W), q, k, v, tok3(qs), tok3(ks), tok3(vs), ck, cv)
    return o, o_s.reshape(Bs, D_ATT)


SCAN_COLS = 1024
B_GROUPS = 16
C_GROUPS = 8


def _ssm_kernel(u_ref, bb_ref, lre_ref, lim_ref, c_ref, d_ref, wglu_ref, *refs, tc, bk, fresh):
    n_h0 = 0 if fresh else 2
    n_cast = (len(refs) - n_h0 - 5) // 2
    h0_refs, cast_refs = refs[:n_h0], refs[n_h0:n_h0 + n_cast]
    z_ref, hre_ref, him_ref = refs[n_h0 + n_cast:n_h0 + n_cast + 3]
    bu_scr, hs_scr = refs[-2:]
    _convert_rows(cast_refs, refs[n_h0 + n_cast + 3:-2])
    t0 = pl.program_id(0)

    @pl.when(t0 == 0)
    def _():
        for h_ref, h0_ref in zip((hre_ref, him_ref), h0_refs or (None, None)):
            h_ref[...] = jnp.zeros(h_ref.shape, F32) if fresh else h0_ref[...]

    rows = tc * bk
    u = u_ref[...].reshape(rows, D_SSM)
    ub = u.astype(BF16)
    part_w = B_GROUPS * SSM_STATE
    half_w = 2 * part_w
    n_half = N_SSM_GROUPS // B_GROUPS
    slab_w = C_GROUPS * SSM_STATE

    def input_map(half):
        u_half = ub[:, half * B_GROUPS * SSM_GROUP:(half + 1) * B_GROUPS * SSM_GROUP]
        bu = jnp.dot(u_half, bb_ref[half], preferred_element_type=F32)
        bu_scr[:, :, half * half_w:(half + 1) * half_w] = bu.reshape(tc, bk, half_w)

    def scan(half):
        for cg in range(part_w // SCAN_COLS):
            st = slice(half * part_w + cg * SCAN_COLS, half * part_w + (cg + 1) * SCAN_COLS)
            re = slice(half * half_w + cg * SCAN_COLS, half * half_w + (cg + 1) * SCAN_COLS)
            im = slice(re.start + part_w, re.stop + part_w)
            lr = jnp.broadcast_to(lre_ref[:, st], (bk, SCAN_COLS))
            li = jnp.broadcast_to(lim_ref[:, st], (bk, SCAN_COLS))
            hr, hi = hre_ref[:, st], him_ref[:, st]
            for t in range(tc):
                hr, hi = lr * hr - li * hi + bu_scr[t, :, re], lr * hi + li * hr + bu_scr[t, :, im]
                hs_scr[t, :, re] = hr
                hs_scr[t, :, im] = hi
            hre_ref[:, st] = hr
            him_ref[:, st] = hi

    def output_map(s):
        re0 = (s * C_GROUPS // B_GROUPS) * half_w + (s * C_GROUPS % B_GROUPS) * SSM_STATE
        h_re = hs_scr[:, :, re0:re0 + slab_w].reshape(rows, slab_w).astype(BF16)
        h_im = hs_scr[:, :, re0 + part_w:re0 + part_w + slab_w].reshape(rows, slab_w).astype(BF16)
        return (jnp.dot(h_re, c_ref[s, 0], preferred_element_type=F32)
                + jnp.dot(h_im, c_ref[s, 1], preferred_element_type=F32))

    slabs_per_half = B_GROUPS // C_GROUPS
    input_map(0)
    y_slabs = []
    for half in range(n_half):
        if half + 1 < n_half:
            input_map(half + 1)
        scan(half)
        y_slabs += [output_map(half * slabs_per_half + s) for s in range(slabs_per_half)]
    y = jnp.concatenate(y_slabs, axis=1) + d_ref[...] * u
    y = jax.nn.gelu(y)
    gate = jnp.dot(y.astype(BF16), wglu_ref[...], preferred_element_type=F32)
    z_ref[...] = (y * jax.nn.sigmoid(gate)).reshape(tc, bk, D_SSM)


def _ssm(u3, bb, lb_re, lb_im, c_full, d, w_glu, h0, tc, cast=()):
    S, bk, _ = u3.shape
    const = lambda shape: pl.BlockSpec(shape, lambda t: (0,) * len(shape))
    seq = pl.BlockSpec((tc, bk, D_SSM), lambda t: (t, 0, 0))
    h0 = tuple(h0 or ())
    cast_specs = _convert_specs(cast, S // tc, lambda t: t)
    z, h_re, h_im, *converted = pl.pallas_call(
        functools.partial(_ssm_kernel, tc=tc, bk=bk, fresh=not h0),
        grid=(S // tc,),
        in_specs=[seq, const(bb.shape), const((1, N_STATE)), const((1, N_STATE)),
                  const(c_full.shape), const((1, D_SSM)), const((D_SSM, D_SSM))] + [const((bk, N_STATE))] * len(h0)
                 + cast_specs,
        out_specs=[seq, const((bk, N_STATE)), const((bk, N_STATE))] + cast_specs,
        out_shape=[jax.ShapeDtypeStruct((S, bk, D_SSM), F32), jax.ShapeDtypeStruct((bk, N_STATE), F32),
                   jax.ShapeDtypeStruct((bk, N_STATE), F32)] + [jax.ShapeDtypeStruct(c.shape, BF16) for c in cast],
        scratch_shapes=[pltpu.VMEM((tc, bk, 2 * N_STATE), F32), pltpu.VMEM((tc, bk, 2 * N_STATE), F32)],
        compiler_params=_params("arbitrary"),
        name="ssm",
    )(u3, bb, lb_re.reshape(1, N_STATE), lb_im.reshape(1, N_STATE), c_full, d.reshape(1, D_SSM), w_glu, *h0, *cast)
    return z, h_re, h_im, converted


def _ffn_kernel(*refs, tm, decode):
    if decode:
        (x_ref, o_ref, z_ref, wo_ref, gn_ref, wu_ref, cw_ref, cb_ref, wd_ref, fn_ref, p2_ref, p1_ref,
         y_ref, a_out_ref, hid_scr) = refs
    else:
        (x_ref, o_ref, z_ref, wo_ref, gn_ref, wu_ref, cw_ref, cb_ref, wd_ref, fn_ref,
         y_ref, cs_ref, hid_scr, carry_scr) = refs

        @pl.when(pl.program_id(1) == 0)
        def _():
            carry_scr[...] = jnp.zeros(carry_scr.shape, F32)

    mix = (jnp.dot(o_ref[...].astype(BF16), wo_ref[:D_ATT, :], preferred_element_type=F32)
           + jnp.dot(z_ref[...].astype(BF16), wo_ref[D_ATT:, :], preferred_element_type=F32))
    x1 = x_ref[...] + mix
    r = lax.rsqrt(jnp.mean(x1 * x1, axis=-1, keepdims=True) + NORM_EPS)
    hn = (x1 * r * gn_ref[...]).astype(BF16)
    for j in range(N_FF_CHUNKS):
        cols = slice(j * FF_CHUNK, (j + 1) * FF_CHUNK)
        gcols = slice(D_FF + j * FF_CHUNK, D_FF + (j + 1) * FF_CHUNK)
        a = jnp.dot(hn, wu_ref[:, cols], preferred_element_type=F32)
        gate = jnp.dot(hn, wu_ref[:, gcols], preferred_element_type=F32)
        if decode:
            am2, am1 = p2_ref[:, cols], p1_ref[:, cols]
            a_out_ref[:, cols] = a
        else:
            prev = carry_scr[:, cols]
            p2 = jnp.broadcast_to(prev[CARRY_ROWS - 2:CARRY_ROWS - 1], a.shape)
            p1 = jnp.broadcast_to(prev[CARRY_ROWS - 1:], a.shape)
            row = lax.broadcasted_iota(jnp.int32, a.shape, 0)
            am1 = jnp.where(row < 1, p1, pltpu.roll(a, 1, 0))
            am2 = jnp.where(row < 1, p2, jnp.where(row < 2, p1, pltpu.roll(a, 2, 0)))
            carry_scr[:, cols] = a[tm - CARRY_ROWS:]
        cw = cw_ref[:, cols]
        c = cb_ref[:, cols] + (cw[0:1] * am2 + cw[1:2] * am1 + cw[2:3] * a)
        hid_scr[:, cols] = (jax.nn.silu(c) * gate).astype(BF16)
    acc = x1 + jnp.dot(hid_scr[...], wd_ref[...], preferred_element_type=F32)
    r2 = lax.rsqrt(jnp.mean(acc * acc, axis=-1, keepdims=True) + NORM_EPS)
    y_ref[...] = acc * r2 * fn_ref[...]
    if not decode:
        cs_ref[...] = carry_scr[CARRY_ROWS - (CONV_W - 1):, :]


def _ffn(x, o, z, weights, tm, prev=None):
    B, S, D = x.shape
    ns = S // tm
    decode = prev is not None
    wo, gn, wu, cw, cb, wd, fn = weights
    const = lambda shape: pl.BlockSpec(shape, lambda b, s: (0,) * len(shape), pipeline_mode=pl.Buffered(1))
    xs = pl.BlockSpec((None, tm, D), lambda b, s: (b, s, 0))
    in_specs = [xs, pl.BlockSpec((tm, D_ATT), lambda b, s: (b * ns + s, 0)),
                pl.BlockSpec((tm, D_SSM), lambda b, s: (s, b)),
                const((D, D)), const((1, D)), const((D, 2 * D_FF)), const((CONV_W, D_FF)), const((1, D_FF)),
                const((D_FF, D)), const((1, D))]
    args = [x, o, z, wo, gn, wu, cw, cb, wd, fn]
    scratch = [pltpu.VMEM((tm, D_FF), BF16)]
    if decode:
        rows = pl.BlockSpec((tm, D_FF), lambda b, s: (0, 0))
        in_specs += [rows, rows]
        args += list(prev)
        out_specs = [xs, rows]
        out_shape = [jax.ShapeDtypeStruct((B, S, D), F32), jax.ShapeDtypeStruct((tm, D_FF), F32)]
    else:
        out_specs = [xs, pl.BlockSpec((None, CONV_W - 1, D_FF), lambda b, s: (b, 0, 0))]
        out_shape = [jax.ShapeDtypeStruct((B, S, D), F32), jax.ShapeDtypeStruct((B, CONV_W - 1, D_FF), F32)]
        scratch.append(pltpu.VMEM((CARRY_ROWS, D_FF), F32))
    y, extra = pl.pallas_call(
        functools.partial(_ffn_kernel, tm=tm, decode=decode),
        grid=(B, ns),
        in_specs=in_specs, out_specs=out_specs, out_shape=out_shape, scratch_shapes=scratch,
        compiler_params=_params("parallel", "arbitrary"),
        name="ffn_decode" if decode else "ffn",
    )(*args)
    return y, extra


def kernel(x_prompt, x_sample, cache_k, cache_v, state_ssm_re, state_ssm_im, state_conv, page_table, norm_mix, w_in, lambda_q1, lambda_k1, lambda_q2, lambda_k2, subln_g, ssm_a_re, ssm_a_im, ssm_log_dt, ssm_b_re, ssm_b_im, ssm_c_re, ssm_c_im, ssm_d, w_glu, w_o, norm_ffn, w_up, conv_w, conv_b, w_down, final_norm):
    assert norm_mix.shape[0] == 1, "single layer"
    Bp, Sp, D = x_prompt.shape
    Bs = x_sample.shape[0]
    G, N, C = N_SSM_GROUPS, SSM_STATE, SSM_GROUP
    past_len = page_table.shape[1] * cache_k.shape[2]

    lam_vecs = [v.reshape(1, ATT_HEAD_DIM) for v in (lambda_q1[0], lambda_k1[0], lambda_q2[0], lambda_k2[0])]
    g_sub = subln_g[0].reshape(1, HEAD_W)

    lb_re, lb_im, bb_blk, c_blk = _ssm_prep(ssm_a_re[0], ssm_a_im[0], ssm_log_dt[0], ssm_b_re[0], ssm_b_im[0],
                                            ssm_c_re[0], ssm_c_im[0])

    tabs_p = _rope_tables(np.arange(Sp))
    q, k, v, u, (w_in_b, wglu_b) = _in_proj(x_prompt, norm_mix[0], w_in[0], tabs_p, tm=IN_PROJ_ROWS,
                                            cast=(w_in[0], w_glu[0]))
    tabs_s = _rope_tables(np.full((Bs,), past_len))
    qs, ks, vs, us, _ = _in_proj(x_sample.reshape(1, Bs, D), norm_mix[0], w_in_b, tabs_s, tm=Bs)

    o, os_ = _attn(q, k, v, qs, ks, vs, cache_k[0], cache_v[0], page_table, lam_vecs, g_sub, Bp, Sp, tq=ATTN_Q_ROWS)

    z, hre_p, him_p, (wo_b, wup_b, wdown_b) = _ssm(
        u.reshape(Sp, Bp, D_SSM), bb_blk, lb_re, lb_im, c_blk, ssm_d[0], wglu_b, None, tc=SSM_STEPS,
        cast=(w_o[0], w_up[0], w_down[0]))
    ffn_w = (wo_b, norm_ffn[0].reshape(1, D), wup_b, conv_w[0], conv_b[0].reshape(1, D_FF), wdown_b,
             final_norm.reshape(1, D))
    y_prompt, cs_p = _ffn(x_prompt, o, z.reshape(Sp, Bp * D_SSM), ffn_w, tm=FFN_ROWS)
    conv_prompt = cs_p

    zs, hre_s, him_s, _ = _ssm(us.reshape(1, Bs, D_SSM), bb_blk, lb_re, lb_im, c_blk, ssm_d[0], wglu_b,
                            (state_ssm_re[0].reshape(Bs, N_STATE), state_ssm_im[0].reshape(Bs, N_STATE)), tc=1)
    sc = state_conv[0]
    y_s, a_s = _ffn(x_sample.reshape(1, Bs, D), os_, zs.reshape(Bs, D_SSM), ffn_w, tm=Bs, prev=(sc[:, 0], sc[:, 1]))
    conv_sample = jnp.stack([sc[:, 1], a_s], axis=1)

    st = lambda h, b: h.reshape(1, b, G, N)
    return (y_prompt, y_s.reshape(Bs, 1, D),
            k.reshape(1, Bp, Sp, N_ATT_HEADS, HEAD_W), v.reshape(1, Bp, Sp, N_ATT_HEADS, HEAD_W),
            st(hre_p, Bp), st(him_p, Bp), conv_prompt[None],
            ks.reshape(1, Bs, 1, N_ATT_HEADS, HEAD_W), vs.reshape(1, Bs, 1, N_ATT_HEADS, HEAD_W),
            st(hre_s, Bs), st(him_s, Bs), conv_sample[None])
```

```python
import functools
import math

import jax
import jax.numpy as jnp
import numpy as np
from jax import lax
from jax.experimental import pallas as pl
from jax.experimental.pallas import tpu as pltpu

F32 = jnp.float32
BF16 = jnp.bfloat16

D_MODEL = 1024
D_ATT = 512
D_SSM = 512
ATT_HEAD_DIM = 64
N_ATT_HEADS = 4
HEAD_W = 2 * ATT_HEAD_DIM
ROT_DIM = 16
ROPE_THETA = 500000.0
SSM_GROUP = 16
N_SSM_GROUPS = 32
SSM_STATE = 64
N_STATE = N_SSM_GROUPS * SSM_STATE
D_FF = 2816
CONV_W = 3
NORM_EPS = 1e-6
SUBLN_EPS = 1e-5
NEG_INF = -1e30
D_PROJ = 3 * D_ATT + D_SSM
LAMBDA_INIT = 0.8 - 0.6 * math.exp(-0.3 * 0)
LOG2_E = math.log2(math.e)

V7X_VMEM_LIMIT = 58 * 1024 * 1024
IN_PROJ_ROWS = 1024
ATTN_Q_ROWS = 256
SSM_STEPS = 128
FFN_ROWS = 512
FF_CHUNK = 256
N_FF_CHUNKS = D_FF // FF_CHUNK
BF16_SUBLANES = 16
DECODE_GROUPS = 2
CARRY_ROWS = 8


def _params(*sem):
    return pltpu.CompilerParams(dimension_semantics=sem, vmem_limit_bytes=V7X_VMEM_LIMIT)


def _convert_specs(weights, n_steps, step_of):
    assert all(w.shape[0] % (n_steps * BF16_SUBLANES) == 0 for w in weights)
    return [pl.BlockSpec((w.shape[0] // n_steps, w.shape[1]), lambda *idx: (step_of(*idx), 0)) for w in weights]


def _convert_rows(src_refs, dst_refs):
    for src_ref, dst_ref in zip(src_refs, dst_refs):
        dst_ref[...] = src_ref[...].astype(BF16)


def _lambda(lq1_ref, lk1_ref, lq2_ref, lk2_ref):
    s1 = jnp.sum(lq1_ref[...] * lk1_ref[...], axis=-1, keepdims=True)
    s2 = jnp.sum(lq2_ref[...] * lk2_ref[...], axis=-1, keepdims=True)
    return jnp.exp(s1) - jnp.exp(s2) + LAMBDA_INIT


def _sub_ln(o, g):
    r = lax.rsqrt(jnp.mean(o * o, axis=-1, keepdims=True) + SUBLN_EPS)
    return (o * r * g) * (1.0 - LAMBDA_INIT)


def _ssm_prep_kernel(ar_ref, ai_ref, ldt_ref, arr_ref, air_ref, ldtr_ref, br_ref, bi_ref, cre_ref, cim_ref,
                     lbre_ref, lbim_ref, bb_ref, c_ref):
    G, N, C = N_SSM_GROUPS, SSM_STATE, SSM_GROUP

    def disc(ar, ai, dt):
        mag = jnp.exp(ar * dt)
        return mag * jnp.cos(ai * dt), mag * jnp.sin(ai * dt)

    lb_re, lb_im = disc(ar_ref[...], ai_ref[...], jnp.exp(ldt_ref[...]))
    lbre_ref[...] = lb_re
    lbim_ref[...] = lb_im
    ar, ai = arr_ref[...], air_ref[...]
    t_re, t_im = disc(ar, ai, jnp.exp(ldtr_ref[...]))
    den = ar * ar + ai * ai
    nr, ni = t_re - 1.0, t_im
    f_re = (nr * ar + ni * ai) / den
    f_im = (ni * ar - nr * ai) / den
    br, bi = br_ref[...], bi_ref[...]
    bb = ((f_re * br - f_im * bi).astype(BF16), (f_re * bi + f_im * br).astype(BF16))

    def iota(shape, axis):
        return lax.broadcasted_iota(jnp.int32, shape, axis)

    half_rows, part_w = B_GROUPS * C, B_GROUPS * N
    spread = (iota((N, part_w), 1) % N == iota((N, part_w), 0)).astype(BF16)
    own = iota((half_rows, part_w), 0) // C == iota((half_rows, part_w), 1) // N
    for half in range(G // B_GROUPS):
        for part in range(2):
            tiled = jnp.dot(bb[part][half * half_rows:(half + 1) * half_rows], spread, preferred_element_type=F32)
            bb_ref[half, :, part * part_w:(part + 1) * part_w] = jnp.where(own, tiled, 0.0).astype(BF16)

    slab_rows, slab_w = C_GROUPS * N, C_GROUPS * C
    eye_n = (iota((N, N), 0) == iota((N, N), 1)).astype(BF16)
    own_c = iota((slab_rows, slab_w), 0) // N == iota((slab_rows, slab_w), 1) // C
    for part, (c_nat_ref, sign) in enumerate(((cre_ref, 1.0), (cim_ref, -1.0))):
        c_t = lax.dot_general(eye_n, c_nat_ref[...].astype(BF16), (((1,), (1,)), ((), ())),
                              preferred_element_type=F32)
        for s in range(G // C_GROUPS):
            slots = jnp.tile(c_t[:, s * slab_w:(s + 1) * slab_w], (C_GROUPS, 1))
            c_ref[s, part] = jnp.where(own_c, sign * slots, 0.0).astype(BF16)


def _ssm_prep(a_re, a_im, log_dt, b_re, b_im, c_re, c_im):
    G, N, C = N_SSM_GROUPS, SSM_STATE, SSM_GROUP
    per_channel = lambda a: jnp.repeat(a, C, axis=0)
    b_rows = lambda b: jnp.transpose(b, (0, 2, 1)).reshape(G * C, N)
    c_rows = lambda c: c.reshape(G * C, N)
    ldt = log_dt.reshape(G, 1)
    return pl.pallas_call(
        _ssm_prep_kernel,
        out_shape=[jax.ShapeDtypeStruct((G, N), F32)] * 2
                  + [jax.ShapeDtypeStruct((G // B_GROUPS, B_GROUPS * C, 2 * B_GROUPS * N), BF16),
                     jax.ShapeDtypeStruct((G // C_GROUPS, 2, C_GROUPS * N, C_GROUPS * C), BF16)],
        name="ssm_prep",
    )(a_re, a_im, ldt, per_channel(a_re), per_channel(a_im), per_channel(ldt), b_rows(b_re), b_rows(b_im),
      c_rows(c_re), c_rows(c_im))


def _in_proj_kernel(x_ref, g_ref, w_ref, cos_ref, sa_ref, sb_ref, *rest):
    n_cast = (len(rest) - 4) // 2
    q_ref, k_ref, v_ref, u_ref = rest[n_cast:n_cast + 4]
    _convert_rows(rest[:n_cast], rest[n_cast + 4:])
    tm = x_ref.shape[0]
    x = x_ref[...]
    r = lax.rsqrt(jnp.mean(x * x, axis=-1, keepdims=True) + NORM_EPS)
    hn = (x * r * g_ref[...]).astype(BF16)
    proj = jnp.dot(hn, w_ref[...].astype(BF16), preferred_element_type=F32)
    cos, sa, sb = cos_ref[...], sa_ref[...], sb_ref[...]
    half = ROT_DIM // 2

    def rope(xh):
        return xh * cos + pltpu.roll(xh, HEAD_W - half, 1) * sa + pltpu.roll(xh, half, 1) * sb

    for h in range(N_ATT_HEADS):
        lo, hi = h * HEAD_W, (h + 1) * HEAD_W
        q_ref[:, lo:hi] = rope(proj[:, lo:hi]) * (ATT_HEAD_DIM ** -0.5)
        k_ref[pl.ds(h, tm, stride=N_ATT_HEADS), :] = rope(proj[:, D_ATT + lo:D_ATT + hi])
        v_ref[pl.ds(h, tm, stride=N_ATT_HEADS), :] = proj[:, 2 * D_ATT + lo:2 * D_ATT + hi]
    u_ref[...] = proj[:, 3 * D_ATT:]


def _rope_tables(pos):
    half = ROT_DIM // 2
    inv = ROPE_THETA ** (-np.arange(0, ROT_DIM, 2, dtype=np.float64) / ROT_DIM)
    ang = np.asarray(pos, np.float64)[:, None] * inv[None, :]
    cos, sin = np.cos(ang), np.sin(ang)
    S = ang.shape[0]
    pad = np.zeros((S, ATT_HEAD_DIM - ROT_DIM))
    zh = np.zeros((S, half))
    comp = lambda a, b, c: np.concatenate([a, b, c], axis=1)
    two = lambda t: jnp.asarray(np.concatenate([t, t], axis=1), F32)
    return (two(comp(cos, cos, pad + 1.0)), two(comp(-sin, zh, pad)), two(comp(zh, sin, pad)))


def _in_proj(x, g, w, tables, tm, cast=()):
    B, S, D = x.shape
    ns = S // tm
    cast_specs = _convert_specs(cast, B * ns, lambda b, s: b * ns + s)
    row = lambda b, s: (b * ns + s, 0)
    tab = pl.BlockSpec((tm, HEAD_W), lambda b, s: (s, 0))
    q_spec = pl.BlockSpec((tm, D_ATT), row)
    kv_spec = pl.BlockSpec((tm * N_ATT_HEADS, HEAD_W), row)
    kv_shape = jax.ShapeDtypeStruct((B * S * N_ATT_HEADS, HEAD_W), F32)
    q, k, v, u, *converted = pl.pallas_call(
        _in_proj_kernel,
        grid=(B, ns),
        in_specs=[pl.BlockSpec((None, tm, D), lambda b, s: (b, s, 0)),
                  pl.BlockSpec((1, D), lambda b, s: (0, 0)),
                  pl.BlockSpec((D, D_PROJ), lambda b, s: (0, 0)),
                  tab, tab, tab] + cast_specs,
        out_specs=[q_spec, kv_spec, kv_spec, pl.BlockSpec((tm, D_SSM), lambda b, s: (s, b))] + cast_specs,
        out_shape=[jax.ShapeDtypeStruct((B * S, D_ATT), F32), kv_shape, kv_shape,
                   jax.ShapeDtypeStruct((S, B * D_SSM), F32)] + [jax.ShapeDtypeStruct(c.shape, BF16) for c in cast],
        compiler_params=_params("parallel", "parallel"),
        name="in_proj",
    )(x, g.reshape(1, D), w, *tables, *cast)
    return q, k, v, u, converted


def _prompt_tile(i, lam, g_col, q_ref, k_ref, v_ref, o_ref, ml_scr, acc_scr, tq):
    nh = N_ATT_HEADS
    feat = lax.broadcasted_iota(jnp.int32, (HEAD_W, tq), 0)
    units = [(h, c) for h in range(nh) for c in range(2)]
    hs = lambda h: slice(h * HEAD_W, (h + 1) * HEAD_W)
    q_t = []
    for h, c in units:
        qh_t = (q_ref[:, hs(h)] * LOG2_E).T
        own = (feat < ATT_HEAD_DIM) if c == 0 else (feat >= ATT_HEAD_DIM)
        q_t.append(jnp.where(own, qh_t, 0.0))

    n = range(len(units))
    rows = lambda start, size, h: pl.ds(pl.multiple_of(start, tq) * nh + h, size, stride=nh)

    def scores(start, size, masked):
        s = [jnp.dot(k_ref[rows(start, size, h), :], q_t[u], preferred_element_type=F32)
             for u, (h, c) in enumerate(units)]
        if masked:
            key = lax.broadcasted_iota(jnp.int32, (size, tq), 0)
            qry = lax.broadcasted_iota(jnp.int32, (size, tq), 1)
            s = [jnp.where(key <= qry, su, NEG_INF) for su in s]
        return s

    def softmax(s, carry):
        m_new = [jnp.maximum(carry[u][0], jnp.max(s[u], axis=0, keepdims=True)) for u in n]
        alpha = [jnp.exp2(carry[u][0] - m_new[u]) for u in n]
        return m_new, alpha, [jnp.exp2(s[u] - m_new[u]) for u in n]

    def update(start, size, carry, m_new, alpha, p):
        ones_rows = jnp.ones((8, size), F32)
        pv = [jnp.dot(jnp.concatenate([v_ref[rows(start, size, h), :].T, ones_rows], axis=0), p[u],
                      preferred_element_type=F32) for u, (h, c) in enumerate(units)]
        return tuple((m_new[u], alpha[u] * carry[u][1] + pv[u][HEAD_W:HEAD_W + 1],
                      alpha[u] * carry[u][2] + pv[u][:HEAD_W]) for u in n)

    nu = len(units)
    load = lambda: tuple((ml_scr[u:u + 1, :], ml_scr[nu + u:nu + u + 1, :], acc_scr[u]) for u in n)

    def store(carry):
        for u, (m, l, acc) in enumerate(carry):
            ml_scr[u:u + 1, :] = m
            ml_scr[nu + u:nu + u + 1, :] = l
            acc_scr[u] = acc

    def block(start, size):
        carry = load()
        store(update(start, size, carry, *softmax(scores(start, size, False), carry)))

    ml_scr[:nu, :] = jnp.full((nu, tq), NEG_INF, F32)
    ml_scr[nu:, :] = jnp.zeros((nu, tq), F32)
    acc_scr[...] = jnp.zeros(acc_scr.shape, F32)
    lax.fori_loop(0, i // 2, lambda j, _: (block(2 * j * tq, 2 * tq), 0)[1], 0)

    @pl.when(i % 2 == 1)
    def _():
        block((i - 1) * tq, tq)

    yield
    carry = load()
    s = scores(i * tq, tq, True)
    yield
    stats = softmax(s, carry)
    yield
    carry = update(i * tq, tq, carry, *stats)
    for h in range(nh):
        (_, l1, acc1), (_, l2, acc2) = carry[2 * h], carry[2 * h + 1]
        o_t = acc1 / l1 - lam * (acc2 / l2)
        r = lax.rsqrt(jnp.mean(o_t * o_t, axis=0, keepdims=True) + SUBLN_EPS)
        o_ref[:, h * HEAD_W:(h + 1) * HEAD_W] = ((o_t * r * g_col) * (1.0 - LAMBDA_INIT)).T


def _decode_step(j, n_steps, lam, g_row, q_ref, kn_ref, vn_ref, k_refs, v_refs, o_ref, m_scr, l_scr, acc_scr):
    H = N_ATT_HEADS

    @pl.when(j == 0)
    def _():
        m_scr[...] = jnp.full(m_scr.shape, NEG_INF, F32)
        l_scr[...] = jnp.zeros(l_scr.shape, F32)
        acc_scr[...] = jnp.zeros(acc_scr.shape, F32)

    q = q_ref[...]
    lane = lax.broadcasted_iota(jnp.int32, q.shape, 1)
    q8 = jnp.concatenate([jnp.where(lane < ATT_HEAD_DIM, q, 0.0), jnp.where(lane >= ATT_HEAD_DIM, q, 0.0)], axis=0)
    rows = k_refs[0].shape[0]
    zero = jnp.zeros_like(q8)
    pad = jnp.zeros((HEAD_W - 4 * H, 2 * HEAD_W), F32)
    q_pair = jnp.concatenate([jnp.concatenate([q8, zero], axis=1), jnp.concatenate([zero, q8], axis=1), pad], axis=0)
    r = lax.broadcasted_iota(jnp.int32, (4 * H, rows), 0)
    col = lax.broadcasted_iota(jnp.int32, (4 * H, rows), 1)
    own = (col % H) == (r % H)
    both = lambda x: jnp.concatenate([x, x], axis=0)
    halves = lambda x, op: op(x[:2 * H], x[2 * H:])

    def scores(group):
        k_pairs = jnp.concatenate([jnp.concatenate([k_refs[a][...], k_refs[a + 1][...]], axis=1) for a in group],
                                  axis=0)
        s_t = lax.dot_general(k_pairs, q_pair, (((1,), (1,)), ((), ())), preferred_element_type=F32)
        return jnp.concatenate([jnp.where(own, s_t[n * rows:(n + 1) * rows].T[:4 * H], NEG_INF)
                                for n in range(len(group))], axis=1)

    def weighted_values(group, p):
        pv = None
        for n, a in enumerate(group):
            blk = slice(n * rows, (n + 1) * rows)
            part = (jnp.dot(p[:2 * H, blk], v_refs[a][...], preferred_element_type=F32)
                    + jnp.dot(p[2 * H:, blk], v_refs[a + 1][...], preferred_element_type=F32))
            pv = part if pv is None else pv + part
        return pv

    pairs = list(range(0, len(k_refs), 2))
    groups = [pairs[g::DECODE_GROUPS] for g in range(DECODE_GROUPS)]
    s = [scores(g) for g in groups]
    yield
    m_g = [halves(jnp.max(sg, axis=-1, keepdims=True), jnp.maximum) for sg in s]
    p = [jnp.exp(sg - both(mg)) for sg, mg in zip(s, m_g)]
    l_g = [halves(jnp.sum(pg, axis=-1, keepdims=True), jnp.add) for pg in p]
    yield
    pv_g = [weighted_values(g, pg) for g, pg in zip(groups, p)]
    m = m_scr[...]
    m_new = functools.reduce(jnp.maximum, m_g, m)
    alpha = jnp.exp(m - m_new)
    l, acc = alpha * l_scr[...], alpha * acc_scr[...]
    for mg, lg, pvg in zip(m_g, l_g, pv_g):
        w = jnp.exp(mg - m_new)
        l, acc = l + w * lg, acc + w * pvg
    m_scr[...] = m_new
    l_scr[...] = l
    acc_scr[...] = acc

    @pl.when(j == n_steps - 1)
    def _():
        kn = jnp.concatenate([kn_ref[...], kn_ref[...]], axis=0)
        vn = jnp.concatenate([vn_ref[...], vn_ref[...]], axis=0)
        s_new = jnp.sum(q8 * kn, axis=-1, keepdims=True)
        m_old = m_scr[...]
        m_fin = jnp.maximum(m_old, s_new)
        a = jnp.exp(m_old - m_fin)
        p_new = jnp.exp(s_new - m_fin)
        l = a * l_scr[...] + p_new
        acc = (a * acc_scr[...] + p_new * vn) / l
        o_ref[...] = _sub_ln(acc[:H] - lam * acc[H:], g_row)


def _tile_order(p, n):
    return jnp.where(p % 2 == 0, n - 1 - p // 2, p // 2)


def _attn_kernel(pt_ref, lq1_ref, lk1_ref, lq2_ref, lk2_ref, gcol_ref, grow_ref, q_ref, k_ref, v_ref,
                 qs_ref, kn_ref, vn_ref, ck_hbm, cv_hbm, o_ref, os_ref, m_scr, l_scr, acc_scr, kbuf, vbuf, sem,
                 pml_scr, pacc_scr,
                 *, tq, steps_per_seq, pages):
    i = _tile_order(pl.program_id(1), pl.num_programs(1))
    step = pl.program_id(0) * pl.num_programs(1) + pl.program_id(1)
    n_steps = pl.num_programs(0) * pl.num_programs(1)

    def start_fetch(of_step, unrolled):
        slot = of_step % 2

        def one(p):
            pid = pt_ref[of_step // steps_per_seq, (of_step % steps_per_seq) * pages + p]
            pltpu.make_async_copy(ck_hbm.at[pid], kbuf.at[slot, p], sem.at[slot, 0]).start()
            pltpu.make_async_copy(cv_hbm.at[pid], vbuf.at[slot, p], sem.at[slot, 1]).start()
        if unrolled:
            for p in range(pages):
                one(p)
        else:
            lax.fori_loop(0, pages, lambda p, _: (one(p), 0)[1], 0)

    def wait_fetch(of_step):
        slot = of_step % 2
        pltpu.make_async_copy(ck_hbm.at[pl.ds(0, pages)], kbuf.at[slot], sem.at[slot, 0]).wait()
        pltpu.make_async_copy(cv_hbm.at[pl.ds(0, pages)], vbuf.at[slot], sem.at[slot, 1]).wait()

    @pl.when(step == 0)
    def _():
        start_fetch(step, unrolled=False)

    following = lax.rem(step + 1, n_steps)
    start_fetch(following, unrolled=True)
    lam = _lambda(lq1_ref, lk1_ref, lq2_ref, lk2_ref)
    prompt = _prompt_tile(i, lam, gcol_ref[...], q_ref, k_ref, v_ref, o_ref, pml_scr, pacc_scr, tq)
    next(prompt)
    wait_fetch(step)
    slot = step % 2
    decode = _decode_step(step % steps_per_seq, steps_per_seq, lam, grow_ref[...], qs_ref, kn_ref, vn_ref,
                          [kbuf.at[slot, p] for p in range(pages)], [vbuf.at[slot, p] for p in range(pages)],
                          os_ref, m_scr, l_scr, acc_scr)
    running = [prompt, decode]
    while running:
        for gen in list(running):
            if next(gen, StopIteration) is StopIteration:
                running.remove(gen)

    @pl.when(step == n_steps - 1)
    def _():
        wait_fetch(following)


def _attn(q, k, v, qs, ks, vs, cache_k, cache_v, page_table, lam_vecs, g, B, S, tq):
    H = N_ATT_HEADS
    nq = S // tq
    Bs, n_pages = page_table.shape
    n_pool, page = cache_k.shape[0], cache_k.shape[1]
    steps_per_seq = (B * nq) // Bs
    assert steps_per_seq * Bs == B * nq and n_pages % steps_per_seq == 0
    P = n_pages // steps_per_seq
    ck = cache_k.reshape(n_pool, page * H, HEAD_W)
    cv = cache_v.reshape(n_pool, page * H, HEAD_W)
    seq = lambda b, i: (b * nq + i) // steps_per_seq
    vec = pl.BlockSpec((1, ATT_HEAD_DIM), lambda b, i, pt: (0, 0))
    kv = pl.BlockSpec((S * H, HEAD_W), lambda b, i, pt: (b, 0))
    qo = pl.BlockSpec((tq, D_ATT), lambda b, i, pt: (b * nq + _tile_order(i, nq), 0))
    tok = pl.BlockSpec((None, H, HEAD_W), lambda b, i, pt: (seq(b, i), 0, 0))
    hbm = pl.BlockSpec(memory_space=pl.ANY)
    page_buf = pltpu.VMEM((2, P, page * H, HEAD_W), F32)
    grid_spec = pltpu.PrefetchScalarGridSpec(
        num_scalar_prefetch=1,
        grid=(B, nq),
        in_specs=[vec, vec, vec, vec, pl.BlockSpec((HEAD_W, 1), lambda b, i, pt: (0, 0)),
                  pl.BlockSpec((1, HEAD_W), lambda b, i, pt: (0, 0)), qo, kv, kv, tok, tok, tok, hbm, hbm],
        out_specs=[qo, tok],
        scratch_shapes=[pltpu.VMEM((2 * H, 1), F32), pltpu.VMEM((2 * H, 1), F32), pltpu.VMEM((2 * H, HEAD_W), F32),
                        page_buf, page_buf, pltpu.SemaphoreType.DMA((2, 2)),
                        pltpu.VMEM((4 * H, tq), F32), pltpu.VMEM((2 * H, HEAD_W, tq), F32)],
    )
    tok3 = lambda a: a.reshape(Bs, H, HEAD_W)
    o, o_s = pl.pallas_call(
        functools.partial(_attn_kernel, tq=tq, steps_per_seq=steps_per_seq, pages=P),
        grid_spec=grid_spec,
        out_shape=[jax.ShapeDtypeStruct((B * S, D_ATT), F32), jax.ShapeDtypeStruct((Bs, H, HEAD_W), F32)],
        compiler_params=_params("arbitrary", "arbitrary"),
        name="attn",
    )(page_table, *lam_vecs, g.reshape(HEAD_W, 1), g.reshape(1, HEAD_W), q, k, v, tok3(qs), tok3(ks), tok3(vs), ck, cv)
    return o, o_s.reshape(Bs, D_ATT)


SCAN_COLS = 1024
B_GROUPS = 16
C_GROUPS = 8


def _ssm_kernel(u_ref, bb_ref, lre_ref, lim_ref, c_ref, d_ref, wglu_ref, *refs, tc, bk, fresh):
    n_h0 = 0 if fresh else 2
    n_cast = (len(refs) - n_h0 - 5) // 2
    h0_refs, cast_refs = refs[:n_h0], refs[n_h0:n_h0 + n_cast]
    z_ref, hre_ref, him_ref = refs[n_h0 + n_cast:n_h0 + n_cast + 3]
    bu_scr, hs_scr = refs[-2:]
    _convert_rows(cast_refs, refs[n_h0 + n_cast + 3:-2])
    t0 = pl.program_id(0)

    @pl.when(t0 == 0)
    def _():
        for h_ref, h0_ref in zip((hre_ref, him_ref), h0_refs or (None, None)):
            h_ref[...] = jnp.zeros(h_ref.shape, F32) if fresh else h0_ref[...]

    rows = tc * bk
    u = u_ref[...].reshape(rows, D_SSM)
    ub = u.astype(BF16)
    part_w = B_GROUPS * SSM_STATE
    half_w = 2 * part_w
    n_half = N_SSM_GROUPS // B_GROUPS
    slab_w = C_GROUPS * SSM_STATE

    def input_map(half):
        u_half = ub[:, half * B_GROUPS * SSM_GROUP:(half + 1) * B_GROUPS * SSM_GROUP]
        bu = jnp.dot(u_half, bb_ref[half], preferred_element_type=F32)
        bu_scr[:, :, half * half_w:(half + 1) * half_w] = bu.reshape(tc, bk, half_w)

    def scan(half):
        for cg in range(part_w // SCAN_COLS):
            st = slice(half * part_w + cg * SCAN_COLS, half * part_w + (cg + 1) * SCAN_COLS)
            re = slice(half * half_w + cg * SCAN_COLS, half * half_w + (cg + 1) * SCAN_COLS)
            im = slice(re.start + part_w, re.stop + part_w)
            lr = jnp.broadcast_to(lre_ref[:, st], (bk, SCAN_COLS))
            li = jnp.broadcast_to(lim_ref[:, st], (bk, SCAN_COLS))
            hr, hi = hre_ref[:, st], him_ref[:, st]
            for t in range(tc):
                hr, hi = lr * hr - li * hi + bu_scr[t, :, re], lr * hi + li * hr + bu_scr[t, :, im]
                hs_scr[t, :, re] = hr
                hs_scr[t, :, im] = hi
            hre_ref[:, st] = hr
            him_ref[:, st] = hi

    def output_map(s):
        re0 = (s * C_GROUPS // B_GROUPS) * half_w + (s * C_GROUPS % B_GROUPS) * SSM_STATE
        h_re = hs_scr[:, :, re0:re0 + slab_w].reshape(rows, slab_w).astype(BF16)
        h_im = hs_scr[:, :, re0 + part_w:re0 + part_w + slab_w].reshape(rows, slab_w).astype(BF16)
        return (jnp.dot(h_re, c_ref[s, 0], preferred_element_type=F32)
                + jnp.dot(h_im, c_ref[s, 1], preferred_element_type=F32))

    slabs_per_half = B_GROUPS // C_GROUPS
    input_map(0)
    y_slabs = []
    for half in range(n_half):
        if half + 1 < n_half:
            input_map(half + 1)
        scan(half)
        y_slabs += [output_map(half * slabs_per_half + s) for s in range(slabs_per_half)]
    y = jnp.concatenate(y_slabs, axis=1) + d_ref[...] * u
    y = jax.nn.gelu(y)
    gate = jnp.dot(y.astype(BF16), wglu_ref[...], preferred_element_type=F32)
    z_ref[...] = (y * jax.nn.sigmoid(gate)).reshape(tc, bk, D_SSM)


def _ssm(u3, bb, lb_re, lb_im, c_full, d, w_glu, h0, tc, cast=()):
    S, bk, _ = u3.shape
    const = lambda shape: pl.BlockSpec(shape, lambda t: (0,) * len(shape))
    seq = pl.BlockSpec((tc, bk, D_SSM), lambda t: (t, 0, 0))
    h0 = tuple(h0 or ())
    cast_specs = _convert_specs(cast, S // tc, lambda t: t)
    z, h_re, h_im, *converted = pl.pallas_call(
        functools.partial(_ssm_kernel, tc=tc, bk=bk, fresh=not h0),
        grid=(S // tc,),
        in_specs=[seq, const(bb.shape), const((1, N_STATE)), const((1, N_STATE)),
                  const(c_full.shape), const((1, D_SSM)), const((D_SSM, D_SSM))] + [const((bk, N_STATE))] * len(h0)
                 + cast_specs,
        out_specs=[seq, const((bk, N_STATE)), const((bk, N_STATE))] + cast_specs,
        out_shape=[jax.ShapeDtypeStruct((S, bk, D_SSM), F32), jax.ShapeDtypeStruct((bk, N_STATE), F32),
                   jax.ShapeDtypeStruct((bk, N_STATE), F32)] + [jax.ShapeDtypeStruct(c.shape, BF16) for c in cast],
        scratch_shapes=[pltpu.VMEM((tc, bk, 2 * N_STATE), F32), pltpu.VMEM((tc, bk, 2 * N_STATE), F32)],
        compiler_params=_params("arbitrary"),
        name="ssm",
    )(u3, bb, lb_re.reshape(1, N_STATE), lb_im.reshape(1, N_STATE), c_full, d.reshape(1, D_SSM), w_glu, *h0, *cast)
    return z, h_re, h_im, converted


def _ffn_kernel(*refs, tm, decode):
    if decode:
        (x_ref, o_ref, z_ref, wo_ref, gn_ref, wu_ref, cw_ref, cb_ref, wd_ref, fn_ref, p2_ref, p1_ref,
         y_ref, a_out_ref, hid_scr) = refs
    else:
        (x_ref, o_ref, z_ref, wo_ref, gn_ref, wu_ref, cw_ref, cb_ref, wd_ref, fn_ref,
         y_ref, cs_ref, hid_scr, carry_scr) = refs

        @pl.when(pl.program_id(1) == 0)
        def _():
            carry_scr[...] = jnp.zeros(carry_scr.shape, F32)

    mix = (jnp.dot(o_ref[...].astype(BF16), wo_ref[:D_ATT, :], preferred_element_type=F32)
           + jnp.dot(z_ref[...].astype(BF16), wo_ref[D_ATT:, :], preferred_element_type=F32))
    x1 = x_ref[...] + mix
    r = lax.rsqrt(jnp.mean(x1 * x1, axis=-1, keepdims=True) + NORM_EPS)
    hn = (x1 * r * gn_ref[...]).astype(BF16)
    for j in range(N_FF_CHUNKS):
        cols = slice(j * FF_CHUNK, (j + 1) * FF_CHUNK)
        gcols = slice(D_FF + j * FF_CHUNK, D_FF + (j + 1) * FF_CHUNK)
        a = jnp.dot(hn, wu_ref[:, cols], preferred_element_type=F32)
        gate = jnp.dot(hn, wu_ref[:, gcols], preferred_element_type=F32)
        if decode:
            am2, am1 = p2_ref[:, cols], p1_ref[:, cols]
            a_out_ref[:, cols] = a
        else:
            prev = carry_scr[:, cols]
            p2 = jnp.broadcast_to(prev[CARRY_ROWS - 2:CARRY_ROWS - 1], a.shape)
            p1 = jnp.broadcast_to(prev[CARRY_ROWS - 1:], a.shape)
            row = lax.broadcasted_iota(jnp.int32, a.shape, 0)
            am1 = jnp.where(row < 1, p1, pltpu.roll(a, 1, 0))
            am2 = jnp.where(row < 1, p2, jnp.where(row < 2, p1, pltpu.roll(a, 2, 0)))
            carry_scr[:, cols] = a[tm - CARRY_ROWS:]
        cw = cw_ref[:, cols]
        c = cb_ref[:, cols] + (cw[0:1] * am2 + cw[1:2] * am1 + cw[2:3] * a)
        hid_scr[:, cols] = (jax.nn.silu(c) * gate).astype(BF16)
    acc = x1 + jnp.dot(hid_scr[...], wd_ref[...], preferred_element_type=F32)
    r2 = lax.rsqrt(jnp.mean(acc * acc, axis=-1, keepdims=True) + NORM_EPS)
    y_ref[...] = acc * r2 * fn_ref[...]
    if not decode:
        cs_ref[...] = carry_scr[CARRY_ROWS - (CONV_W - 1):, :]


def _ffn(x, o, z, weights, tm, prev=None):
    B, S, D = x.shape
    ns = S // tm
    decode = prev is not None
    wo, gn, wu, cw, cb, wd, fn = weights
    const = lambda shape: pl.BlockSpec(shape, lambda b, s: (0,) * len(shape), pipeline_mode=pl.Buffered(1))
    xs = pl.BlockSpec((None, tm, D), lambda b, s: (b, s, 0))
    in_specs = [xs, pl.BlockSpec((tm, D_ATT), lambda b, s: (b * ns + s, 0)),
                pl.BlockSpec((tm, D_SSM), lambda b, s: (s, b)),
                const((D, D)), const((1, D)), const((D, 2 * D_FF)), const((CONV_W, D_FF)), const((1, D_FF)),
                const((D_FF, D)), const((1, D))]
    args = [x, o, z, wo, gn, wu, cw, cb, wd, fn]
    scratch = [pltpu.VMEM((tm, D_FF), BF16)]
    if decode:
        rows = pl.BlockSpec((tm, D_FF), lambda b, s: (0, 0))
        in_specs += [rows, rows]
        args += list(prev)
        out_specs = [xs, rows]
        out_shape = [jax.ShapeDtypeStruct((B, S, D), F32), jax.ShapeDtypeStruct((tm, D_FF), F32)]
    else:
        out_specs = [xs, pl.BlockSpec((None, CONV_W - 1, D_FF), lambda b, s: (b, 0, 0))]
        out_shape = [jax.ShapeDtypeStruct((B, S, D), F32), jax.ShapeDtypeStruct((B, CONV_W - 1, D_FF), F32)]
        scratch.append(pltpu.VMEM((CARRY_ROWS, D_FF), F32))
    y, extra = pl.pallas_call(
        functools.partial(_ffn_kernel, tm=tm, decode=decode),
        grid=(B, ns),
        in_specs=in_specs, out_specs=out_specs, out_shape=out_shape, scratch_shapes=scratch,
        compiler_params=_params("parallel", "arbitrary"),
        name="ffn_decode" if decode else "ffn",
    )(*args)
    return y, extra


def kernel(x_prompt, x_sample, cache_k, cache_v, state_ssm_re, state_ssm_im, state_conv, page_table, norm_mix, w_in, lambda_q1, lambda_k1, lambda_q2, lambda_k2, subln_g, ssm_a_re, ssm_a_im, ssm_log_dt, ssm_b_re, ssm_b_im, ssm_c_re, ssm_c_im, ssm_d, w_glu, w_o, norm_ffn, w_up, conv_w, conv_b, w_down, final_norm):
    assert norm_mix.shape[0] == 1, "single layer"
    Bp, Sp, D = x_prompt.shape
    Bs = x_sample.shape[0]
    G, N, C = N_SSM_GROUPS, SSM_STATE, SSM_GROUP
    past_len = page_table.shape[1] * cache_k.shape[2]

    lam_vecs = [v.reshape(1, ATT_HEAD_DIM) for v in (lambda_q1[0], lambda_k1[0], lambda_q2[0], lambda_k2[0])]
    g_sub = subln_g[0].reshape(1, HEAD_W)

    lb_re, lb_im, bb_blk, c_blk = _ssm_prep(ssm_a_re[0], ssm_a_im[0], ssm_log_dt[0], ssm_b_re[0], ssm_b_im[0],
                                            ssm_c_re[0], ssm_c_im[0])

    tabs_p = _rope_tables(np.arange(Sp))
    q, k, v, u, (w_in_b, wglu_b) = _in_proj(x_prompt, norm_mix[0], w_in[0], tabs_p, tm=IN_PROJ_ROWS,
                                            cast=(w_in[0], w_glu[0]))
    tabs_s = _rope_tables(np.full((Bs,), past_len))
    qs, ks, vs, us, _ = _in_proj(x_sample.reshape(1, Bs, D), norm_mix[0], w_in_b, tabs_s, tm=Bs)

    o, os_ = _attn(q, k, v, qs, ks, vs, cache_k[0], cache_v[0], page_table, lam_vecs, g_sub, Bp, Sp, tq=ATTN_Q_ROWS)

    z, hre_p, him_p, (wo_b, wup_b, wdown_b) = _ssm(
        u.reshape(Sp, Bp, D_SSM), bb_blk, lb_re, lb_im, c_blk, ssm_d[0], wglu_b, None, tc=SSM_STEPS,
        cast=(w_o[0], w_up[0], w_down[0]))
    ffn_w = (wo_b, norm_ffn[0].reshape(1, D), wup_b, conv_w[0], conv_b[0].reshape(1, D_FF), wdown_b,
             final_norm.reshape(1, D))
    y_prompt, cs_p = _ffn(x_prompt, o, z.reshape(Sp, Bp * D_SSM), ffn_w, tm=FFN_ROWS)
    conv_prompt = cs_p

    zs, hre_s, him_s, _ = _ssm(us.reshape(1, Bs, D_SSM), bb_blk, lb_re, lb_im, c_blk, ssm_d[0], wglu_b,
                            (state_ssm_re[0].reshape(Bs, N_STATE), state_ssm_im[0].reshape(Bs, N_STATE)), tc=1)
    sc = state_conv[0]
    y_s, a_s = _ffn(x_sample.reshape(1, Bs, D), os_, zs.reshape(Bs, D_SSM), ffn_w, tm=Bs, prev=(sc[:, 0], sc[:, 1]))
    conv_sample = jnp.stack([sc[:, 1], a_s], axis=1)

    st = lambda h, b: h.reshape(1, b, G, N)
    return (y_prompt, y_s.reshape(Bs, 1, D),
            k.reshape(1, Bp, Sp, N_ATT_HEADS, HEAD_W), v.reshape(1, Bp, Sp, N_ATT_HEADS, HEAD_W),
            st(hre_p, Bp), st(him_p, Bp), conv_prompt[None],
            ks.reshape(1, Bs, 1, N_ATT_HEADS, HEAD_W), vs.reshape(1, Bs, 1, N_ATT_HEADS, HEAD_W),
            st(hre_s, Bs), st(him_s, Bs), conv_sample[None])
```

```python
import functools
import math

import jax
import jax.numpy as jnp
import numpy as np
from jax import lax
from jax.experimental import pallas as pl
from jax.experimental.pallas import tpu as pltpu

F32 = jnp.float32
BF16 = jnp.bfloat16

D_MODEL = 1024
D_ATT = 512
D_SSM = 512
ATT_HEAD_DIM = 64
N_ATT_HEADS = 4
HEAD_W = 2 * ATT_HEAD_DIM
ROT_DIM = 16
ROPE_THETA = 500000.0
SSM_GROUP = 16
N_SSM_GROUPS = 32
SSM_STATE = 64
N_STATE = N_SSM_GROUPS * SSM_STATE
D_FF = 2816
CONV_W = 3
NORM_EPS = 1e-6
SUBLN_EPS = 1e-5
NEG_INF = -1e30
D_PROJ = 3 * D_ATT + D_SSM
LAMBDA_INIT = 0.8 - 0.6 * math.exp(-0.3 * 0)
LOG2_E = math.log2(math.e)

V7X_VMEM_LIMIT = 58 * 1024 * 1024
IN_PROJ_ROWS = 1024
ATTN_Q_ROWS = 256
SSM_STEPS = 128
FFN_ROWS = 512
FF_CHUNK = 256
N_FF_CHUNKS = D_FF // FF_CHUNK
BF16_SUBLANES = 16
DECODE_GROUPS = 2
CARRY_ROWS = 8


def _params(*sem):
    return pltpu.CompilerParams(dimension_semantics=sem, vmem_limit_bytes=V7X_VMEM_LIMIT)


def _convert_specs(weights, n_steps, step_of):
    assert all(w.shape[0] % (n_steps * BF16_SUBLANES) == 0 for w in weights)
    return [pl.BlockSpec((w.shape[0] // n_steps, w.shape[1]), lambda *idx: (step_of(*idx), 0)) for w in weights]


def _convert_rows(src_refs, dst_refs):
    for src_ref, dst_ref in zip(src_refs, dst_refs):
        dst_ref[...] = src_ref[...].astype(BF16)


def _lambda(lq1_ref, lk1_ref, lq2_ref, lk2_ref):
    s1 = jnp.sum(lq1_ref[...] * lk1_ref[...], axis=-1, keepdims=True)
    s2 = jnp.sum(lq2_ref[...] * lk2_ref[...], axis=-1, keepdims=True)
    return jnp.exp(s1) - jnp.exp(s2) + LAMBDA_INIT


def _sub_ln(o, g):
    r = lax.rsqrt(jnp.mean(o * o, axis=-1, keepdims=True) + SUBLN_EPS)
    return (o * r * g) * (1.0 - LAMBDA_INIT)


def _ssm_prep_kernel(ar_ref, ai_ref, ldt_ref, arr_ref, air_ref, ldtr_ref, br_ref, bi_ref, cre_ref, cim_ref,
                     lbre_ref, lbim_ref, bb_ref, c_ref):
    G, N, C = N_SSM_GROUPS, SSM_STATE, SSM_GROUP

    def disc(ar, ai, dt):
        mag = jnp.exp(ar * dt)
        return mag * jnp.cos(ai * dt), mag * jnp.sin(ai * dt)

    lb_re, lb_im = disc(ar_ref[...], ai_ref[...], jnp.exp(ldt_ref[...]))
    lbre_ref[...] = lb_re
    lbim_ref[...] = lb_im
    ar, ai = arr_ref[...], air_ref[...]
    t_re, t_im = disc(ar, ai, jnp.exp(ldtr_ref[...]))
    den = ar * ar + ai * ai
    nr, ni = t_re - 1.0, t_im
    f_re = (nr * ar + ni * ai) / den
    f_im = (ni * ar - nr * ai) / den
    br, bi = br_ref[...], bi_ref[...]
    bb = ((f_re * br - f_im * bi).astype(BF16), (f_re * bi + f_im * br).astype(BF16))

    def iota(shape, axis):
        return lax.broadcasted_iota(jnp.int32, shape, axis)

    half_rows, part_w = B_GROUPS * C, B_GROUPS * N
    spread = (iota((N, part_w), 1) % N == iota((N, part_w), 0)).astype(BF16)
    own = iota((half_rows, part_w), 0) // C == iota((half_rows, part_w), 1) // N
    for half in range(G // B_GROUPS):
        for part in range(2):
            tiled = jnp.dot(bb[part][half * half_rows:(half + 1) * half_rows], spread, preferred_element_type=F32)
            bb_ref[half, :, part * part_w:(part + 1) * part_w] = jnp.where(own, tiled, 0.0).astype(BF16)

    slab_rows, slab_w = C_GROUPS * N, C_GROUPS * C
    eye_n = (iota((N, N), 0) == iota((N, N), 1)).astype(BF16)
    own_c = iota((slab_rows, slab_w), 0) // N == iota((slab_rows, slab_w), 1) // C
    for part, (c_nat_ref, sign) in enumerate(((cre_ref, 1.0), (cim_ref, -1.0))):
        c_t = lax.dot_general(eye_n, c_nat_ref[...].astype(BF16), (((1,), (1,)), ((), ())),
                              preferred_element_type=F32)
        for s in range(G // C_GROUPS):
            slots = jnp.tile(c_t[:, s * slab_w:(s + 1) * slab_w], (C_GROUPS, 1))
            c_ref[s, part] = jnp.where(own_c, sign * slots, 0.0).astype(BF16)


def _ssm_prep(a_re, a_im, log_dt, b_re, b_im, c_re, c_im):
    G, N, C = N_SSM_GROUPS, SSM_STATE, SSM_GROUP
    per_channel = lambda a: jnp.repeat(a, C, axis=0)
    b_rows = lambda b: jnp.transpose(b, (0, 2, 1)).reshape(G * C, N)
    c_rows = lambda c: c.reshape(G * C, N)
    ldt = log_dt.reshape(G, 1)
    return pl.pallas_call(
        _ssm_prep_kernel,
        out_shape=[jax.ShapeDtypeStruct((G, N), F32)] * 2
                  + [jax.ShapeDtypeStruct((G // B_GROUPS, B_GROUPS * C, 2 * B_GROUPS * N), BF16),
                     jax.ShapeDtypeStruct((G // C_GROUPS, 2, C_GROUPS * N, C_GROUPS * C), BF16)],
        name="ssm_prep",
    )(a_re, a_im, ldt, per_channel(a_re), per_channel(a_im), per_channel(ldt), b_rows(b_re), b_rows(b_im),
      c_rows(c_re), c_rows(c_im))


def _in_proj_kernel(x_ref, g_ref, w_ref, cos_ref, sa_ref, sb_ref, *rest):
    n_cast = (len(rest) - 4) // 2
    q_ref, k_ref, v_ref, u_ref = rest[n_cast:n_cast + 4]
    _convert_rows(rest[:n_cast], rest[n_cast + 4:])
    tm = x_ref.shape[0]
    x = x_ref[...]
    r = lax.rsqrt(jnp.mean(x * x, axis=-1, keepdims=True) + NORM_EPS)
    hn = (x * r * g_ref[...]).astype(BF16)
    proj = jnp.dot(hn, w_ref[...].astype(BF16), preferred_element_type=F32)
    cos, sa, sb = cos_ref[...], sa_ref[...], sb_ref[...]
    half = ROT_DIM // 2

    def rope(xh):
        return xh * cos + pltpu.roll(xh, HEAD_W - half, 1) * sa + pltpu.roll(xh, half, 1) * sb

    for h in range(N_ATT_HEADS):
        lo, hi = h * HEAD_W, (h + 1) * HEAD_W
        q_ref[:, lo:hi] = rope(proj[:, lo:hi]) * (ATT_HEAD_DIM ** -0.5)
        k_ref[pl.ds(h, tm, stride=N_ATT_HEADS), :] = rope(proj[:, D_ATT + lo:D_ATT + hi])
        v_ref[pl.ds(h, tm, stride=N_ATT_HEADS), :] = proj[:, 2 * D_ATT + lo:2 * D_ATT + hi]
    u_ref[...] = proj[:, 3 * D_ATT:]


def _rope_tables(pos):
    half = ROT_DIM // 2
    inv = ROPE_THETA ** (-np.arange(0, ROT_DIM, 2, dtype=np.float64) / ROT_DIM)
    ang = np.asarray(pos, np.float64)[:, None] * inv[None, :]
    cos, sin = np.cos(ang), np.sin(ang)
    S = ang.shape[0]
    pad = np.zeros((S, ATT_HEAD_DIM - ROT_DIM))
    zh = np.zeros((S, half))
    comp = lambda a, b, c: np.concatenate([a, b, c], axis=1)
    two = lambda t: jnp.asarray(np.concatenate([t, t], axis=1), F32)
    return (two(comp(cos, cos, pad + 1.0)), two(comp(-sin, zh, pad)), two(comp(zh, sin, pad)))


def _in_proj(x, g, w, tables, tm, cast=()):
    B, S, D = x.shape
    ns = S // tm
    cast_specs = _convert_specs(cast, B * ns, lambda b, s: b * ns + s)
    row = lambda b, s: (b * ns + s, 0)
    tab = pl.BlockSpec((tm, HEAD_W), lambda b, s: (s, 0))
    q_spec = pl.BlockSpec((tm, D_ATT), row)
    kv_spec = pl.BlockSpec((tm * N_ATT_HEADS, HEAD_W), row)
    kv_shape = jax.ShapeDtypeStruct((B * S * N_ATT_HEADS, HEAD_W), F32)
    q, k, v, u, *converted = pl.pallas_call(
        _in_proj_kernel,
        grid=(B, ns),
        in_specs=[pl.BlockSpec((None, tm, D), lambda b, s: (b, s, 0)),
                  pl.BlockSpec((1, D), lambda b, s: (0, 0)),
                  pl.BlockSpec((D, D_PROJ), lambda b, s: (0, 0)),
                  tab, tab, tab] + cast_specs,
        out_specs=[q_spec, kv_spec, kv_spec, pl.BlockSpec((tm, D_SSM), lambda b, s: (s, b))] + cast_specs,
        out_shape=[jax.ShapeDtypeStruct((B * S, D_ATT), F32), kv_shape, kv_shape,
                   jax.ShapeDtypeStruct((S, B * D_SSM), F32)] + [jax.ShapeDtypeStruct(c.shape, BF16) for c in cast],
        compiler_params=_params("parallel", "parallel"),
        name="in_proj",
    )(x, g.reshape(1, D), w, *tables, *cast)
    return q, k, v, u, converted


def _prompt_tile(i, lam, g_col, q_ref, k_ref, v_ref, o_ref, ml_scr, acc_scr, tq):
    nh = N_ATT_HEADS
    feat = lax.broadcasted_iota(jnp.int32, (HEAD_W, tq), 0)
    units = [(h, c) for h in range(nh) for c in range(2)]
    hs = lambda h: slice(h * HEAD_W, (h + 1) * HEAD_W)
    q_t = []
    for h, c in units:
        qh_t = (q_ref[:, hs(h)] * LOG2_E).T
        own = (feat < ATT_HEAD_DIM) if c == 0 else (feat >= ATT_HEAD_DIM)
        q_t.append(jnp.where(own, qh_t, 0.0))

    n = range(len(units))
    rows = lambda start, size, h: pl.ds(pl.multiple_of(start, tq) * nh + h, size, stride=nh)

    def scores(start, size, masked):
        s = [jnp.dot(k_ref[rows(start, size, h), :], q_t[u], preferred_element_type=F32)
             for u, (h, c) in enumerate(units)]
        if masked:
            key = lax.broadcasted_iota(jnp.int32, (size, tq), 0)
            qry = lax.broadcasted_iota(jnp.int32, (size, tq), 1)
            s = [jnp.where(key <= qry, su, NEG_INF) for su in s]
        return s

    def softmax(s, carry):
        m_new = [jnp.maximum(carry[u][0], jnp.max(s[u], axis=0, keepdims=True)) for u in n]
        alpha = [jnp.exp2(carry[u][0] - m_new[u]) for u in n]
        return m_new, alpha, [jnp.exp2(s[u] - m_new[u]) for u in n]

    def update(start, size, carry, m_new, alpha, p):
        ones_rows = jnp.ones((8, size), F32)
        pv = [jnp.dot(jnp.concatenate([v_ref[rows(start, size, h), :].T, ones_rows], axis=0), p[u],
                      preferred_element_type=F32) for u, (h, c) in enumerate(units)]
        return tuple((m_new[u], alpha[u] * carry[u][1] + pv[u][HEAD_W:HEAD_W + 1],
                      alpha[u] * carry[u][2] + pv[u][:HEAD_W]) for u in n)

    nu = len(units)
    load = lambda: tuple((ml_scr[u:u + 1, :], ml_scr[nu + u:nu + u + 1, :], acc_scr[u]) for u in n)

    def store(carry):
        for u, (m, l, acc) in enumerate(carry):
            ml_scr[u:u + 1, :] = m
            ml_scr[nu + u:nu + u + 1, :] = l
            acc_scr[u] = acc

    def block(start, size):
        carry = load()
        store(update(start, size, carry, *softmax(scores(start, size, False), carry)))

    ml_scr[:nu, :] = jnp.full((nu, tq), NEG_INF, F32)
    ml_scr[nu:, :] = jnp.zeros((nu, tq), F32)
    acc_scr[...] = jnp.zeros(acc_scr.shape, F32)
    lax.fori_loop(0, i // 2, lambda j, _: (block(2 * j * tq, 2 * tq), 0)[1], 0)

    @pl.when(i % 2 == 1)
    def _():
        block((i - 1) * tq, tq)

    yield
    carry = load()
    s = scores(i * tq, tq, True)
    yield
    stats = softmax(s, carry)
    yield
    carry = update(i * tq, tq, carry, *stats)
    for h in range(nh):
        (_, l1, acc1), (_, l2, acc2) = carry[2 * h], carry[2 * h + 1]
        o_t = acc1 / l1 - lam * (acc2 / l2)
        r = lax.rsqrt(jnp.mean(o_t * o_t, axis=0, keepdims=True) + SUBLN_EPS)
        o_ref[:, h * HEAD_W:(h + 1) * HEAD_W] = ((o_t * r * g_col) * (1.0 - LAMBDA_INIT)).T


def _decode_step(j, n_steps, lam, g_row, q_ref, kn_ref, vn_ref, k_refs, v_refs, o_ref, m_scr, l_scr, acc_scr):
    H = N_ATT_HEADS

    @pl.when(j == 0)
    def _():
        m_scr[...] = jnp.full(m_scr.shape, NEG_INF, F32)
        l_scr[...] = jnp.zeros(l_scr.shape, F32)
        acc_scr[...] = jnp.zeros(acc_scr.shape, F32)

    q = q_ref[...]
    lane = lax.broadcasted_iota(jnp.int32, q.shape, 1)
    q8 = jnp.concatenate([jnp.where(lane < ATT_HEAD_DIM, q, 0.0), jnp.where(lane >= ATT_HEAD_DIM, q, 0.0)], axis=0)
    rows = k_refs[0].shape[0]
    zero = jnp.zeros_like(q8)
    pad = jnp.zeros((HEAD_W - 4 * H, 2 * HEAD_W), F32)
    q_pair = jnp.concatenate([jnp.concatenate([q8, zero], axis=1), jnp.concatenate([zero, q8], axis=1), pad], axis=0)
    r = lax.broadcasted_iota(jnp.int32, (4 * H, rows), 0)
    col = lax.broadcasted_iota(jnp.int32, (4 * H, rows), 1)
    own = (col % H) == (r % H)
    both = lambda x: jnp.concatenate([x, x], axis=0)
    halves = lambda x, op: op(x[:2 * H], x[2 * H:])

    def scores(group):
        k_pairs = jnp.concatenate([jnp.concatenate([k_refs[a][...], k_refs[a + 1][...]], axis=1) for a in group],
                                  axis=0)
        s_t = lax.dot_general(k_pairs, q_pair, (((1,), (1,)), ((), ())), preferred_element_type=F32)
        return jnp.concatenate([jnp.where(own, s_t[n * rows:(n + 1) * rows].T[:4 * H], NEG_INF)
                                for n in range(len(group))], axis=1)

    def weighted_values(group, p):
        pv = None
        for n, a in enumerate(group):
            blk = slice(n * rows, (n + 1) * rows)
            part = (jnp.dot(p[:2 * H, blk], v_refs[a][...], preferred_element_type=F32)
                    + jnp.dot(p[2 * H:, blk], v_refs[a + 1][...], preferred_element_type=F32))
            pv = part if pv is None else pv + part
        return pv

    pairs = list(range(0, len(k_refs), 2))
    groups = [pairs[g::DECODE_GROUPS] for g in range(DECODE_GROUPS)]
    s = [scores(g) for g in groups]
    yield
    m_g = [halves(jnp.max(sg, axis=-1, keepdims=True), jnp.maximum) for sg in s]
    p = [jnp.exp(sg - both(mg)) for sg, mg in zip(s, m_g)]
    l_g = [halves(jnp.sum(pg, axis=-1, keepdims=True), jnp.add) for pg in p]
    yield
    pv_g = [weighted_values(g, pg) for g, pg in zip(groups, p)]
    m = m_scr[...]
    m_new = functools.reduce(jnp.maximum, m_g, m)
    alpha = jnp.exp(m - m_new)
    l, acc = alpha * l_scr[...], alpha * acc_scr[...]
    for mg, lg, pvg in zip(m_g, l_g, pv_g):
        w = jnp.exp(mg - m_new)
        l, acc = l + w * lg, acc + w * pvg
    m_scr[...] = m_new
    l_scr[...] = l
    acc_scr[...] = acc

    @pl.when(j == n_steps - 1)
    def _():
        kn = jnp.concatenate([kn_ref[...], kn_ref[...]], axis=0)
        vn = jnp.concatenate([vn_ref[...], vn_ref[...]], axis=0)
        s_new = jnp.sum(q8 * kn, axis=-1, keepdims=True)
        m_old = m_scr[...]
        m_fin = jnp.maximum(m_old, s_new)
        a = jnp.exp(m_old - m_fin)
        p_new = jnp.exp(s_new - m_fin)
        l = a * l_scr[...] + p_new
        acc = (a * acc_scr[...] + p_new * vn) / l
        o_ref[...] = _sub_ln(acc[:H] - lam * acc[H:], g_row)


def _tile_order(p, n):
    return jnp.where(p % 2 == 0, n - 1 - p // 2, p // 2)


def _attn_kernel(pt_ref, lq1_ref, lk1_ref, lq2_ref, lk2_ref, gcol_ref, grow_ref, q_ref, k_ref, v_ref,
                 qs_ref, kn_ref, vn_ref, ck_hbm, cv_hbm, o_ref, os_ref, m_scr, l_scr, acc_scr, kbuf, vbuf, sem,
                 pml_scr, pacc_scr,
                 *, tq, steps_per_seq, pages):
    i = _tile_order(pl.program_id(1), pl.num_programs(1))
    step = pl.program_id(0) * pl.num_programs(1) + pl.program_id(1)
    n_steps = pl.num_programs(0) * pl.num_programs(1)

    def start_fetch(of_step, unrolled, first=0, last=pages):
        slot = of_step % 2

        def one(p):
            pid = pt_ref[of_step // steps_per_seq, (of_step % steps_per_seq) * pages + p]
            pltpu.make_async_copy(ck_hbm.at[pid], kbuf.at[slot, p], sem.at[slot, 0]).start()
            pltpu.make_async_copy(cv_hbm.at[pid], vbuf.at[slot, p], sem.at[slot, 1]).start()
        if unrolled:
            for p in range(first, last):
                one(p)
        else:
            lax.fori_loop(first, last, lambda p, _: (one(p), 0)[1], 0)

    def wait_fetch(of_step):
        slot = of_step % 2
        pltpu.make_async_copy(ck_hbm.at[pl.ds(0, pages)], kbuf.at[slot], sem.at[slot, 0]).wait()
        pltpu.make_async_copy(cv_hbm.at[pl.ds(0, pages)], vbuf.at[slot], sem.at[slot, 1]).wait()

    @pl.when(step == 0)
    def _():
        start_fetch(step, unrolled=False)

    following = lax.rem(step + 1, n_steps)
    start_fetch(following, unrolled=True, last=pages // 2)
    lam = _lambda(lq1_ref, lk1_ref, lq2_ref, lk2_ref)
    prompt = _prompt_tile(i, lam, gcol_ref[...], q_ref, k_ref, v_ref, o_ref, pml_scr, pacc_scr, tq)
    next(prompt)
    wait_fetch(step)
    start_fetch(following, unrolled=True, first=pages // 2)
    slot = step % 2
    decode = _decode_step(step % steps_per_seq, steps_per_seq, lam, grow_ref[...], qs_ref, kn_ref, vn_ref,
                          [kbuf.at[slot, p] for p in range(pages)], [vbuf.at[slot, p] for p in range(pages)],
                          os_ref, m_scr, l_scr, acc_scr)
    running = [prompt, decode]
    while running:
        for gen in list(running):
            if next(gen, StopIteration) is StopIteration:
                running.remove(gen)

    @pl.when(step == n_steps - 1)
    def _():
        wait_fetch(following)


def _attn(q, k, v, qs, ks, vs, cache_k, cache_v, page_table, lam_vecs, g, B, S, tq):
    H = N_ATT_HEADS
    nq = S // tq
    Bs, n_pages = page_table.shape
    n_pool, page = cache_k.shape[0], cache_k.shape[1]
    steps_per_seq = (B * nq) // Bs
    assert steps_per_seq * Bs == B * nq and n_pages % steps_per_seq == 0
    P = n_pages // steps_per_seq
    ck = cache_k.reshape(n_pool, page * H, HEAD_W)
    cv = cache_v.reshape(n_pool, page * H, HEAD_W)
    seq = lambda b, i: (b * nq + i) // steps_per_seq
    vec = pl.BlockSpec((1, ATT_HEAD_DIM), lambda b, i, pt: (0, 0))
    kv = pl.BlockSpec((S * H, HEAD_W), lambda b, i, pt: (b, 0))
    qo = pl.BlockSpec((tq, D_ATT), lambda b, i, pt: (b * nq + _tile_order(i, nq), 0))
    tok = pl.BlockSpec((None, H, HEAD_W), lambda b, i, pt: (seq(b, i), 0, 0))
    hbm = pl.BlockSpec(memory_space=pl.ANY)
    page_buf = pltpu.VMEM((2, P, page * H, HEAD_W), F32)
    grid_spec = pltpu.PrefetchScalarGridSpec(
        num_scalar_prefetch=1,
        grid=(B, nq),
        in_specs=[vec, vec, vec, vec, pl.BlockSpec((HEAD_W, 1), lambda b, i, pt: (0, 0)),
                  pl.BlockSpec((1, HEAD_W), lambda b, i, pt: (0, 0)), qo, kv, kv, tok, tok, tok, hbm, hbm],
        out_specs=[qo, tok],
        scratch_shapes=[pltpu.VMEM((2 * H, 1), F32), pltpu.VMEM((2 * H, 1), F32), pltpu.VMEM((2 * H, HEAD_W), F32),
                        page_buf, page_buf, pltpu.SemaphoreType.DMA((2, 2)),
                        pltpu.VMEM((4 * H, tq), F32), pltpu.VMEM((2 * H, HEAD_W, tq), F32)],
    )
    tok3 = lambda a: a.reshape(Bs, H, HEAD_W)
    o, o_s = pl.pallas_call(
        functools.partial(_attn_kernel, tq=tq, steps_per_seq=steps_per_seq, pages=P),
        grid_spec=grid_spec,
        out_shape=[jax.ShapeDtypeStruct((B * S, D_ATT), F32), jax.ShapeDtypeStruct((Bs, H, HEAD_W), F32)],
        compiler_params=_params("arbitrary", "arbitrary"),
        name="attn",
    )(page_table, *lam_vecs, g.reshape(HEAD_W, 1), g.reshape(1, HEAD_W), q, k, v, tok3(qs), tok3(ks), tok3(vs), ck, cv)
    return o, o_s.reshape(Bs, D_ATT)


SCAN_COLS = 1024
B_GROUPS = 16
C_GROUPS = 8


def _ssm_kernel(u_ref, bb_ref, lre_ref, lim_ref, c_ref, d_ref, wglu_ref, *refs, tc, bk, fresh):
    n_h0 = 0 if fresh else 2
    n_cast = (len(refs) - n_h0 - 5) // 2
    h0_refs, cast_refs = refs[:n_h0], refs[n_h0:n_h0 + n_cast]
    z_ref, hre_ref, him_ref = refs[n_h0 + n_cast:n_h0 + n_cast + 3]
    bu_scr, hs_scr = refs[-2:]
    _convert_rows(cast_refs, refs[n_h0 + n_cast + 3:-2])
    t0 = pl.program_id(0)

    @pl.when(t0 == 0)
    def _():
        for h_ref, h0_ref in zip((hre_ref, him_ref), h0_refs or (None, None)):
            h_ref[...] = jnp.zeros(h_ref.shape, F32) if fresh else h0_ref[...]

    rows = tc * bk
    u = u_ref[...].reshape(rows, D_SSM)
    ub = u.astype(BF16)
    part_w = B_GROUPS * SSM_STATE
    half_w = 2 * part_w
    n_half = N_SSM_GROUPS // B_GROUPS
    slab_w = C_GROUPS * SSM_STATE

    def input_map(half):
        u_half = ub[:, half * B_GROUPS * SSM_GROUP:(half + 1) * B_GROUPS * SSM_GROUP]
        bu = jnp.dot(u_half, bb_ref[half], preferred_element_type=F32)
        bu_scr[:, :, half * half_w:(half + 1) * half_w] = bu.reshape(tc, bk, half_w)

    def scan(half):
        for cg in range(part_w // SCAN_COLS):
            st = slice(half * part_w + cg * SCAN_COLS, half * part_w + (cg + 1) * SCAN_COLS)
            re = slice(half * half_w + cg * SCAN_COLS, half * half_w + (cg + 1) * SCAN_COLS)
            im = slice(re.start + part_w, re.stop + part_w)
            lr = jnp.broadcast_to(lre_ref[:, st], (bk, SCAN_COLS))
            li = jnp.broadcast_to(lim_ref[:, st], (bk, SCAN_COLS))
            hr, hi = hre_ref[:, st], him_ref[:, st]
            for t in range(tc):
                hr, hi = lr * hr - li * hi + bu_scr[t, :, re], lr * hi + li * hr + bu_scr[t, :, im]
                hs_scr[t, :, re] = hr
                hs_scr[t, :, im] = hi
            hre_ref[:, st] = hr
            him_ref[:, st] = hi

    def output_map(s):
        re0 = (s * C_GROUPS // B_GROUPS) * half_w + (s * C_GROUPS % B_GROUPS) * SSM_STATE
        h_re = hs_scr[:, :, re0:re0 + slab_w].reshape(rows, slab_w).astype(BF16)
        h_im = hs_scr[:, :, re0 + part_w:re0 + part_w + slab_w].reshape(rows, slab_w).astype(BF16)
        return (jnp.dot(h_re, c_ref[s, 0], preferred_element_type=F32)
                + jnp.dot(h_im, c_ref[s, 1], preferred_element_type=F32))

    slabs_per_half = B_GROUPS // C_GROUPS
    input_map(0)
    y_slabs = []
    for half in range(n_half):
        if half + 1 < n_half:
            input_map(half + 1)
        scan(half)
        y_slabs += [output_map(half * slabs_per_half + s) for s in range(slabs_per_half)]
    y = jnp.concatenate(y_slabs, axis=1) + d_ref[...] * u
    y = jax.nn.gelu(y)
    gate = jnp.dot(y.astype(BF16), wglu_ref[...], preferred_element_type=F32)
    z_ref[...] = (y * jax.nn.sigmoid(gate)).reshape(tc, bk, D_SSM)


def _ssm(u3, bb, lb_re, lb_im, c_full, d, w_glu, h0, tc, cast=()):
    S, bk, _ = u3.shape
    const = lambda shape: pl.BlockSpec(shape, lambda t: (0,) * len(shape))
    seq = pl.BlockSpec((tc, bk, D_SSM), lambda t: (t, 0, 0))
    h0 = tuple(h0 or ())
    cast_specs = _convert_specs(cast, S // tc, lambda t: t)
    z, h_re, h_im, *converted = pl.pallas_call(
        functools.partial(_ssm_kernel, tc=tc, bk=bk, fresh=not h0),
        grid=(S // tc,),
        in_specs=[seq, const(bb.shape), const((1, N_STATE)), const((1, N_STATE)),
                  const(c_full.shape), const((1, D_SSM)), const((D_SSM, D_SSM))] + [const((bk, N_STATE))] * len(h0)
                 + cast_specs,
        out_specs=[seq, const((bk, N_STATE)), const((bk, N_STATE))] + cast_specs,
        out_shape=[jax.ShapeDtypeStruct((S, bk, D_SSM), F32), jax.ShapeDtypeStruct((bk, N_STATE), F32),
                   jax.ShapeDtypeStruct((bk, N_STATE), F32)] + [jax.ShapeDtypeStruct(c.shape, BF16) for c in cast],
        scratch_shapes=[pltpu.VMEM((tc, bk, 2 * N_STATE), F32), pltpu.VMEM((tc, bk, 2 * N_STATE), F32)],
        compiler_params=_params("arbitrary"),
        name="ssm",
    )(u3, bb, lb_re.reshape(1, N_STATE), lb_im.reshape(1, N_STATE), c_full, d.reshape(1, D_SSM), w_glu, *h0, *cast)
    return z, h_re, h_im, converted


def _ffn_kernel(*refs, tm, decode):
    if decode:
        (x_ref, o_ref, z_ref, wo_ref, gn_ref, wu_ref, cw_ref, cb_ref, wd_ref, fn_ref, p2_ref, p1_ref,
         y_ref, a_out_ref, hid_scr) = refs
    else:
        (x_ref, o_ref, z_ref, wo_ref, gn_ref, wu_ref, cw_ref, cb_ref, wd_ref, fn_ref,
         y_ref, cs_ref, hid_scr, carry_scr) = refs

        @pl.when(pl.program_id(1) == 0)
        def _():
            carry_scr[...] = jnp.zeros(carry_scr.shape, F32)

    mix = (jnp.dot(o_ref[...].astype(BF16), wo_ref[:D_ATT, :], preferred_element_type=F32)
           + jnp.dot(z_ref[...].astype(BF16), wo_ref[D_ATT:, :], preferred_element_type=F32))
    x1 = x_ref[...] + mix
    r = lax.rsqrt(jnp.mean(x1 * x1, axis=-1, keepdims=True) + NORM_EPS)
    hn = (x1 * r * gn_ref[...]).astype(BF16)
    for j in range(N_FF_CHUNKS):
        cols = slice(j * FF_CHUNK, (j + 1) * FF_CHUNK)
        gcols = slice(D_FF + j * FF_CHUNK, D_FF + (j + 1) * FF_CHUNK)
        a = jnp.dot(hn, wu_ref[:, cols], preferred_element_type=F32)
        gate = jnp.dot(hn, wu_ref[:, gcols], preferred_element_type=F32)
        if decode:
            am2, am1 = p2_ref[:, cols], p1_ref[:, cols]
            a_out_ref[:, cols] = a
        else:
            prev = carry_scr[:, cols]
            p2 = jnp.broadcast_to(prev[CARRY_ROWS - 2:CARRY_ROWS - 1], a.shape)
            p1 = jnp.broadcast_to(prev[CARRY_ROWS - 1:], a.shape)
            row = lax.broadcasted_iota(jnp.int32, a.shape, 0)
            am1 = jnp.where(row < 1, p1, pltpu.roll(a, 1, 0))
            am2 = jnp.where(row < 1, p2, jnp.where(row < 2, p1, pltpu.roll(a, 2, 0)))
            carry_scr[:, cols] = a[tm - CARRY_ROWS:]
        cw = cw_ref[:, cols]
        c = cb_ref[:, cols] + (cw[0:1] * am2 + cw[1:2] * am1 + cw[2:3] * a)
        hid_scr[:, cols] = (jax.nn.silu(c) * gate).astype(BF16)
    acc = x1 + jnp.dot(hid_scr[...], wd_ref[...], preferred_element_type=F32)
    r2 = lax.rsqrt(jnp.mean(acc * acc, axis=-1, keepdims=True) + NORM_EPS)
    y_ref[...] = acc * r2 * fn_ref[...]
    if not decode:
        cs_ref[...] = carry_scr[CARRY_ROWS - (CONV_W - 1):, :]


def _ffn(x, o, z, weights, tm, prev=None):
    B, S, D = x.shape
    ns = S // tm
    decode = prev is not None
    wo, gn, wu, cw, cb, wd, fn = weights
    const = lambda shape: pl.BlockSpec(shape, lambda b, s: (0,) * len(shape), pipeline_mode=pl.Buffered(1))
    xs = pl.BlockSpec((None, tm, D), lambda b, s: (b, s, 0))
    in_specs = [xs, pl.BlockSpec((tm, D_ATT), lambda b, s: (b * ns + s, 0)),
                pl.BlockSpec((tm, D_SSM), lambda b, s: (s, b)),
                const((D, D)), const((1, D)), const((D, 2 * D_FF)), const((CONV_W, D_FF)), const((1, D_FF)),
                const((D_FF, D)), const((1, D))]
    args = [x, o, z, wo, gn, wu, cw, cb, wd, fn]
    scratch = [pltpu.VMEM((tm, D_FF), BF16)]
    if decode:
        rows = pl.BlockSpec((tm, D_FF), lambda b, s: (0, 0))
        in_specs += [rows, rows]
        args += list(prev)
        out_specs = [xs, rows]
        out_shape = [jax.ShapeDtypeStruct((B, S, D), F32), jax.ShapeDtypeStruct((tm, D_FF), F32)]
    else:
        out_specs = [xs, pl.BlockSpec((None, CONV_W - 1, D_FF), lambda b, s: (b, 0, 0))]
        out_shape = [jax.ShapeDtypeStruct((B, S, D), F32), jax.ShapeDtypeStruct((B, CONV_W - 1, D_FF), F32)]
        scratch.append(pltpu.VMEM((CARRY_ROWS, D_FF), F32))
    y, extra = pl.pallas_call(
        functools.partial(_ffn_kernel, tm=tm, decode=decode),
        grid=(B, ns),
        in_specs=in_specs, out_specs=out_specs, out_shape=out_shape, scratch_shapes=scratch,
        compiler_params=_params("parallel", "arbitrary"),
        name="ffn_decode" if decode else "ffn",
    )(*args)
    return y, extra


def kernel(x_prompt, x_sample, cache_k, cache_v, state_ssm_re, state_ssm_im, state_conv, page_table, norm_mix, w_in, lambda_q1, lambda_k1, lambda_q2, lambda_k2, subln_g, ssm_a_re, ssm_a_im, ssm_log_dt, ssm_b_re, ssm_b_im, ssm_c_re, ssm_c_im, ssm_d, w_glu, w_o, norm_ffn, w_up, conv_w, conv_b, w_down, final_norm):
    assert norm_mix.shape[0] == 1, "single layer"
    Bp, Sp, D = x_prompt.shape
    Bs = x_sample.shape[0]
    G, N, C = N_SSM_GROUPS, SSM_STATE, SSM_GROUP
    past_len = page_table.shape[1] * cache_k.shape[2]

    lam_vecs = [v.reshape(1, ATT_HEAD_DIM) for v in (lambda_q1[0], lambda_k1[0], lambda_q2[0], lambda_k2[0])]
    g_sub = subln_g[0].reshape(1, HEAD_W)

    lb_re, lb_im, bb_blk, c_blk = _ssm_prep(ssm_a_re[0], ssm_a_im[0], ssm_log_dt[0], ssm_b_re[0], ssm_b_im[0],
                                            ssm_c_re[0], ssm_c_im[0])

    tabs_p = _rope_tables(np.arange(Sp))
    q, k, v, u, (w_in_b, wglu_b) = _in_proj(x_prompt, norm_mix[0], w_in[0], tabs_p, tm=IN_PROJ_ROWS,
                                            cast=(w_in[0], w_glu[0]))
    tabs_s = _rope_tables(np.full((Bs,), past_len))
    qs, ks, vs, us, _ = _in_proj(x_sample.reshape(1, Bs, D), norm_mix[0], w_in_b, tabs_s, tm=Bs)

    o, os_ = _attn(q, k, v, qs, ks, vs, cache_k[0], cache_v[0], page_table, lam_vecs, g_sub, Bp, Sp, tq=ATTN_Q_ROWS)

    z, hre_p, him_p, (wo_b, wup_b, wdown_b) = _ssm(
        u.reshape(Sp, Bp, D_SSM), bb_blk, lb_re, lb_im, c_blk, ssm_d[0], wglu_b, None, tc=SSM_STEPS,
        cast=(w_o[0], w_up[0], w_down[0]))
    ffn_w = (wo_b, norm_ffn[0].reshape(1, D), wup_b, conv_w[0], conv_b[0].reshape(1, D_FF), wdown_b,
             final_norm.reshape(1, D))
    y_prompt, cs_p = _ffn(x_prompt, o, z.reshape(Sp, Bp * D_SSM), ffn_w, tm=FFN_ROWS)
    conv_prompt = cs_p

    zs, hre_s, him_s, _ = _ssm(us.reshape(1, Bs, D_SSM), bb_blk, lb_re, lb_im, c_blk, ssm_d[0], wglu_b,
                            (state_ssm_re[0].reshape(Bs, N_STATE), state_ssm_im[0].reshape(Bs, N_STATE)), tc=1)
    sc = state_conv[0]
    y_s, a_s = _ffn(x_sample.reshape(1, Bs, D), os_, zs.reshape(Bs, D_SSM), ffn_w, tm=Bs, prev=(sc[:, 0], sc[:, 1]))
    conv_sample = jnp.stack([sc[:, 1], a_s], axis=1)

    st = lambda h, b: h.reshape(1, b, G, N)
    return (y_prompt, y_s.reshape(Bs, 1, D),
            k.reshape(1, Bp, Sp, N_ATT_HEADS, HEAD_W), v.reshape(1, Bp, Sp, N_ATT_HEADS, HEAD_W),
            st(hre_p, Bp), st(him_p, Bp), conv_prompt[None],
            ks.reshape(1, Bs, 1, N_ATT_HEADS, HEAD_W), vs.reshape(1, Bs, 1, N_ATT_HEADS, HEAD_W),
            st(hre_s, Bs), st(him_s, Bs), conv_sample[None])
```
